```python
import jax
import jax.numpy as jnp
from jax import lax
import numpy as np

D_MODEL = 1024
BATCH = 16
SEQ = 4096
DEPTH = 2

CHUNK = 64
CONV_K = 4
EPS = 1e-6
H_A = 4
DK_A = 128
DV_A = 128
H_B = 4
DK_B = 64
DV_B = 128
H_C = 4
DK_C = 64
DV_C = 128
GLA_RANK = 16
GLA_TAU = 16.0
QK_A = H_A * DK_A
V_A = H_A * DV_A
QK_B = H_B * DK_B
V_B = H_B * DV_B
QK_C = H_C * DK_C
V_C = H_C * DV_C
N_BRANCH = 3
BRANCH_WIDTH = V_A
IN_SPLITS = (QK_A, QK_A, V_A, V_A, H_A, H_A,
             QK_B, QK_B, V_B, V_B, H_B, H_B,
             QK_C, QK_C, V_C, V_C, GLA_RANK)
IN_WIDTH = sum(IN_SPLITS)
D_FF = 2816
N_EXPERTS = 8
TOP_K = 2
D_FF_EXPERT = 3584
N_DENSE = (DEPTH + 1) // 2
N_MOE = DEPTH // 2

kernel_name = "hybrid_deltanet_mlstm_gla_moe_adaln"


def _rms(x, g):
    xf = x.astype(jnp.float32)
    y = xf * lax.rsqrt(jnp.mean(jnp.square(xf), axis=-1, keepdims=True) + EPS)
    return (y * g.astype(jnp.float32)).astype(x.dtype)


def _l2norm(x):
    return x * lax.rsqrt(jnp.sum(jnp.square(x), axis=-1, keepdims=True) + 1e-6)


def _causal_conv(x, w):
    return lax.conv_general_dilated(
        x, w[:, None, :].astype(x.dtype), window_strides=(1,), padding=[(CONV_K - 1, 0)],
        dimension_numbers=("NWC", "WIO", "NWC"), feature_group_count=x.shape[-1])


def _to_chunks(x, n_heads):
    b, s, _ = x.shape
    return x.reshape(b, s // CHUNK, CHUNK, n_heads, -1).transpose(0, 3, 1, 2, 4).astype(jnp.float32)


def _scalar_chunks(x):
    b, s, h = x.shape
    return x.reshape(b, s // CHUNK, CHUNK, h).transpose(0, 3, 1, 2).astype(jnp.float32)


def _from_chunks(o):
    b, h, n, l, d = o.shape
    return o.transpose(0, 2, 3, 1, 4).reshape(b, n * l, h, d)


def _causal_masks():
    pos = jnp.arange(CHUNK)
    return pos[:, None] >= pos[None, :], pos[:, None] > pos[None, :]


def _chunk_first(a):
    return jnp.moveaxis(a, 2, 0)


def _gated_delta_rule(q, k, v, g, beta):
    dk, dv = q.shape[-1], v.shape[-1]
    tri, strict = _causal_masks()
    q = _l2norm(q) * (dk ** -0.5)
    k = _l2norm(k)
    gc = jnp.cumsum(g, axis=-1)
    seg = gc[..., :, None] - gc[..., None, :]
    decay = jnp.where(tri, jnp.exp(jnp.where(tri, seg, 0.0)), 0.0)
    kb = k * beta[..., None]
    a_mat = jnp.where(strict, jnp.einsum("bhnid,bhnjd->bhnij", kb, k) * decay, 0.0)
    t_mat = a_mat + jnp.eye(CHUNK, dtype=a_mat.dtype)
    rhs = jnp.concatenate([v * beta[..., None], kb * jnp.exp(gc)[..., None]], axis=-1)
    sol = lax.linalg.triangular_solve(t_mat, rhs, left_side=True, lower=True, unit_diagonal=True)
    u, w = sol[..., :dv], sol[..., dv:]
    k_end = k * jnp.exp(gc[..., -1:] - gc)[..., None]
    g_end = jnp.exp(gc[..., -1])

    def step(s, inp):
        w_n, u_n, ke_n, ge_n = inp
        delta = u_n - jnp.einsum("bhlk,bhkv->bhlv", w_n, s)
        s_new = ge_n[..., None, None] * s + jnp.einsum("bhlk,bhlv->bhkv", ke_n, delta)
        return s_new, s

    s0 = jnp.zeros(q.shape[:2] + (dk, dv), jnp.float32)
    _, s_prev = lax.scan(step, s0, (_chunk_first(w), _chunk_first(u), _chunk_first(k_end), _chunk_first(g_end)))
    s_prev = jnp.moveaxis(s_prev, 0, 2)
    delta = u - jnp.einsum("bhnlk,bhnkv->bhnlv", w, s_prev)
    attn = jnp.einsum("bhnik,bhnjk->bhnij", q, k) * decay
    return (jnp.einsum("bhnlk,bhnkv->bhnlv", q * jnp.exp(gc)[..., None], s_prev)
            + jnp.einsum("bhnij,bhnjv->bhniv", attn, delta))


def _mlstm(q, k, v, i_pre, f_pre):
    dk = q.shape[-1]
    tri, _ = _causal_masks()
    k = k * (dk ** -0.5)
    bcum = jnp.cumsum(jax.nn.log_sigmoid(f_pre), axis=-1)
    dlog = jnp.where(tri, bcum[..., :, None] - bcum[..., None, :] + i_pre[..., None, :], -jnp.inf)
    m_intra = jnp.max(dlog, axis=-1)
    wts = jnp.exp(dlog - m_intra[..., None]) * jnp.einsum("bhnik,bhnjk->bhnij", q, k)
    num_intra = jnp.einsum("bhnij,bhnjv->bhniv", wts, v)
    den_intra = jnp.sum(wts, axis=-1)
    a_end = bcum[..., -1:] - bcum + i_pre
    m_chunk = jnp.max(a_end, axis=-1)
    k_w = k * jnp.exp(a_end - m_chunk[..., None])[..., None]
    kv_chunk = jnp.einsum("bhnlk,bhnlv->bhnkv", k_w, v)
    n_chunk = jnp.sum(k_w, axis=-2)
    b_end = bcum[..., -1]

    def step(carry, inp):
        c_s, n_s, m_s = carry
        kv_n, nc_n, mc_n, be_n = inp
        m_new = jnp.maximum(be_n + m_s, mc_n)
        sp = jnp.exp(be_n + m_s - m_new)
        sc = jnp.exp(mc_n - m_new)
        c_new = sp[..., None, None] * c_s + sc[..., None, None] * kv_n
        n_new = sp[..., None] * n_s + sc[..., None] * nc_n
        return (c_new, n_new, m_new), (c_s, n_s, m_s)

    bh = q.shape[:2]
    init = (jnp.zeros(bh + (dk, v.shape[-1]), jnp.float32), jnp.zeros(bh + (dk,), jnp.float32),
            jnp.zeros(bh, jnp.float32))
    _, (c_prev, n_prev, m_prev) = lax.scan(
        step, init, (_chunk_first(kv_chunk), _chunk_first(n_chunk), _chunk_first(m_chunk), _chunk_first(b_end)))
    c_prev = jnp.moveaxis(c_prev, 0, 2)
    n_prev = jnp.moveaxis(n_prev, 0, 2)
    m_prev = jnp.moveaxis(m_prev, 0, 2)[..., None]
    m_tot = jnp.maximum(bcum + m_prev, m_intra)
    s_inter = jnp.exp(bcum + m_prev - m_tot)
    s_intra = jnp.exp(m_intra - m_tot)
    num = (s_inter[..., None] * jnp.einsum("bhnlk,bhnkv->bhnlv", q, c_prev)
           + s_intra[..., None] * num_intra)
    den = s_inter * jnp.einsum("bhnlk,bhnk->bhnl", q, n_prev) + s_intra * den_intra
    return num / jnp.maximum(jnp.abs(den), jnp.exp(-m_tot))[..., None]


def _gla(q, k, v, log_a):
    dk = q.shape[-1]
    tri, _ = _causal_masks()
    q = q * (dk ** -0.5)
    bc = jnp.cumsum(log_a, axis=-2)
    ref = bc[..., CHUNK // 2:CHUNK // 2 + 1, :]
    attn = jnp.where(tri, jnp.einsum("bhnik,bhnjk->bhnij", q * jnp.exp(bc - ref), k * jnp.exp(ref - bc)), 0.0)
    o_intra = jnp.einsum("bhnij,bhnjv->bhniv", attn, v)
    k_end = k * jnp.exp(bc[..., -1:, :] - bc)
    kv_chunk = jnp.einsum("bhnlk,bhnlv->bhnkv", k_end, v)
    a_end = jnp.exp(bc[..., -1, :])

    def step(s, inp):
        kv_n, ae_n = inp
        return ae_n[..., None] * s + kv_n, s

    s0 = jnp.zeros(q.shape[:2] + (dk, v.shape[-1]), jnp.float32)
    _, s_prev = lax.scan(step, s0, (_chunk_first(kv_chunk), _chunk_first(a_end)))
    s_prev = jnp.moveaxis(s_prev, 0, 2)
    return jnp.einsum("bhnlk,bhnkv->bhnlv", q * jnp.exp(bc), s_prev) + o_intra


def _token_mixers(h, w_in, conv_a, a_log, dt_bias, norm_a, conv_b, b_i, b_f, norm_b,
                  w_gla2, b_gla, norm_c, w_br, w_mg, b_mg, w_o):
    bsz, seq, _ = h.shape
    f32 = jnp.float32
    proj = h @ w_in
    points = [int(p) for p in np.cumsum(IN_SPLITS)[:-1]]
    (q_a, k_a, v_a, z_a, a_a, b_a, q_b, k_b, v_b, o_b, i_b, f_b,
     q_c, k_c, v_c, z_c, g_c) = jnp.split(proj, points, axis=-1)

    qkv_a = jax.nn.silu(_causal_conv(jnp.concatenate([q_a, k_a, v_a], axis=-1), conv_a))
    q_a, k_a, v_a = jnp.split(qkv_a, [QK_A, 2 * QK_A], axis=-1)
    g_a = -jnp.exp(a_log.astype(f32)) * jax.nn.softplus(a_a.astype(f32) + dt_bias.astype(f32))
    beta_a = jax.nn.sigmoid(b_a.astype(f32))
    o_a = _gated_delta_rule(_to_chunks(q_a, H_A), _to_chunks(k_a, H_A), _to_chunks(v_a, H_A),
                            _scalar_chunks(g_a), _scalar_chunks(beta_a))
    y_a = (_rms(_from_chunks(o_a).astype(h.dtype), norm_a)
           * jax.nn.silu(z_a.reshape(bsz, seq, H_A, DV_A))).reshape(bsz, seq, V_A)

    qk_b = jax.nn.silu(_causal_conv(jnp.concatenate([q_b, k_b], axis=-1), conv_b))
    q_b, k_b = jnp.split(qk_b, [QK_B], axis=-1)
    h_b = _mlstm(_to_chunks(q_b, H_B), _to_chunks(k_b, H_B), _to_chunks(v_b, H_B),
                 _scalar_chunks(i_b.astype(f32) + b_i.astype(f32)),
                 _scalar_chunks(f_b.astype(f32) + b_f.astype(f32)))
    y_b = (jax.nn.sigmoid(o_b.reshape(bsz, seq, H_B, DV_B))
           * _rms(_from_chunks(h_b).astype(h.dtype), norm_b)).reshape(bsz, seq, V_B)

    log_a_c = jax.nn.log_sigmoid(g_c.astype(f32) @ w_gla2.astype(f32) + b_gla.astype(f32)) / GLA_TAU
    o_c = _gla(_to_chunks(q_c, H_C), _to_chunks(k_c, H_C), _to_chunks(v_c, H_C), _to_chunks(log_a_c, H_C))
    y_c = (_rms(_from_chunks(o_c).astype(h.dtype), norm_c)
           * jax.nn.silu(z_c.reshape(bsz, seq, H_C, DV_C))).reshape(bsz, seq, V_C)

    branches = (y_a, y_b, y_c)
    merged = jax.nn.sigmoid(h @ w_mg[0] + b_mg[0]) * (branches[0] @ w_br[0])
    for g in range(1, N_BRANCH):
        merged = merged + jax.nn.sigmoid(h @ w_mg[g] + b_mg[g]) * (branches[g] @ w_br[g])
    return merged @ w_o


def _swiglu(h, w1, w3, w2):
    return (jax.nn.silu(h @ w1) * (h @ w3)) @ w2


def _moe(h, w_router, b_router, w1, w3, w2):
    bsz, seq, d = h.shape
    t = h.reshape(-1, d)
    logits = (t @ w_router).astype(jnp.float32) + b_router.astype(jnp.float32)
    top_logit, top_idx = lax.top_k(logits, TOP_K)
    top_w = jax.nn.softmax(top_logit, axis=-1)
    combine = jnp.einsum("tk,tke->te", top_w,
                         jax.nn.one_hot(top_idx, N_EXPERTS, dtype=jnp.float32)).astype(t.dtype)
    out = jnp.zeros_like(t)
    for e in range(N_EXPERTS):
        out = out + combine[:, e:e + 1] * _swiglu(t, w1[e], w3[e], w2[e])
    return out.reshape(bsz, seq, d)


def setup_inputs(seed: int = 0) -> dict:
    key = jax.random.key(seed)
    ks = jax.random.split(key, 32)
    f32 = jnp.float32
    D = D_MODEL

    def nrm(k, shape, scale):
        return jax.random.normal(k, shape, f32) * scale

    gate_offset = jnp.concatenate([jnp.zeros((2 * D,), f32), jnp.ones((D,), f32),
                                   jnp.zeros((2 * D,), f32), jnp.ones((D,), f32)])
    dt = jnp.exp(jax.random.uniform(ks[9], (DEPTH, H_A), f32, np.log(1e-3), np.log(1e-1)))
    return {
        "x": nrm(ks[0], (BATCH, SEQ, D), 1.0),
        "c": nrm(ks[1], (BATCH, D), 1.0),
        "w_ada": nrm(ks[2], (DEPTH, D, 6 * D), 0.1 * D ** -0.5),
        "b_ada": nrm(ks[3], (DEPTH, 6 * D), 0.02) + gate_offset,
        "g_mix": 1.0 + nrm(ks[4], (DEPTH, D), 0.02),
        "g_ffn": 1.0 + nrm(ks[5], (DEPTH, D), 0.02),
        "g_final": 1.0 + nrm(ks[6], (D,), 0.02),
        "w_in": nrm(ks[7], (DEPTH, D, IN_WIDTH), D ** -0.5),
        "conv_a": nrm(ks[8], (DEPTH, CONV_K, 2 * QK_A + V_A), CONV_K ** -0.5),
        "a_log": jnp.log(jax.random.uniform(ks[10], (DEPTH, H_A), f32, 1.0, 16.0)),
        "dt_bias": dt + jnp.log(-jnp.expm1(-dt)),
        "norm_a": 1.0 + nrm(ks[11], (DEPTH, DV_A), 0.02),
        "conv_b": nrm(ks[12], (DEPTH, CONV_K, 2 * QK_B), CONV_K ** -0.5),
        "b_i": nrm(ks[13], (DEPTH, H_B), 0.1),
        "b_f": jax.random.uniform(ks[14], (DEPTH, H_B), f32, 3.0, 6.0),
        "norm_b": 1.0 + nrm(ks[15], (DEPTH, DV_B), 0.02),
        "w_gla2": nrm(ks[16], (DEPTH, GLA_RANK, QK_C), GLA_RANK ** -0.5),
        "b_gla": nrm(ks[17], (DEPTH, QK_C), 0.1),
        "norm_c": 1.0 + nrm(ks[18], (DEPTH, DV_C), 0.02),
        "w_br": nrm(ks[19], (DEPTH, N_BRANCH, BRANCH_WIDTH, D), BRANCH_WIDTH ** -0.5),
        "w_mg": nrm(ks[20], (DEPTH, N_BRANCH, D, D), D ** -0.5),
        "b_mg": nrm(ks[21], (DEPTH, N_BRANCH, D), 0.02),
        "w_o": nrm(ks[22], (DEPTH, D, D), D ** -0.5),
        "w1_d": nrm(ks[23], (N_DENSE, D, D_FF), D ** -0.5),
        "w3_d": nrm(ks[24], (N_DENSE, D, D_FF), D ** -0.5),
        "w2_d": nrm(ks[25], (N_DENSE, D_FF, D), D_FF ** -0.5),
        "w_router": nrm(ks[26], (N_MOE, D, N_EXPERTS), D ** -0.5),
        "b_router": nrm(ks[27], (N_MOE, N_EXPERTS), 0.01),
        "w1_e": nrm(ks[28], (N_MOE, N_EXPERTS, D, D_FF_EXPERT), D ** -0.5),
        "w3_e": nrm(ks[29], (N_MOE, N_EXPERTS, D, D_FF_EXPERT), D ** -0.5),
        "w2_e": nrm(ks[30], (N_MOE, N_EXPERTS, D_FF_EXPERT, D), D_FF_EXPERT ** -0.5),
    }


def reference(x, c, w_ada, b_ada, g_mix, g_ffn, g_final, w_in, conv_a, a_log, dt_bias, norm_a,
              conv_b, b_i, b_f, norm_b, w_gla2, b_gla, norm_c, w_br, w_mg, b_mg, w_o,
              w1_d, w3_d, w2_d, w_router, b_router, w1_e, w3_e, w2_e):
    c_act = jax.nn.silu(c)
    for layer in range(DEPTH):
        mod = c_act @ w_ada[layer] + b_ada[layer]
        sh1, sc1, gt1, sh2, sc2, gt2 = [m[:, None, :] for m in jnp.split(mod, 6, axis=-1)]
        h = _rms(x, g_mix[layer]) * (1 + sc1) + sh1
        x = x + gt1 * _token_mixers(h, w_in[layer], conv_a[layer], a_log[layer], dt_bias[layer], norm_a[layer],
                                    conv_b[layer], b_i[layer], b_f[layer], norm_b[layer], w_gla2[layer],
                                    b_gla[layer], norm_c[layer], w_br[layer], w_mg[layer], b_mg[layer],
                                    w_o[layer])
        h = _rms(x, g_ffn[layer]) * (1 + sc2) + sh2
        j = layer // 2
        if layer % 2 == 0:
            f = _swiglu(h, w1_d[j], w3_d[j], w2_d[j])
        else:
            f = _moe(h, w_router[j], b_router[j], w1_e[j], w3_e[j], w2_e[j])
        x = x + gt2 * f
    return _rms(x, g_final)
```

```python
import functools

import jax
import jax.numpy as jnp
from jax import lax
from jax.experimental import pallas as pl
from jax.experimental.pallas import tpu as pltpu

F32 = jnp.float32
BF16 = jnp.bfloat16

EPS = 1e-6
CHUNK = 64
CONV_K = 4
N_HEADS = 4
DV = 128
DK_A = 128
DK_B = 64
DK_C = 64
GLA_RANK = 16
GLA_TAU = 16.0
TOP_K = 2

V7X_MXU_DIM = 256
MIX_ROWS = V7X_MXU_DIM
LANES = 128
SUBLANES = 8
V7X_VMEM_LIMIT = 56 * 1024 * 1024

LANE_A_DECAY = 0
LANE_A_BETA = 4
LANE_B_IN = 8
LANE_B_FORGET = 12
LANE_C_GATE = 16

NN = (((1,), (0,)), ((), ()))
NT = (((1,), (1,)), ((), ()))
TN = (((0,), (0,)), ((), ()))


def _dot(a, b, dims=NN):
    return lax.dot_general(a, b, dims, preferred_element_type=F32)


def _bdot(a, b, dims=NN):
    return _dot(a.astype(BF16), b.astype(BF16), dims)


def _split3(x):
    hi = x.astype(BF16)
    r1 = x - hi.astype(F32)
    mid = r1.astype(BF16)
    lo = (r1 - mid.astype(F32)).astype(BF16)
    return hi, mid, lo


def _dot_exact_lhs(a, b, dims=NN):
    hi, mid, lo = _split3(b)
    return _dot(a, hi, dims) + _dot(a, mid, dims) + _dot(a, lo, dims)


def _dot_hi(a, b, dims=NN):
    a_hi = a.astype(BF16)
    a_lo = (a - a_hi.astype(F32)).astype(BF16)
    b_hi = b.astype(BF16)
    b_lo = (b - b_hi.astype(F32)).astype(BF16)
    return _dot(a_hi, b_hi, dims) + _dot(a_hi, b_lo, dims) + _dot(a_lo, b_hi, dims)


def _sigmoid(x):
    return 1.0 / (1.0 + jnp.exp(-x))


def _silu(x):
    return x * _sigmoid(x)


def _softplus(x):
    return jnp.maximum(x, 0.0) + jnp.log(1.0 + jnp.exp(-jnp.abs(x)))


def _log_sigmoid(x):
    return jnp.minimum(x, 0.0) - jnp.log(1.0 + jnp.exp(-jnp.abs(x)))


def _params(n_axes):
    return pltpu.CompilerParams(
        dimension_semantics=("arbitrary",) * n_axes,
        vmem_limit_bytes=V7X_VMEM_LIMIT,
    )


def _row_tile(t, want):
    tm = min(want, t)
    assert t % tm == 0, (t, tm)
    return tm


def _ada_kernel(c_ref, w_ref, b_ref, o_ref):
    c = c_ref[...]
    o_ref[0] = _bdot(_silu(c), w_ref[0]) + b_ref[0]


def _ada_call(c, w_ada, b_ada):
    depth, d, d6 = w_ada.shape
    bsz = c.shape[0]
    n_col = d6 // d
    return pl.pallas_call(
        _ada_kernel,
        grid=(depth, n_col),
        in_specs=[
            pl.BlockSpec((bsz, d), lambda l, n: (0, 0)),
            pl.BlockSpec((1, d, d), lambda l, n: (l, 0, n)),
            pl.BlockSpec((1, 1, d), lambda l, n: (l, 0, n)),
        ],
        out_specs=pl.BlockSpec((1, bsz, d), lambda l, n: (l, 0, n)),
        out_shape=jax.ShapeDtypeStruct((depth, bsz, d6), F32),
        compiler_params=_params(2),
        name="ada_mod",
    )(c, w_ada, b_ada.reshape(depth, 1, d6))


def _norm_mod(x, g, shift, scale):
    ms = jnp.mean(x * x, axis=-1, keepdims=True)
    return (x * lax.rsqrt(ms + EPS) * g) * (1.0 + scale) + shift


IN_WIDTHS = (1536, 512, 512, 512, 512, 512, 512, 512)


def _inproj_kernel(x_ref, mod_ref, g_ref, w_ref, wsh_ref, wsl_ref, h_ref, *out_refs):
    h = _norm_mod(x_ref[...], g_ref[...], mod_ref[0, 0:1, :], mod_ref[0, 1:2, :])
    hb = h.astype(BF16)
    h_ref[...] = hb.astype(h_ref.dtype)
    off = 0
    for o_ref, width in zip(out_refs[:-1], IN_WIDTHS):
        o_ref[...] = _dot(hb, w_ref[:, off:off + width]).astype(o_ref.dtype)
        off += width
    hl = (h - hb.astype(F32)).astype(BF16)
    wsh = wsh_ref[...]
    out_refs[-1][...] = _dot(hb, wsh) + _dot(hl, wsh) + _dot(hb, wsl_ref[...])


def _inproj_call(x, mod, g, w_big, ws_hi, ws_lo, seq):
    t, d = x.shape
    tm = _row_tile(seq, 512)
    per_seq = seq // tm
    n_big = w_big.shape[1]
    row = lambda i: (i, 0)
    const = lambda i: (0, 0)
    out_shapes = [jax.ShapeDtypeStruct((t, d), BF16)]
    out_specs = [pl.BlockSpec((tm, d), row)]
    for width in IN_WIDTHS:
        out_shapes.append(jax.ShapeDtypeStruct((t, width), BF16))
        out_specs.append(pl.BlockSpec((tm, width), row))
    out_shapes.append(jax.ShapeDtypeStruct((t, LANES), F32))
    out_specs.append(pl.BlockSpec((tm, LANES), row))
    return pl.pallas_call(
        _inproj_kernel,
        grid=(t // tm,),
        in_specs=[
            pl.BlockSpec((tm, d), row),
            pl.BlockSpec((1, 6, d), lambda i: (i // per_seq, 0, 0)),
            pl.BlockSpec((1, d), const),
            pl.BlockSpec((d, n_big), const),
            pl.BlockSpec((d, LANES), const),
            pl.BlockSpec((d, LANES), const),
        ],
        out_specs=out_specs,
        out_shape=out_shapes,
        compiler_params=_params(1),
        name="in_proj",
    )(x, mod, g, w_big, ws_hi, ws_lo)


def _conv_silu(x_ref, cw_ref, cbuf):
    rows = x_ref.shape[0]
    x = x_ref[...].astype(F32)
    cbuf[SUBLANES:SUBLANES + rows, :] = x
    acc = x * cw_ref[CONV_K - 1:CONV_K, :]
    for s in range(1, CONV_K):
        acc = acc + cbuf[SUBLANES - s:SUBLANES - s + rows, :] * cw_ref[CONV_K - 1 - s:CONV_K - s, :]
    cbuf[0:SUBLANES, :] = cbuf[rows:rows + SUBLANES, :]
    return _silu(acc)


def _chunk_masks(rows):
    r = lax.broadcasted_iota(jnp.int32, (rows, rows), 0)
    c = lax.broadcasted_iota(jnp.int32, (rows, rows), 1)
    same = jnp.right_shift(r, 6) == jnp.right_shift(c, 6)
    causal = jnp.logical_and(same, r >= c)
    strict = jnp.logical_and(same, r > c)
    return causal, strict


def _chunk_end_rows(x, rows):
    parts = []
    for c in range(rows // CHUNK):
        last = x[c * CHUNK + CHUNK - 1:c * CHUNK + CHUNK, :]
        parts.append(jnp.broadcast_to(last, (CHUNK, x.shape[1])))
    return jnp.concatenate(parts, axis=0)


def _head_rms(o, w_row):
    ms = jnp.mean(o * o, axis=-1, keepdims=True)
    return o * lax.rsqrt(ms + EPS) * w_row


def _solve_unit_lower(a, rhs):
    x = rhs - _bdot(a, rhs)
    p = _bdot(a, a)
    for it in range(5):
        x = x + _bdot(p, x)
        if it < 4:
            p = _bdot(p, p)
    return x


def _mixer_a_kernel(qkv_ref, z_ref, small_ref, cw_ref, pa_ref, nw_ref, y_ref, cbuf, s_scr):
    rows = qkv_ref.shape[0]
    n_chunks = rows // CHUNK
    hw = N_HEADS * DK_A

    @pl.when(pl.program_id(1) == 0)
    def _():
        cbuf[0:SUBLANES, :] = jnp.zeros((SUBLANES, cbuf.shape[1]), F32)
        s_scr[...] = jnp.zeros(s_scr.shape, F32)

    qkv = _conv_silu(qkv_ref, cw_ref, cbuf)
    causal, strict = _chunk_masks(rows)

    gates = small_ref[...]
    log_decay = pa_ref[0:1, :] * _softplus(gates + pa_ref[1:2, :])
    beta_all = _sigmoid(gates)
    gc = _dot_exact_lhs(causal.astype(BF16), log_decay)
    gc_t = gc.T
    gc_end = _chunk_end_rows(gc, rows)

    for h in range(N_HEADS):
        q = qkv[:, h * DK_A:(h + 1) * DK_A]
        k = qkv[:, hw + h * DK_A:hw + (h + 1) * DK_A]
        v = qkv[:, 2 * hw + h * DV:2 * hw + (h + 1) * DV]
        q = q * (lax.rsqrt(jnp.sum(q * q, axis=-1, keepdims=True) + 1e-6) * (DK_A ** -0.5))
        k = k * lax.rsqrt(jnp.sum(k * k, axis=-1, keepdims=True) + 1e-6)
        beta = beta_all[:, LANE_A_BETA + h:LANE_A_BETA + h + 1]
        g_col = gc[:, LANE_A_DECAY + h:LANE_A_DECAY + h + 1]
        g_row = gc_t[LANE_A_DECAY + h:LANE_A_DECAY + h + 1, :]
        ge_col = gc_end[:, LANE_A_DECAY + h:LANE_A_DECAY + h + 1]
        decay = jnp.exp(jnp.where(causal, g_col - g_row, 0.0))
        kb = k * beta
        k16 = k.astype(BF16)
        a_mat = jnp.where(strict, _dot(kb.astype(BF16), k16, NT) * decay, 0.0)
        eg = jnp.exp(g_col)
        sol = _solve_unit_lower(a_mat, jnp.concatenate([v * beta, kb * eg], axis=1))
        u = sol[:, :DV]
        w = sol[:, DV:].astype(BF16)
        attn = jnp.where(causal, _dot(q.astype(BF16), k16, NT) * decay, 0.0)
        qg = (q * eg).astype(BF16)
        k_end = (k * jnp.exp(ge_col - g_col)).astype(BF16)
        s_decay = jnp.exp(ge_col)

        s = s_scr[h]
        deltas, inters = [], []
        for c in range(n_chunks):
            lo, hi = c * CHUNK, (c + 1) * CHUNK
            s16 = s.astype(BF16)
            delta = u[lo:hi] - _dot(w[lo:hi], s16)
            inters.append(_dot(qg[lo:hi], s16))
            sd = jnp.concatenate([s_decay[lo:hi]] * (DK_A // CHUNK), axis=0)
            s = sd * s + _dot(k_end[lo:hi], delta.astype(BF16), TN)
            deltas.append(delta)
        s_scr[h] = s
        delta = jnp.concatenate(deltas, axis=0)
        o = jnp.concatenate(inters, axis=0) + _bdot(attn, delta)
        z = z_ref[:, h * DV:(h + 1) * DV].astype(F32)
        y_ref[:, h * DV:(h + 1) * DV] = (_head_rms(o, nw_ref[...]) * _silu(z)).astype(y_ref.dtype)


def _mixer_a_call(qkv, z, small, conv_w, pa, norm_w, bsz, seq):
    per_seq = seq // MIX_ROWS
    cw = qkv.shape[1]
    row = lambda b, j: (b * per_seq + j, 0)
    const = lambda b, j: (0, 0)
    return pl.pallas_call(
        _mixer_a_kernel,
        grid=(bsz, per_seq),
        in_specs=[
            pl.BlockSpec((MIX_ROWS, cw), row),
            pl.BlockSpec((MIX_ROWS, N_HEADS * DV), row),
            pl.BlockSpec((MIX_ROWS, LANES), row),
            pl.BlockSpec((CONV_K, cw), const),
            pl.BlockSpec((SUBLANES, LANES), const),
            pl.BlockSpec((1, DV), const),
        ],
        out_specs=pl.BlockSpec((MIX_ROWS, N_HEADS * DV), row),
        out_shape=jax.ShapeDtypeStruct((bsz * seq, N_HEADS * DV), BF16),
        scratch_shapes=[
            pltpu.VMEM((MIX_ROWS + 2 * SUBLANES, cw), F32),
            pltpu.VMEM((N_HEADS, DK_A, DV), F32),
        ],
        compiler_params=_params(2),
        name="mixer_deltanet",
    )(qkv, z, small, conv_w, pa, norm_w)


def _mixer_b_kernel(qk_ref, v_ref, og_ref, small_ref, cw_ref, pb_ref, nw_ref, y_ref, cbuf, e_scr, m_scr):
    rows = qk_ref.shape[0]
    n_chunks = rows // CHUNK
    hw = N_HEADS * DK_B

    @pl.when(pl.program_id(1) == 0)
    def _():
        cbuf[0:SUBLANES, :] = jnp.zeros((SUBLANES, cbuf.shape[1]), F32)
        e_scr[...] = jnp.zeros(e_scr.shape, F32)
        m_scr[...] = jnp.zeros(m_scr.shape, F32)

    qk = _conv_silu(qk_ref, cw_ref, cbuf)
    causal, _ = _chunk_masks(rows)

    pre = small_ref[...] + pb_ref[0:1, :]
    log_f = _log_sigmoid(pre)
    bcum = _dot_exact_lhs(causal.astype(BF16), log_f)
    bcum_t = bcum.T
    pre_t = pre.T
    bend = _chunk_end_rows(bcum, rows)
    ones = jnp.ones((rows, DV), BF16)

    for h in range(N_HEADS):
        q = qk[:, h * DK_B:(h + 1) * DK_B].astype(BF16)
        k = qk[:, hw + h * DK_B:hw + (h + 1) * DK_B] * (DK_B ** -0.5)
        v_ext = jnp.concatenate([v_ref[:, h * DV:(h + 1) * DV], ones], axis=1)
        lf = LANE_B_FORGET + h
        li = LANE_B_IN + h
        b_col = bcum[:, lf:lf + 1]
        b_row = bcum_t[lf:lf + 1, :]
        i_row = pre_t[li:li + 1, :]
        i_col = pre[:, li:li + 1]
        be_col = bend[:, lf:lf + 1]
        dlog = jnp.where(causal, b_col - b_row + i_row, -jnp.inf)
        m_intra = jnp.max(dlog, axis=-1, keepdims=True)
        wts = jnp.exp(dlog - m_intra) * _dot(q, k.astype(BF16), NT)
        intra = _bdot(wts, v_ext)
        m_chunk = _chunk_end_rows(m_intra, rows)
        k_w = (k * jnp.exp(be_col - b_col + i_col - m_chunk)).astype(BF16)

        e = e_scr[h]
        m_s = m_scr[:, h:h + 1]
        outs = []
        for c in range(n_chunks):
            lo, hi = c * CHUNK, (c + 1) * CHUNK
            m_tot = jnp.maximum(b_col[lo:hi] + m_s, m_intra[lo:hi])
            s_inter = jnp.exp(b_col[lo:hi] + m_s - m_tot)
            s_intra = jnp.exp(m_intra[lo:hi] - m_tot)
            both = s_inter * _dot(q[lo:hi], e.astype(BF16)) + s_intra * intra[lo:hi]
            den = jnp.maximum(jnp.abs(both[:, DV:]), jnp.exp(-m_tot))
            outs.append(both[:, :DV] / den)
            m_new = jnp.maximum(be_col[lo:hi] + m_s, m_chunk[lo:hi])
            s_prev = jnp.exp(be_col[lo:hi] + m_s - m_new)
            s_cur = jnp.exp(m_chunk[lo:hi] - m_new)
            e = s_prev * e + s_cur * _dot(k_w[lo:hi], v_ext[lo:hi], TN)
            m_s = m_new
        e_scr[h] = e
        m_scr[:, h:h + 1] = m_s
        o = jnp.concatenate(outs, axis=0)
        og = og_ref[:, h * DV:(h + 1) * DV].astype(F32)
        y_ref[:, h * DV:(h + 1) * DV] = (_sigmoid(og) * _head_rms(o, nw_ref[...])).astype(y_ref.dtype)


def _mixer_b_call(qk, v, og, small, conv_w, pb, norm_w, bsz, seq):
    per_seq = seq // MIX_ROWS
    cw = qk.shape[1]
    row = lambda b, j: (b * per_seq + j, 0)
    const = lambda b, j: (0, 0)
    return pl.pallas_call(
        _mixer_b_kernel,
        grid=(bsz, per_seq),
        in_specs=[
            pl.BlockSpec((MIX_ROWS, cw), row),
            pl.BlockSpec((MIX_ROWS, N_HEADS * DV), row),
            pl.BlockSpec((MIX_ROWS, N_HEADS * DV), row),
            pl.BlockSpec((MIX_ROWS, LANES), row),
            pl.BlockSpec((CONV_K, cw), const),
            pl.BlockSpec((SUBLANES, LANES), const),
            pl.BlockSpec((1, DV), const),
        ],
        out_specs=pl.BlockSpec((MIX_ROWS, N_HEADS * DV), row),
        out_shape=jax.ShapeDtypeStruct((bsz * seq, N_HEADS * DV), BF16),
        scratch_shapes=[
            pltpu.VMEM((MIX_ROWS + 2 * SUBLANES, cw), F32),
            pltpu.VMEM((N_HEADS, DK_B, 2 * DV), F32),
            pltpu.VMEM((CHUNK, LANES), F32),
        ],
        compiler_params=_params(2),
        name="mixer_mlstm",
    )(qk, v, og, small, conv_w, pb, norm_w)


def _mixer_c_kernel(qk_ref, v_ref, z_ref, small_ref, wg_ref, bg_ref, nw_ref, y_ref, st_scr):
    rows = qk_ref.shape[0]
    n_chunks = rows // CHUNK
    hw = N_HEADS * DK_C

    @pl.when(pl.program_id(1) == 0)
    def _():
        st_scr[...] = jnp.zeros(st_scr.shape, F32)

    causal, _ = _chunk_masks(rows)
    log_a = _log_sigmoid(_dot_hi(small_ref[...], wg_ref[...]) + bg_ref[...]) * (1.0 / GLA_TAU)
    bc = _dot_exact_lhs(causal.astype(BF16), log_a)
    parts = []
    for c in range(n_chunks):
        mid = bc[c * CHUNK + CHUNK // 2:c * CHUNK + CHUNK // 2 + 1, :]
        parts.append(jnp.broadcast_to(mid, (CHUNK, hw)))
    rel = bc - jnp.concatenate(parts, axis=0)
    bc_end = _chunk_end_rows(bc, rows)
    qk = qk_ref[...].astype(F32)
    q_all = qk[:, :hw] * (DK_C ** -0.5)
    k_all = qk[:, hw:]
    q_rel = (q_all * jnp.exp(rel)).astype(BF16)
    k_rel = (k_all * jnp.exp(-rel)).astype(BF16)
    k_end = (k_all * jnp.exp(bc_end - bc)).astype(BF16)
    q_abs = (q_all * jnp.exp(bc)).astype(BF16)
    a_end = jnp.exp(bc_end)

    for h in range(N_HEADS):
        sl = slice(h * DK_C, (h + 1) * DK_C)
        v = v_ref[:, h * DV:(h + 1) * DV]
        attn = jnp.where(causal, _dot(q_rel[:, sl], k_rel[:, sl], NT), 0.0)
        o_intra = _bdot(attn, v)
        st = st_scr[h]
        inters = []
        for c in range(n_chunks):
            lo, hi = c * CHUNK, (c + 1) * CHUNK
            inters.append(_dot(q_abs[lo:hi, sl], st.astype(BF16), NT))
            st = st * a_end[lo:lo + 1, sl] + _dot(v[lo:hi], k_end[lo:hi, sl], TN)
        st_scr[h] = st
        o = jnp.concatenate(inters, axis=0) + o_intra
        z = z_ref[:, h * DV:(h + 1) * DV].astype(F32)
        y_ref[:, h * DV:(h + 1) * DV] = (_head_rms(o, nw_ref[...]) * _silu(z)).astype(y_ref.dtype)


def _mixer_c_call(qk, v, z, small, wg, bg, norm_w, bsz, seq):
    per_seq = seq // MIX_ROWS
    cw = qk.shape[1]
    row = lambda b, j: (b * per_seq + j, 0)
    const = lambda b, j: (0, 0)
    return pl.pallas_call(
        _mixer_c_kernel,
        grid=(bsz, per_seq),
        in_specs=[
            pl.BlockSpec((MIX_ROWS, cw), row),
            pl.BlockSpec((MIX_ROWS, N_HEADS * DV), row),
            pl.BlockSpec((MIX_ROWS, N_HEADS * DV), row),
            pl.BlockSpec((MIX_ROWS, LANES), row),
            pl.BlockSpec((LANES, N_HEADS * DK_C), const),
            pl.BlockSpec((1, N_HEADS * DK_C), const),
            pl.BlockSpec((1, DV), const),
        ],
        out_specs=pl.BlockSpec((MIX_ROWS, N_HEADS * DV), row),
        out_shape=jax.ShapeDtypeStruct((bsz * seq, N_HEADS * DV), BF16),
        scratch_shapes=[pltpu.VMEM((N_HEADS, DV, DK_C), F32)],
        compiler_params=_params(2),
        name="mixer_gla",
    )(qk, v, z, small, wg, bg, norm_w)


def _merge_kernel(x_ref, h_ref, ya_ref, yb_ref, yc_ref, mod_ref, wmg_ref, bmg_ref, wbr_ref, wo_ref, o_ref):
    h = h_ref[...]
    merged = None
    for g, y_ref in enumerate((ya_ref, yb_ref, yc_ref)):
        gate = _sigmoid(_dot(h, wmg_ref[g]) + bmg_ref[g])
        term = gate * _dot(y_ref[...], wbr_ref[g])
        merged = term if merged is None else merged + term
    o_ref[...] = x_ref[...] + mod_ref[0, 2:3, :] * _bdot(merged, wo_ref[...])


def _merge_call(x, h, ya, yb, yc, mod, w_mg, b_mg, w_br, w_o, seq):
    t, d = x.shape
    tm = _row_tile(seq, 512)
    per_seq = seq // tm
    bw = ya.shape[1]
    row = lambda i: (i, 0)
    return pl.pallas_call(
        _merge_kernel,
        grid=(t // tm,),
        in_specs=[
            pl.BlockSpec((tm, d), row),
            pl.BlockSpec((tm, d), row),
            pl.BlockSpec((tm, bw), row),
            pl.BlockSpec((tm, bw), row),
            pl.BlockSpec((tm, bw), row),
            pl.BlockSpec((1, 6, d), lambda i: (i // per_seq, 0, 0)),
            pl.BlockSpec((3, d, d), lambda i: (0, 0, 0)),
            pl.BlockSpec((3, 1, d), lambda i: (0, 0, 0)),
            pl.BlockSpec((3, bw, d), lambda i: (0, 0, 0)),
            pl.BlockSpec((d, d), lambda i: (0, 0)),
        ],
        out_specs=pl.BlockSpec((tm, d), row),
        out_shape=jax.ShapeDtypeStruct((t, d), F32),
        compiler_params=_params(1),
        name="merge",
    )(x, h, ya, yb, yc, mod, w_mg, b_mg, w_br, w_o)


def _prenorm_kernel(x_ref, mod_ref, g_ref, h_ref):
    h_ref[...] = _norm_mod(x_ref[...], g_ref[...], mod_ref[0, 3:4, :], mod_ref[0, 4:5, :]).astype(h_ref.dtype)


def _router_kernel(n_experts, x_ref, mod_ref, g_ref, wr_ref, br_ref, h_ref, comb_ref):
    h = _norm_mod(x_ref[...], g_ref[...], mod_ref[0, 3:4, :], mod_ref[0, 4:5, :])
    h_ref[...] = h.astype(h_ref.dtype)
    logits = _dot_hi(h, wr_ref[...]) + br_ref[...]
    lane = lax.broadcasted_iota(jnp.int32, logits.shape, 1)
    logits = jnp.where(lane < n_experts, logits, -jnp.inf)
    m1 = jnp.max(logits, axis=-1, keepdims=True)
    i1 = jnp.min(jnp.where(logits == m1, lane, LANES), axis=-1, keepdims=True)
    rest = jnp.where(lane == i1, -jnp.inf, logits)
    m2 = jnp.max(rest, axis=-1, keepdims=True)
    i2 = jnp.min(jnp.where(rest == m2, lane, LANES), axis=-1, keepdims=True)
    e2 = jnp.exp(m2 - m1)
    w1 = 1.0 / (1.0 + e2)
    comb_ref[...] = jnp.where(lane == i1, w1, 0.0) + jnp.where(lane == i2, e2 * w1, 0.0)


def _prenorm_call(x, mod, g, seq, router=None):
    t, d = x.shape
    tm = _row_tile(seq, 512)
    per_seq = seq // tm
    row = lambda i: (i, 0)
    const = lambda i: (0, 0)
    in_specs = [
        pl.BlockSpec((tm, d), row),
        pl.BlockSpec((1, 6, d), lambda i: (i // per_seq, 0, 0)),
        pl.BlockSpec((1, d), const),
    ]
    if router is None:
        return pl.pallas_call(
            _prenorm_kernel,
            grid=(t // tm,),
            in_specs=in_specs,
            out_specs=pl.BlockSpec((tm, d), row),
            out_shape=jax.ShapeDtypeStruct((t, d), BF16),
            compiler_params=_params(1),
            name="prenorm",
        )(x, mod, g)
    wr, br, n_experts = router
    return pl.pallas_call(
        functools.partial(_router_kernel, n_experts),
        grid=(t // tm,),
        in_specs=in_specs + [pl.BlockSpec((d, LANES), const), pl.BlockSpec((1, LANES), const)],
        out_specs=[pl.BlockSpec((tm, d), row), pl.BlockSpec((tm, LANES), row)],
        out_shape=[jax.ShapeDtypeStruct((t, d), BF16), jax.ShapeDtypeStruct((t, LANES), F32)],
        compiler_params=_params(1),
        name="prenorm_router",
    )(x, mod, g, wr, br)


def _ffn_kernel(use_comb, h_ref, comb_ref, w1_ref, w3_ref, w2_ref, x_ref, mod_ref, o_ref, acc_ref):
    e = pl.program_id(1)
    f = pl.program_id(2)

    @pl.when(jnp.logical_and(e == 0, f == 0))
    def _():
        acc_ref[...] = jnp.zeros(acc_ref.shape, F32)

    h = h_ref[...]
    a = _dot(h, w1_ref[0])
    g = _silu(a) * _dot(h, w3_ref[0])
    if use_comb:
        comb = comb_ref[...]
        lane = lax.broadcasted_iota(jnp.int32, comb.shape, 1)
        g = g * jnp.sum(jnp.where(lane == e, comb, 0.0), axis=-1, keepdims=True)
    acc_ref[...] += _bdot(g, w2_ref[0])

    @pl.when(jnp.logical_and(e == pl.num_programs(1) - 1, f == pl.num_programs(2) - 1))
    def _():
        o_ref[...] = x_ref[...] + mod_ref[0, 5:6, :] * acc_ref[...]


def _ffn_tile(d_ff):
    for tf in (512, 256, 128):
        if d_ff % tf == 0:
            return tf
    raise ValueError(d_ff)


def _ffn_call(h, comb, w1, w3, w2, x, mod, seq):
    t, d = x.shape
    n_e, _, d_ff = w1.shape
    tm = _row_tile(seq, 1024)
    per_seq = seq // tm
    tf = _ffn_tile(d_ff)
    use_comb = comb is not None
    if comb is None:
        comb = jnp.zeros((SUBLANES, LANES), F32)
        comb_spec = pl.BlockSpec((SUBLANES, LANES), lambda i, e, f: (0, 0))
    else:
        comb_spec = pl.BlockSpec((tm, LANES), lambda i, e, f: (i, 0))
    row = lambda i, e, f: (i, 0)
    return pl.pallas_call(
        functools.partial(_ffn_kernel, use_comb),
        grid=(t // tm, n_e, d_ff // tf),
        in_specs=[
            pl.BlockSpec((tm, d), row),
            comb_spec,
            pl.BlockSpec((1, d, tf), lambda i, e, f: (e, 0, f)),
            pl.BlockSpec((1, d, tf), lambda i, e, f: (e, 0, f)),
            pl.BlockSpec((1, tf, d), lambda i, e, f: (e, f, 0)),
            pl.BlockSpec((tm, d), row),
            pl.BlockSpec((1, 6, d), lambda i, e, f: (i // per_seq, 0, 0)),
        ],
        out_specs=pl.BlockSpec((tm, d), row),
        out_shape=jax.ShapeDtypeStruct((t, d), F32),
        scratch_shapes=[pltpu.VMEM((tm, d), F32)],
        compiler_params=_params(3),
        name="ffn_moe" if use_comb else "ffn_dense",
    )(h, comb, w1, w3, w2, x, mod)


def _final_kernel(x_ref, g_ref, o_ref):
    x = x_ref[...]
    ms = jnp.mean(x * x, axis=-1, keepdims=True)
    o_ref[...] = x * lax.rsqrt(ms + EPS) * g_ref[...]


def _final_call(x, g):
    t, d = x.shape
    tm = _row_tile(t, 512)
    return pl.pallas_call(
        _final_kernel,
        grid=(t // tm,),
        in_specs=[pl.BlockSpec((tm, d), lambda i: (i, 0)), pl.BlockSpec((1, d), lambda i: (0, 0))],
        out_specs=pl.BlockSpec((tm, d), lambda i: (i, 0)),
        out_shape=jax.ShapeDtypeStruct((t, d), F32),
        compiler_params=_params(1),
        name="final_norm",
    )(x, g)


def _lane_row(values, lane0, n_rows=1, row=0):
    out = jnp.zeros((n_rows, LANES), F32)
    return out.at[row, lane0:lane0 + values.shape[0]].set(values.astype(F32))


def _split_w_in(w_in):
    qk_a, v_a = N_HEADS * DK_A, N_HEADS * DV
    qk_b, qk_c, vw = N_HEADS * DK_B, N_HEADS * DK_C, N_HEADS * DV
    sizes = (qk_a, qk_a, v_a, v_a, N_HEADS, N_HEADS,
             qk_b, qk_b, vw, vw, N_HEADS, N_HEADS,
             qk_c, qk_c, vw, vw, GLA_RANK)
    names = ("q_a", "k_a", "v_a", "z_a", "a_a", "b_a", "q_b", "k_b", "v_b", "o_b", "i_b", "f_b",
             "q_c", "k_c", "v_c", "z_c", "g_c")
    assert sum(sizes) == w_in.shape[1]
    cols, off = {}, 0
    for name, size in zip(names, sizes):
        cols[name] = w_in[:, off:off + size]
        off += size
    big = jnp.concatenate([cols[n] for n in ("q_a", "k_a", "v_a", "z_a", "q_b", "k_b", "v_b", "o_b",
                                             "q_c", "k_c", "v_c", "z_c")], axis=1).astype(BF16)
    small = jnp.concatenate([cols[n] for n in ("a_a", "b_a", "i_b", "f_b", "g_c")], axis=1)
    small = jnp.pad(small, ((0, 0), (0, LANES - small.shape[1])))
    s_hi = small.astype(BF16)
    s_lo = (small - s_hi.astype(F32)).astype(BF16)
    return big, s_hi, s_lo


def kernel(x, c, w_ada, b_ada, g_mix, g_ffn, g_final, w_in, conv_a, a_log, dt_bias, norm_a, conv_b, b_i, b_f,
           norm_b, w_gla2, b_gla, norm_c, w_br, w_mg, b_mg, w_o, w1_d, w3_d, w2_d, w_router, b_router,
           w1_e, w3_e, w2_e):
    bsz, seq, d = x.shape
    depth = w_ada.shape[0]
    assert seq % MIX_ROWS == 0 and d % LANES == 0
    t = bsz * seq
    xf = x.reshape(t, d)
    mod_all = _ada_call(c, w_ada, b_ada).reshape(depth, bsz, 6, d)

    for layer in range(depth):
        mod = mod_all[layer]
        w_big, ws_hi, ws_lo = _split_w_in(w_in[layer])
        (h, qkv_a, z_a, qk_b, v_b, o_b, qk_c, v_c, z_c, small) = _inproj_call(
            xf, mod, g_mix[layer].reshape(1, d), w_big, ws_hi, ws_lo, seq)

        pa = (_lane_row(-jnp.exp(a_log[layer]), LANE_A_DECAY, SUBLANES, 0)
              + _lane_row(dt_bias[layer], LANE_A_DECAY, SUBLANES, 1))
        y_a = _mixer_a_call(qkv_a, z_a, small, conv_a[layer], pa, norm_a[layer].reshape(1, DV), bsz, seq)

        pb = (_lane_row(b_i[layer], LANE_B_IN, SUBLANES, 0) + _lane_row(b_f[layer], LANE_B_FORGET, SUBLANES, 0))
        y_b = _mixer_b_call(qk_b, v_b, o_b, small, conv_b[layer], pb, norm_b[layer].reshape(1, DV), bsz, seq)

        wg = jnp.zeros((LANES, N_HEADS * DK_C), F32).at[LANE_C_GATE:LANE_C_GATE + GLA_RANK].set(w_gla2[layer])
        y_c = _mixer_c_call(qk_c, v_c, z_c, small, wg, b_gla[layer].reshape(1, -1),
                            norm_c[layer].reshape(1, DV), bsz, seq)

        xf = _merge_call(xf, h, y_a, y_b, y_c, mod, w_mg[layer].astype(BF16), b_mg[layer][:, None, :],
                         w_br[layer].astype(BF16), w_o[layer].astype(BF16), seq)

        j = layer // 2
        if layer % 2 == 0:
            h2 = _prenorm_call(xf, mod, g_ffn[layer].reshape(1, d), seq)
            xf = _ffn_call(h2, None, w1_d[j][None].astype(BF16), w3_d[j][None].astype(BF16),
                           w2_d[j][None].astype(BF16), xf, mod, seq)
        else:
            n_experts = w_router.shape[2]
            wr = jnp.pad(w_router[j], ((0, 0), (0, LANES - n_experts)))
            br = jnp.pad(b_router[j], (0, LANES - n_experts)).reshape(1, LANES)
            h2, comb = _prenorm_call(xf, mod, g_ffn[layer].reshape(1, d), seq, router=(wr, br, n_experts))
            xf = _ffn_call(h2, comb, w1_e[j].astype(BF16), w3_e[j].astype(BF16), w2_e[j].astype(BF16),
                           xf, mod, seq)

    return _final_call(xf, g_final.reshape(1, d)).reshape(bsz, seq, d)
```

```python
import functools

import jax
import jax.numpy as jnp
from jax import lax
from jax.experimental import pallas as pl
from jax.experimental.pallas import tpu as pltpu

F32 = jnp.float32
BF16 = jnp.bfloat16

EPS = 1e-6
CHUNK = 64
CONV_K = 4
N_HEADS = 4
DV = 128
DK_A = 128
DK_B = 64
DK_C = 64
GLA_RANK = 16
GLA_TAU = 16.0
TOP_K = 2

V7X_MXU_DIM = 256
MIX_ROWS = V7X_MXU_DIM
LANES = 128
SUBLANES = 8
V7X_VMEM_LIMIT = 56 * 1024 * 1024

LANE_A_DECAY = 0
LANE_A_BETA = 4
LANE_B_IN = 8
LANE_B_FORGET = 12
LANE_C_GATE = 16

NN = (((1,), (0,)), ((), ()))
NT = (((1,), (1,)), ((), ()))
TN = (((0,), (0,)), ((), ()))


def _dot(a, b, dims=NN):
    return lax.dot_general(a, b, dims, preferred_element_type=F32)


def _bdot(a, b, dims=NN):
    return _dot(a.astype(BF16), b.astype(BF16), dims)


def _split3(x):
    hi = x.astype(BF16)
    r1 = x - hi.astype(F32)
    mid = r1.astype(BF16)
    lo = (r1 - mid.astype(F32)).astype(BF16)
    return hi, mid, lo


def _dot_exact_lhs(a, b, dims=NN):
    hi, mid, lo = _split3(b)
    return _dot(a, hi, dims) + _dot(a, mid, dims) + _dot(a, lo, dims)


def _dot_hi(a, b, dims=NN):
    a_hi = a.astype(BF16)
    a_lo = (a - a_hi.astype(F32)).astype(BF16)
    b_hi = b.astype(BF16)
    b_lo = (b - b_hi.astype(F32)).astype(BF16)
    return _dot(a_hi, b_hi, dims) + _dot(a_hi, b_lo, dims) + _dot(a_lo, b_hi, dims)


def _sigmoid(x):
    return 1.0 / (1.0 + jnp.exp(-x))


def _silu(x):
    return x * _sigmoid(x)


def _softplus(x):
    return jnp.maximum(x, 0.0) + jnp.log(1.0 + jnp.exp(-jnp.abs(x)))


def _log_sigmoid(x):
    return jnp.minimum(x, 0.0) - jnp.log(1.0 + jnp.exp(-jnp.abs(x)))


def _params(n_axes):
    return pltpu.CompilerParams(
        dimension_semantics=("arbitrary",) * n_axes,
        vmem_limit_bytes=V7X_VMEM_LIMIT,
    )


def _row_tile(t, want):
    tm = min(want, t)
    assert t % tm == 0, (t, tm)
    return tm


def _ada_kernel(c_ref, w_ref, b_ref, o_ref):
    c = c_ref[...]
    o_ref[0] = _bdot(_silu(c), w_ref[0]) + b_ref[0]


def _ada_call(c, w_ada, b_ada):
    depth, d, d6 = w_ada.shape
    bsz = c.shape[0]
    n_col = d6 // d
    return pl.pallas_call(
        _ada_kernel,
        grid=(depth, n_col),
        in_specs=[
            pl.BlockSpec((bsz, d), lambda l, n: (0, 0)),
            pl.BlockSpec((1, d, d), lambda l, n: (l, 0, n)),
            pl.BlockSpec((1, 1, d), lambda l, n: (l, 0, n)),
        ],
        out_specs=pl.BlockSpec((1, bsz, d), lambda l, n: (l, 0, n)),
        out_shape=jax.ShapeDtypeStruct((depth, bsz, d6), F32),
        compiler_params=_params(2),
        name="ada_mod",
    )(c, w_ada, b_ada.reshape(depth, 1, d6))


def _norm_mod(x, g, shift, scale):
    ms = jnp.mean(x * x, axis=-1, keepdims=True)
    return (x * lax.rsqrt(ms + EPS) * g) * (1.0 + scale) + shift


IN_WIDTHS = (1536, 512, 512, 512, 512, 512, 512, 512)


def _inproj_kernel(x_ref, mod_ref, g_ref, w_ref, wsh_ref, wsl_ref, h_ref, *out_refs):
    h = _norm_mod(x_ref[...], g_ref[...], mod_ref[0, 0:1, :], mod_ref[0, 1:2, :])
    hb = h.astype(BF16)
    h_ref[...] = hb.astype(h_ref.dtype)
    off = 0
    for o_ref, width in zip(out_refs[:-1], IN_WIDTHS):
        o_ref[...] = _dot(hb, w_ref[:, off:off + width]).astype(o_ref.dtype)
        off += width
    hl = (h - hb.astype(F32)).astype(BF16)
    wsh = wsh_ref[...]
    out_refs[-1][...] = _dot(hb, wsh) + _dot(hl, wsh) + _dot(hb, wsl_ref[...])


def _inproj_call(x, mod, g, w_big, ws_hi, ws_lo, seq):
    t, d = x.shape
    tm = _row_tile(seq, 512)
    per_seq = seq // tm
    n_big = w_big.shape[1]
    row = lambda i: (i, 0)
    const = lambda i: (0, 0)
    out_shapes = [jax.ShapeDtypeStruct((t, d), BF16)]
    out_specs = [pl.BlockSpec((tm, d), row)]
    for width in IN_WIDTHS:
        out_shapes.append(jax.ShapeDtypeStruct((t, width), BF16))
        out_specs.append(pl.BlockSpec((tm, width), row))
    out_shapes.append(jax.ShapeDtypeStruct((t, LANES), F32))
    out_specs.append(pl.BlockSpec((tm, LANES), row))
    return pl.pallas_call(
        _inproj_kernel,
        grid=(t // tm,),
        in_specs=[
            pl.BlockSpec((tm, d), row),
            pl.BlockSpec((1, 6, d), lambda i: (i // per_seq, 0, 0)),
            pl.BlockSpec((1, d), const),
            pl.BlockSpec((d, n_big), const),
            pl.BlockSpec((d, LANES), const),
            pl.BlockSpec((d, LANES), const),
        ],
        out_specs=out_specs,
        out_shape=out_shapes,
        compiler_params=_params(1),
        name="in_proj",
    )(x, mod, g, w_big, ws_hi, ws_lo)


def _conv_silu(x_ref, cw_ref, cbuf):
    rows = x_ref.shape[0]
    x = x_ref[...].astype(F32)
    cbuf[SUBLANES:SUBLANES + rows, :] = x
    acc = x * cw_ref[CONV_K - 1:CONV_K, :]
    for s in range(1, CONV_K):
        acc = acc + cbuf[SUBLANES - s:SUBLANES - s + rows, :] * cw_ref[CONV_K - 1 - s:CONV_K - s, :]
    cbuf[0:SUBLANES, :] = cbuf[rows:rows + SUBLANES, :]
    return _silu(acc)


def _chunk_masks(rows):
    r = lax.broadcasted_iota(jnp.int32, (rows, rows), 0)
    c = lax.broadcasted_iota(jnp.int32, (rows, rows), 1)
    same = jnp.right_shift(r, 6) == jnp.right_shift(c, 6)
    causal = jnp.logical_and(same, r >= c)
    strict = jnp.logical_and(same, r > c)
    return causal, strict


def _chunk_end_rows(x, rows):
    parts = []
    for c in range(rows // CHUNK):
        last = x[c * CHUNK + CHUNK - 1:c * CHUNK + CHUNK, :]
        parts.append(jnp.broadcast_to(last, (CHUNK, x.shape[1])))
    return jnp.concatenate(parts, axis=0)


def _head_rms(o, w_row):
    ms = jnp.mean(o * o, axis=-1, keepdims=True)
    return o * lax.rsqrt(ms + EPS) * w_row


def _solve_unit_lower(a, rhs):
    x = rhs - _bdot(a, rhs)
    p = _bdot(a, a)
    for it in range(5):
        x = x + _bdot(p, x)
        if it < 4:
            p = _bdot(p, p)
    return x


def _mixer_a_kernel(qkv_ref, z_ref, small_ref, cw_ref, pa_ref, nw_ref, y_ref, cbuf, s_scr):
    rows = qkv_ref.shape[0]
    n_chunks = rows // CHUNK
    hw = N_HEADS * DK_A

    @pl.when(pl.program_id(1) == 0)
    def _():
        cbuf[0:SUBLANES, :] = jnp.zeros((SUBLANES, cbuf.shape[1]), F32)
        s_scr[...] = jnp.zeros(s_scr.shape, F32)

    qkv = _conv_silu(qkv_ref, cw_ref, cbuf)
    causal, strict = _chunk_masks(rows)

    gates = small_ref[...]
    log_decay = pa_ref[0:1, :] * _softplus(gates + pa_ref[1:2, :])
    beta_all = _sigmoid(gates)
    gc = _dot_exact_lhs(causal.astype(BF16), log_decay)
    gc_t = gc.T
    gc_end = _chunk_end_rows(gc, rows)

    for h in range(N_HEADS):
        q = qkv[:, h * DK_A:(h + 1) * DK_A]
        k = qkv[:, hw + h * DK_A:hw + (h + 1) * DK_A]
        v = qkv[:, 2 * hw + h * DV:2 * hw + (h + 1) * DV]
        q = q * (lax.rsqrt(jnp.sum(q * q, axis=-1, keepdims=True) + 1e-6) * (DK_A ** -0.5))
        k = k * lax.rsqrt(jnp.sum(k * k, axis=-1, keepdims=True) + 1e-6)
        beta = beta_all[:, LANE_A_BETA + h:LANE_A_BETA + h + 1]
        g_col = gc[:, LANE_A_DECAY + h:LANE_A_DECAY + h + 1]
        g_row = gc_t[LANE_A_DECAY + h:LANE_A_DECAY + h + 1, :]
        ge_col = gc_end[:, LANE_A_DECAY + h:LANE_A_DECAY + h + 1]
        decay = jnp.exp(jnp.where(causal, g_col - g_row, 0.0))
        kb = k * beta
        k16 = k.astype(BF16)
        a_mat = jnp.where(strict, _dot(kb.astype(BF16), k16, NT) * decay, 0.0)
        eg = jnp.exp(g_col)
        sol = _solve_unit_lower(a_mat, jnp.concatenate([v * beta, kb * eg], axis=1))
        u = sol[:, :DV]
        w = sol[:, DV:].astype(BF16)
        attn = jnp.where(causal, _dot(q.astype(BF16), k16, NT) * decay, 0.0)
        qg = (q * eg).astype(BF16)
        k_end = (k * jnp.exp(ge_col - g_col)).astype(BF16)
        s_decay = jnp.exp(ge_col)

        s = s_scr[h]
        deltas, inters = [], []
        for c in range(n_chunks):
            lo, hi = c * CHUNK, (c + 1) * CHUNK
            s16 = s.astype(BF16)
            delta = u[lo:hi] - _dot(w[lo:hi], s16)
            inters.append(_dot(qg[lo:hi], s16))
            sd = jnp.concatenate([s_decay[lo:hi]] * (DK_A // CHUNK), axis=0)
            s = sd * s + _dot(k_end[lo:hi], delta.astype(BF16), TN)
            deltas.append(delta)
        s_scr[h] = s
        delta = jnp.concatenate(deltas, axis=0)
        o = jnp.concatenate(inters, axis=0) + _bdot(attn, delta)
        z = z_ref[:, h * DV:(h + 1) * DV].astype(F32)
        y_ref[:, h * DV:(h + 1) * DV] = (_head_rms(o, nw_ref[...]) * _silu(z)).astype(y_ref.dtype)


def _mixer_a_call(qkv, z, small, conv_w, pa, norm_w, bsz, seq):
    per_seq = seq // MIX_ROWS
    cw = qkv.shape[1]
    row = lambda b, j: (b * per_seq + j, 0)
    const = lambda b, j: (0, 0)
    return pl.pallas_call(
        _mixer_a_kernel,
        grid=(bsz, per_seq),
        in_specs=[
            pl.BlockSpec((MIX_ROWS, cw), row),
            pl.BlockSpec((MIX_ROWS, N_HEADS * DV), row),
            pl.BlockSpec((MIX_ROWS, LANES), row),
            pl.BlockSpec((CONV_K, cw), const),
            pl.BlockSpec((SUBLANES, LANES), const),
            pl.BlockSpec((1, DV), const),
        ],
        out_specs=pl.BlockSpec((MIX_ROWS, N_HEADS * DV), row),
        out_shape=jax.ShapeDtypeStruct((bsz * seq, N_HEADS * DV), BF16),
        scratch_shapes=[
            pltpu.VMEM((MIX_ROWS + 2 * SUBLANES, cw), F32),
            pltpu.VMEM((N_HEADS, DK_A, DV), F32),
        ],
        compiler_params=_params(2),
        name="mixer_deltanet",
    )(qkv, z, small, conv_w, pa, norm_w)


def _mixer_b_kernel(qk_ref, v_ref, og_ref, small_ref, cw_ref, pb_ref, nw_ref, y_ref, cbuf, e_scr, m_scr):
    rows = qk_ref.shape[0]
    n_chunks = rows // CHUNK
    hw = N_HEADS * DK_B

    @pl.when(pl.program_id(1) == 0)
    def _():
        cbuf[0:SUBLANES, :] = jnp.zeros((SUBLANES, cbuf.shape[1]), F32)
        e_scr[...] = jnp.zeros(e_scr.shape, F32)
        m_scr[...] = jnp.zeros(m_scr.shape, F32)

    qk = _conv_silu(qk_ref, cw_ref, cbuf)
    causal, _ = _chunk_masks(rows)

    pre = small_ref[...] + pb_ref[0:1, :]
    log_f = _log_sigmoid(pre)
    bcum = _dot_exact_lhs(causal.astype(BF16), log_f)
    bcum_t = bcum.T
    pre_t = pre.T
    bend = _chunk_end_rows(bcum, rows)
    ones = jnp.ones((rows, DV), BF16)

    for h in range(N_HEADS):
        q = qk[:, h * DK_B:(h + 1) * DK_B].astype(BF16)
        k = qk[:, hw + h * DK_B:hw + (h + 1) * DK_B] * (DK_B ** -0.5)
        v_ext = jnp.concatenate([v_ref[:, h * DV:(h + 1) * DV], ones], axis=1)
        lf = LANE_B_FORGET + h
        li = LANE_B_IN + h
        b_col = bcum[:, lf:lf + 1]
        b_row = bcum_t[lf:lf + 1, :]
        i_row = pre_t[li:li + 1, :]
        i_col = pre[:, li:li + 1]
        be_col = bend[:, lf:lf + 1]
        dlog = jnp.where(causal, b_col - b_row + i_row, -jnp.inf)
        m_intra = jnp.max(dlog, axis=-1, keepdims=True)
        wts = jnp.exp(dlog - m_intra) * _dot(q, k.astype(BF16), NT)
        intra = _bdot(wts, v_ext)
        m_chunk = _chunk_end_rows(m_intra, rows)
        k_w = (k * jnp.exp(be_col - b_col + i_col - m_chunk)).astype(BF16)

        e = e_scr[h]
        m_s = m_scr[:, h:h + 1]
        outs = []
        for c in range(n_chunks):
            lo, hi = c * CHUNK, (c + 1) * CHUNK
            m_tot = jnp.maximum(b_col[lo:hi] + m_s, m_intra[lo:hi])
            s_inter = jnp.exp(b_col[lo:hi] + m_s - m_tot)
            s_intra = jnp.exp(m_intra[lo:hi] - m_tot)
            both = s_inter * _dot(q[lo:hi], e.astype(BF16)) + s_intra * intra[lo:hi]
            den = jnp.maximum(jnp.abs(both[:, DV:]), jnp.exp(-m_tot))
            outs.append(both[:, :DV] / den)
            m_new = jnp.maximum(be_col[lo:hi] + m_s, m_chunk[lo:hi])
            s_prev = jnp.exp(be_col[lo:hi] + m_s - m_new)
            s_cur = jnp.exp(m_chunk[lo:hi] - m_new)
            e = s_prev * e + s_cur * _dot(k_w[lo:hi], v_ext[lo:hi], TN)
            m_s = m_new
        e_scr[h] = e
        m_scr[:, h:h + 1] = m_s
        o = jnp.concatenate(outs, axis=0)
        og = og_ref[:, h * DV:(h + 1) * DV].astype(F32)
        y_ref[:, h * DV:(h + 1) * DV] = (_sigmoid(og) * _head_rms(o, nw_ref[...])).astype(y_ref.dtype)


def _mixer_b_call(qk, v, og, small, conv_w, pb, norm_w, bsz, seq):
    per_seq = seq // MIX_ROWS
    cw = qk.shape[1]
    row = lambda b, j: (b * per_seq + j, 0)
    const = lambda b, j: (0, 0)
    return pl.pallas_call(
        _mixer_b_kernel,
        grid=(bsz, per_seq),
        in_specs=[
            pl.BlockSpec((MIX_ROWS, cw), row),
            pl.BlockSpec((MIX_ROWS, N_HEADS * DV), row),
            pl.BlockSpec((MIX_ROWS, N_HEADS * DV), row),
            pl.BlockSpec((MIX_ROWS, LANES), row),
            pl.BlockSpec((CONV_K, cw), const),
            pl.BlockSpec((SUBLANES, LANES), const),
            pl.BlockSpec((1, DV), const),
        ],
        out_specs=pl.BlockSpec((MIX_ROWS, N_HEADS * DV), row),
        out_shape=jax.ShapeDtypeStruct((bsz * seq, N_HEADS * DV), BF16),
        scratch_shapes=[
            pltpu.VMEM((MIX_ROWS + 2 * SUBLANES, cw), F32),
            pltpu.VMEM((N_HEADS, DK_B, 2 * DV), F32),
            pltpu.VMEM((CHUNK, LANES), F32),
        ],
        compiler_params=_params(2),
        name="mixer_mlstm",
    )(qk, v, og, small, conv_w, pb, norm_w)


def _mixer_c_kernel(qk_ref, v_ref, z_ref, small_ref, wg_ref, bg_ref, nw_ref, y_ref, st_scr):
    rows = qk_ref.shape[0]
    n_chunks = rows // CHUNK
    hw = N_HEADS * DK_C

    @pl.when(pl.program_id(1) == 0)
    def _():
        st_scr[...] = jnp.zeros(st_scr.shape, F32)

    causal, _ = _chunk_masks(rows)
    log_a = _log_sigmoid(_dot_hi(small_ref[...], wg_ref[...]) + bg_ref[...]) * (1.0 / GLA_TAU)
    bc = _dot_exact_lhs(causal.astype(BF16), log_a)
    parts = []
    for c in range(n_chunks):
        mid = bc[c * CHUNK + CHUNK // 2:c * CHUNK + CHUNK // 2 + 1, :]
        parts.append(jnp.broadcast_to(mid, (CHUNK, hw)))
    rel = bc - jnp.concatenate(parts, axis=0)
    bc_end = _chunk_end_rows(bc, rows)
    qk = qk_ref[...].astype(F32)
    q_all = qk[:, :hw] * (DK_C ** -0.5)
    k_all = qk[:, hw:]
    q_rel = (q_all * jnp.exp(rel)).astype(BF16)
    k_rel = (k_all * jnp.exp(-rel)).astype(BF16)
    k_end = (k_all * jnp.exp(bc_end - bc)).astype(BF16)
    q_abs = (q_all * jnp.exp(bc)).astype(BF16)
    a_end = jnp.exp(bc_end)

    for h in range(N_HEADS):
        sl = slice(h * DK_C, (h + 1) * DK_C)
        v = v_ref[:, h * DV:(h + 1) * DV]
        attn = jnp.where(causal, _dot(q_rel[:, sl], k_rel[:, sl], NT), 0.0)
        o_intra = _bdot(attn, v)
        st = st_scr[h]
        inters = []
        for c in range(n_chunks):
            lo, hi = c * CHUNK, (c + 1) * CHUNK
            inters.append(_dot(q_abs[lo:hi, sl], st.astype(BF16), NT))
            st = st * a_end[lo:lo + 1, sl] + _dot(v[lo:hi], k_end[lo:hi, sl], TN)
        st_scr[h] = st
        o = jnp.concatenate(inters, axis=0) + o_intra
        z = z_ref[:, h * DV:(h + 1) * DV].astype(F32)
        y_ref[:, h * DV:(h + 1) * DV] = (_head_rms(o, nw_ref[...]) * _silu(z)).astype(y_ref.dtype)


def _mixer_c_call(qk, v, z, small, wg, bg, norm_w, bsz, seq):
    per_seq = seq // MIX_ROWS
    cw = qk.shape[1]
    row = lambda b, j: (b * per_seq + j, 0)
    const = lambda b, j: (0, 0)
    return pl.pallas_call(
        _mixer_c_kernel,
        grid=(bsz, per_seq),
        in_specs=[
            pl.BlockSpec((MIX_ROWS, cw), row),
            pl.BlockSpec((MIX_ROWS, N_HEADS * DV), row),
            pl.BlockSpec((MIX_ROWS, N_HEADS * DV), row),
            pl.BlockSpec((MIX_ROWS, LANES), row),
            pl.BlockSpec((LANES, N_HEADS * DK_C), const),
            pl.BlockSpec((1, N_HEADS * DK_C), const),
            pl.BlockSpec((1, DV), const),
        ],
        out_specs=pl.BlockSpec((MIX_ROWS, N_HEADS * DV), row),
        out_shape=jax.ShapeDtypeStruct((bsz * seq, N_HEADS * DV), BF16),
        scratch_shapes=[pltpu.VMEM((N_HEADS, DV, DK_C), F32)],
        compiler_params=_params(2),
        name="mixer_gla",
    )(qk, v, z, small, wg, bg, norm_w)


def _merge_kernel(x_ref, h_ref, ya_ref, yb_ref, yc_ref, mod_ref, wmg_ref, bmg_ref, wbr_ref, wo_ref, o_ref):
    h = h_ref[...]
    merged = None
    for g, y_ref in enumerate((ya_ref, yb_ref, yc_ref)):
        gate = _sigmoid(_dot(h, wmg_ref[g]) + bmg_ref[g])
        term = gate * _dot(y_ref[...], wbr_ref[g])
        merged = term if merged is None else merged + term
    o_ref[...] = x_ref[...] + mod_ref[0, 2:3, :] * _bdot(merged, wo_ref[...])


def _merge_call(x, h, ya, yb, yc, mod, w_mg, b_mg, w_br, w_o, seq):
    t, d = x.shape
    tm = _row_tile(seq, 512)
    per_seq = seq // tm
    bw = ya.shape[1]
    row = lambda i: (i, 0)
    return pl.pallas_call(
        _merge_kernel,
        grid=(t // tm,),
        in_specs=[
            pl.BlockSpec((tm, d), row),
            pl.BlockSpec((tm, d), row),
            pl.BlockSpec((tm, bw), row),
            pl.BlockSpec((tm, bw), row),
            pl.BlockSpec((tm, bw), row),
            pl.BlockSpec((1, 6, d), lambda i: (i // per_seq, 0, 0)),
            pl.BlockSpec((3, d, d), lambda i: (0, 0, 0)),
            pl.BlockSpec((3, 1, d), lambda i: (0, 0, 0)),
            pl.BlockSpec((3, bw, d), lambda i: (0, 0, 0)),
            pl.BlockSpec((d, d), lambda i: (0, 0)),
        ],
        out_specs=pl.BlockSpec((tm, d), row),
        out_shape=jax.ShapeDtypeStruct((t, d), F32),
        compiler_params=_params(1),
        name="merge",
    )(x, h, ya, yb, yc, mod, w_mg, b_mg, w_br, w_o)


def _prenorm_kernel(x_ref, mod_ref, g_ref, h_ref):
    h_ref[...] = _norm_mod(x_ref[...], g_ref[...], mod_ref[0, 3:4, :], mod_ref[0, 4:5, :]).astype(h_ref.dtype)


ROUTE_E1, ROUTE_E2, ROUTE_W1, ROUTE_W2, ROUTE_R1, ROUTE_R2 = range(6)


def _router_kernel(n_experts, x_ref, mod_ref, g_ref, wr_ref, br_ref, h_ref, route_ref, count_ref, carry):
    @pl.when(pl.program_id(0) == 0)
    def _():
        carry[...] = jnp.zeros(carry.shape, F32)

    h = _norm_mod(x_ref[...], g_ref[...], mod_ref[0, 3:4, :], mod_ref[0, 4:5, :])
    h_ref[...] = h
    logits = _dot_hi(h, wr_ref[...]) + br_ref[...]
    tm = logits.shape[0]
    lane = lax.broadcasted_iota(jnp.int32, logits.shape, 1)
    logits = jnp.where(lane < n_experts, logits, -jnp.inf)
    m1 = jnp.max(logits, axis=-1, keepdims=True)
    i1 = jnp.min(jnp.where(logits == m1, lane, LANES), axis=-1, keepdims=True)
    rest = jnp.where(lane == i1, -jnp.inf, logits)
    m2 = jnp.max(rest, axis=-1, keepdims=True)
    i2 = jnp.min(jnp.where(rest == m2, lane, LANES), axis=-1, keepdims=True)
    e2 = jnp.exp(m2 - m1)
    w1 = 1.0 / (1.0 + e2)
    w2 = e2 * w1

    sel1 = lane == i1
    sel2 = lane == i2
    onehot = jnp.where(jnp.logical_or(sel1, sel2), 1.0, 0.0)
    r = lax.broadcasted_iota(jnp.int32, (tm, tm), 0)
    c = lax.broadcasted_iota(jnp.int32, (tm, tm), 1)
    before = jnp.where(r > c, 1.0, 0.0).astype(BF16)
    rank = _dot(before, onehot.astype(BF16)) + carry[0:1, :]
    carry[...] = carry[...] + jnp.sum(onehot, axis=0, keepdims=True)
    count_ref[...] = carry[...]
    r1 = jnp.sum(jnp.where(sel1, rank, 0.0), axis=-1, keepdims=True)
    r2 = jnp.sum(jnp.where(sel2, rank, 0.0), axis=-1, keepdims=True)
    rec = jnp.zeros(logits.shape, F32)
    for pos, val in ((ROUTE_E1, i1.astype(F32)), (ROUTE_E2, i2.astype(F32)), (ROUTE_W1, w1), (ROUTE_W2, w2),
                     (ROUTE_R1, r1), (ROUTE_R2, r2)):
        rec = jnp.where(lane == pos, val, rec)
    route_ref[...] = rec


def _prenorm_call(x, mod, g, seq, router=None):
    t, d = x.shape
    tm = _row_tile(seq, 512)
    per_seq = seq // tm
    row = lambda i: (i, 0)
    const = lambda i: (0, 0)
    in_specs = [
        pl.BlockSpec((tm, d), row),
        pl.BlockSpec((1, 6, d), lambda i: (i // per_seq, 0, 0)),
        pl.BlockSpec((1, d), const),
    ]
    if router is None:
        return pl.pallas_call(
            _prenorm_kernel,
            grid=(t // tm,),
            in_specs=in_specs,
            out_specs=pl.BlockSpec((tm, d), row),
            out_shape=jax.ShapeDtypeStruct((t, d), BF16),
            compiler_params=_params(1),
            name="prenorm",
        )(x, mod, g)
    wr, br, n_experts = router
    return pl.pallas_call(
        functools.partial(_router_kernel, n_experts),
        grid=(t // tm,),
        in_specs=in_specs + [pl.BlockSpec((d, LANES), const), pl.BlockSpec((1, LANES), const)],
        out_specs=[pl.BlockSpec((tm, d), row), pl.BlockSpec((tm, LANES), row),
                   pl.BlockSpec((SUBLANES, LANES), const)],
        out_shape=[jax.ShapeDtypeStruct((t, d), F32), jax.ShapeDtypeStruct((t, LANES), F32),
                   jax.ShapeDtypeStruct((SUBLANES, LANES), F32)],
        scratch_shapes=[pltpu.VMEM((SUBLANES, LANES), F32)],
        compiler_params=_params(1),
        name="prenorm_router",
    )(x, mod, g, wr, br)


MOE_ROWS = 1024
COMBINE_ROWS = 256


def _row_gather_start(idx_ref, src_hbm, dst_ref, sem, n_rows):
    def issue(r, carry):
        pltpu.make_async_copy(src_hbm.at[pl.ds(idx_ref[r], 1)], dst_ref.at[pl.ds(r, 1)], sem).start()
        return carry

    lax.fori_loop(0, n_rows, issue, 0, unroll=8)


def _row_gather_wait(src_hbm, dst_ref, sem, n_rows):
    pltpu.make_async_copy(src_hbm.at[pl.ds(0, n_rows)], dst_ref, sem).wait()


def _moe_gather_kernel(tok_ref, src_hbm, o_ref, sem):
    rows = o_ref.shape[0]
    _row_gather_start(tok_ref, src_hbm, o_ref, sem, rows)
    _row_gather_wait(src_hbm, o_ref, sem, rows)


def _moe_gather_call(tok, src):
    p = tok.shape[0]
    d = src.shape[1]
    return pl.pallas_call(
        _moe_gather_kernel,
        grid=(p // MOE_ROWS,),
        in_specs=[
            pl.BlockSpec((MOE_ROWS,), lambda i: (i,), memory_space=pltpu.SMEM),
            pl.BlockSpec(memory_space=pl.ANY),
        ],
        out_specs=pl.BlockSpec((MOE_ROWS, d), lambda i: (i, 0)),
        out_shape=jax.ShapeDtypeStruct((p, d), src.dtype),
        scratch_shapes=[pltpu.SemaphoreType.DMA(())],
        compiler_params=_params(1),
        name="moe_gather",
    )(tok, src)


def _moe_ffn_kernel(te_ref, nu_ref, xs_ref, w1_ref, w3_ref, w2_ref, ys_ref, hb_scr, acc_ref):
    i = pl.program_id(0)
    f = pl.program_id(1)
    last = pl.num_programs(1) - 1
    used = i < nu_ref[0]

    @pl.when(jnp.logical_and(used, f == 0))
    def _():
        hb_scr[...] = xs_ref[...].astype(BF16)
        acc_ref[...] = jnp.zeros(acc_ref.shape, F32)

    @pl.when(used)
    def _():
        h = hb_scr[...]
        g = _silu(_dot(h, w1_ref[0])) * _dot(h, w3_ref[0])
        acc_ref[...] += _bdot(g, w2_ref[0])

    @pl.when(jnp.logical_and(used, f == last))
    def _():
        ys_ref[...] = acc_ref[...]

    @pl.when(jnp.logical_and(jnp.logical_not(used), f == last))
    def _():
        ys_ref[...] = jnp.zeros(ys_ref.shape, F32)


def _moe_ffn_call(tile_expert, n_used, xs, w1, w3, w2):
    p, d = xs.shape
    d_ff = w1.shape[2]
    tf = _ffn_tile(d_ff)
    grid_spec = pltpu.PrefetchScalarGridSpec(
        num_scalar_prefetch=2,
        grid=(p // MOE_ROWS, d_ff // tf),
        in_specs=[
            pl.BlockSpec((MOE_ROWS, d), lambda i, f, te, nu: (i, 0)),
            pl.BlockSpec((1, d, tf), lambda i, f, te, nu: (te[i], 0, f)),
            pl.BlockSpec((1, d, tf), lambda i, f, te, nu: (te[i], 0, f)),
            pl.BlockSpec((1, tf, d), lambda i, f, te, nu: (te[i], f, 0)),
        ],
        out_specs=pl.BlockSpec((MOE_ROWS, d), lambda i, f, te, nu: (i, 0)),
        scratch_shapes=[pltpu.VMEM((MOE_ROWS, d), BF16), pltpu.VMEM((MOE_ROWS, d), F32)],
    )
    return pl.pallas_call(
        _moe_ffn_kernel,
        grid_spec=grid_spec,
        out_shape=jax.ShapeDtypeStruct((p, d), F32),
        compiler_params=_params(2),
        name="moe_ffn",
    )(tile_expert, n_used, xs, w1, w3, w2)


def _moe_combine_kernel(final, d1_ref, d2_ref, ys_hbm, route_ref, x_ref, mod_ref, gf_ref, o_ref, buf, sem):
    rows = x_ref.shape[0]
    _row_gather_start(d1_ref, ys_hbm, buf.at[0], sem.at[0], rows)
    _row_gather_start(d2_ref, ys_hbm, buf.at[1], sem.at[1], rows)
    _row_gather_wait(ys_hbm, buf.at[0], sem.at[0], rows)
    _row_gather_wait(ys_hbm, buf.at[1], sem.at[1], rows)
    route = route_ref[...]
    f = route[:, ROUTE_W1:ROUTE_W1 + 1] * buf[0] + route[:, ROUTE_W2:ROUTE_W2 + 1] * buf[1]
    y = x_ref[...] + mod_ref[0, 5:6, :] * f
    if final:
        ms = jnp.mean(y * y, axis=-1, keepdims=True)
        y = y * lax.rsqrt(ms + EPS) * gf_ref[...]
    o_ref[...] = y


def _moe_combine_call(d1, d2, ys, route, x, mod, g_final, seq, final):
    t, d = x.shape
    tm = _row_tile(seq, COMBINE_ROWS)
    per_seq = seq // tm
    row = lambda i: (i, 0)
    return pl.pallas_call(
        functools.partial(_moe_combine_kernel, final),
        grid=(t // tm,),
        in_specs=[
            pl.BlockSpec((tm,), lambda i: (i,), memory_space=pltpu.SMEM),
            pl.BlockSpec((tm,), lambda i: (i,), memory_space=pltpu.SMEM),
            pl.BlockSpec(memory_space=pl.ANY),
            pl.BlockSpec((tm, LANES), row),
            pl.BlockSpec((tm, d), row),
            pl.BlockSpec((1, 6, d), lambda i: (i // per_seq, 0, 0)),
            pl.BlockSpec((1, d), lambda i: (0, 0)),
        ],
        out_specs=pl.BlockSpec((tm, d), row),
        out_shape=jax.ShapeDtypeStruct((t, d), F32),
        scratch_shapes=[pltpu.VMEM((2, tm, d), F32), pltpu.SemaphoreType.DMA((2,))],
        compiler_params=_params(1),
        name="moe_combine",
    )(d1, d2, ys, route, x, mod, g_final)


def _moe_layout(route, counts, n_experts, t):
    cnt = counts[0, :n_experts].astype(jnp.int32)
    padded = ((cnt + MOE_ROWS - 1) // MOE_ROWS) * MOE_ROWS
    ends = jnp.cumsum(padded)
    offs = ends - padded
    e1 = route[:, ROUTE_E1].astype(jnp.int32)
    e2 = route[:, ROUTE_E2].astype(jnp.int32)
    d1 = offs[e1] + route[:, ROUTE_R1].astype(jnp.int32)
    d2 = offs[e2] + route[:, ROUTE_R2].astype(jnp.int32)
    p = (TOP_K * t // MOE_ROWS + n_experts) * MOE_ROWS
    ids = jnp.arange(t, dtype=jnp.int32)
    tok = jnp.zeros((p,), jnp.int32).at[d1].set(ids).at[d2].set(ids)
    starts = jnp.arange(p // MOE_ROWS, dtype=jnp.int32) * MOE_ROWS
    tile_expert = jnp.minimum(jnp.searchsorted(ends, starts, side="right"), n_experts - 1).astype(jnp.int32)
    n_used = (ends[-1:] // MOE_ROWS).astype(jnp.int32)
    return d1, d2, tok, tile_expert, n_used


def _ffn_kernel(h_ref, w1_ref, w3_ref, w2_ref, x_ref, mod_ref, o_ref, acc_ref):
    f = pl.program_id(1)

    @pl.when(f == 0)
    def _():
        acc_ref[...] = jnp.zeros(acc_ref.shape, F32)

    h = h_ref[...]
    g = _silu(_dot(h, w1_ref[...])) * _dot(h, w3_ref[...])
    acc_ref[...] += _bdot(g, w2_ref[...])

    @pl.when(f == pl.num_programs(1) - 1)
    def _():
        o_ref[...] = x_ref[...] + mod_ref[0, 5:6, :] * acc_ref[...]


def _ffn_tile(d_ff):
    for tf in (512, 256, 128):
        if d_ff % tf == 0:
            return tf
    raise ValueError(d_ff)


def _ffn_call(h, w1, w3, w2, x, mod, seq):
    t, d = x.shape
    d_ff = w1.shape[1]
    tm = _row_tile(seq, 1024)
    per_seq = seq // tm
    tf = _ffn_tile(d_ff)
    row = lambda i, f: (i, 0)
    return pl.pallas_call(
        _ffn_kernel,
        grid=(t // tm, d_ff // tf),
        in_specs=[
            pl.BlockSpec((tm, d), row),
            pl.BlockSpec((d, tf), lambda i, f: (0, f)),
            pl.BlockSpec((d, tf), lambda i, f: (0, f)),
            pl.BlockSpec((tf, d), lambda i, f: (f, 0)),
            pl.BlockSpec((tm, d), row),
            pl.BlockSpec((1, 6, d), lambda i, f: (i // per_seq, 0, 0)),
        ],
        out_specs=pl.BlockSpec((tm, d), row),
        out_shape=jax.ShapeDtypeStruct((t, d), F32),
        scratch_shapes=[pltpu.VMEM((tm, d), F32)],
        compiler_params=_params(2),
        name="ffn_dense",
    )(h, w1, w3, w2, x, mod)


def _final_kernel(x_ref, g_ref, o_ref):
    x = x_ref[...]
    ms = jnp.mean(x * x, axis=-1, keepdims=True)
    o_ref[...] = x * lax.rsqrt(ms + EPS) * g_ref[...]


def _final_call(x, g):
    t, d = x.shape
    tm = _row_tile(t, 512)
    return pl.pallas_call(
        _final_kernel,
        grid=(t // tm,),
        in_specs=[pl.BlockSpec((tm, d), lambda i: (i, 0)), pl.BlockSpec((1, d), lambda i: (0, 0))],
        out_specs=pl.BlockSpec((tm, d), lambda i: (i, 0)),
        out_shape=jax.ShapeDtypeStruct((t, d), F32),
        compiler_params=_params(1),
        name="final_norm",
    )(x, g)


def _lane_row(values, lane0, n_rows=1, row=0):
    out = jnp.zeros((n_rows, LANES), F32)
    return out.at[row, lane0:lane0 + values.shape[0]].set(values.astype(F32))


def _split_w_in(w_in):
    qk_a, v_a = N_HEADS * DK_A, N_HEADS * DV
    qk_b, qk_c, vw = N_HEADS * DK_B, N_HEADS * DK_C, N_HEADS * DV
    sizes = (qk_a, qk_a, v_a, v_a, N_HEADS, N_HEADS,
             qk_b, qk_b, vw, vw, N_HEADS, N_HEADS,
             qk_c, qk_c, vw, vw, GLA_RANK)
    names = ("q_a", "k_a", "v_a", "z_a", "a_a", "b_a", "q_b", "k_b", "v_b", "o_b", "i_b", "f_b",
             "q_c", "k_c", "v_c", "z_c", "g_c")
    assert sum(sizes) == w_in.shape[1]
    cols, off = {}, 0
    for name, size in zip(names, sizes):
        cols[name] = w_in[:, off:off + size]
        off += size
    big = jnp.concatenate([cols[n] for n in ("q_a", "k_a", "v_a", "z_a", "q_b", "k_b", "v_b", "o_b",
                                             "q_c", "k_c", "v_c", "z_c")], axis=1).astype(BF16)
    small = jnp.concatenate([cols[n] for n in ("a_a", "b_a", "i_b", "f_b", "g_c")], axis=1)
    small = jnp.pad(small, ((0, 0), (0, LANES - small.shape[1])))
    s_hi = small.astype(BF16)
    s_lo = (small - s_hi.astype(F32)).astype(BF16)
    return big, s_hi, s_lo


def kernel(x, c, w_ada, b_ada, g_mix, g_ffn, g_final, w_in, conv_a, a_log, dt_bias, norm_a, conv_b, b_i, b_f,
           norm_b, w_gla2, b_gla, norm_c, w_br, w_mg, b_mg, w_o, w1_d, w3_d, w2_d, w_router, b_router,
           w1_e, w3_e, w2_e):
    bsz, seq, d = x.shape
    depth = w_ada.shape[0]
    assert seq % MIX_ROWS == 0 and d % LANES == 0
    t = bsz * seq
    xf = x.reshape(t, d)
    mod_all = _ada_call(c, w_ada, b_ada).reshape(depth, bsz, 6, d)

    for layer in range(depth):
        mod = mod_all[layer]
        w_big, ws_hi, ws_lo = _split_w_in(w_in[layer])
        (h, qkv_a, z_a, qk_b, v_b, o_b, qk_c, v_c, z_c, small) = _inproj_call(
            xf, mod, g_mix[layer].reshape(1, d), w_big, ws_hi, ws_lo, seq)

        pa = (_lane_row(-jnp.exp(a_log[layer]), LANE_A_DECAY, SUBLANES, 0)
              + _lane_row(dt_bias[layer], LANE_A_DECAY, SUBLANES, 1))
        y_a = _mixer_a_call(qkv_a, z_a, small, conv_a[layer], pa, norm_a[layer].reshape(1, DV), bsz, seq)

        pb = (_lane_row(b_i[layer], LANE_B_IN, SUBLANES, 0) + _lane_row(b_f[layer], LANE_B_FORGET, SUBLANES, 0))
        y_b = _mixer_b_call(qk_b, v_b, o_b, small, conv_b[layer], pb, norm_b[layer].reshape(1, DV), bsz, seq)

        wg = jnp.zeros((LANES, N_HEADS * DK_C), F32).at[LANE_C_GATE:LANE_C_GATE + GLA_RANK].set(w_gla2[layer])
        y_c = _mixer_c_call(qk_c, v_c, z_c, small, wg, b_gla[layer].reshape(1, -1),
                            norm_c[layer].reshape(1, DV), bsz, seq)

        xf = _merge_call(xf, h, y_a, y_b, y_c, mod, w_mg[layer].astype(BF16), b_mg[layer][:, None, :],
                         w_br[layer].astype(BF16), w_o[layer].astype(BF16), seq)

        j = layer // 2
        last = layer == depth - 1
        if layer % 2 == 0:
            h2 = _prenorm_call(xf, mod, g_ffn[layer].reshape(1, d), seq)
            xf = _ffn_call(h2, w1_d[j].astype(BF16), w3_d[j].astype(BF16), w2_d[j].astype(BF16), xf, mod, seq)
            if last:
                xf = _final_call(xf, g_final.reshape(1, d))
        else:
            n_experts = w_router.shape[2]
            wr = jnp.pad(w_router[j], ((0, 0), (0, LANES - n_experts)))
            br = jnp.pad(b_router[j], (0, LANES - n_experts)).reshape(1, LANES)
            h2, route, counts = _prenorm_call(xf, mod, g_ffn[layer].reshape(1, d), seq, router=(wr, br, n_experts))
            d1, d2, tok, tile_expert, n_used = _moe_layout(route, counts, n_experts, t)
            xs = _moe_gather_call(tok, h2)
            ys = _moe_ffn_call(tile_expert, n_used, xs, w1_e[j].astype(BF16), w3_e[j].astype(BF16),
                               w2_e[j].astype(BF16))
            xf = _moe_combine_call(d1, d2, ys, route, xf, mod, g_final.reshape(1, d), seq, last)

    return xf.reshape(bsz, seq, d)
```

```python
import functools

import jax
import jax.numpy as jnp
from jax import lax
from jax.experimental import pallas as pl
from jax.experimental.pallas import tpu as pltpu

F32 = jnp.float32
BF16 = jnp.bfloat16

EPS = 1e-6
CHUNK = 64
CONV_K = 4
N_HEADS = 4
DV = 128
DK_A = 128
DK_B = 64
DK_C = 64
GLA_RANK = 16
GLA_TAU = 16.0
TOP_K = 2

V7X_MXU_DIM = 256
MIX_ROWS = V7X_MXU_DIM
LANES = 128
SUBLANES = 8
V7X_VMEM_LIMIT = 56 * 1024 * 1024

LANE_A_DECAY = 0
LANE_A_BETA = 4
LANE_B_IN = 8
LANE_B_FORGET = 12
LANE_C_GATE = 16

NN = (((1,), (0,)), ((), ()))
NT = (((1,), (1,)), ((), ()))
TN = (((0,), (0,)), ((), ()))


def _dot(a, b, dims=NN):
    return lax.dot_general(a, b, dims, preferred_element_type=F32)


def _bdot(a, b, dims=NN):
    return _dot(a.astype(BF16), b.astype(BF16), dims)


def _split3(x):
    hi = x.astype(BF16)
    r1 = x - hi.astype(F32)
    mid = r1.astype(BF16)
    lo = (r1 - mid.astype(F32)).astype(BF16)
    return hi, mid, lo


def _dot_exact_lhs(a, b, dims=NN):
    hi, mid, lo = _split3(b)
    return _dot(a, hi, dims) + _dot(a, mid, dims) + _dot(a, lo, dims)


def _dot_hi(a, b, dims=NN):
    a_hi = a.astype(BF16)
    a_lo = (a - a_hi.astype(F32)).astype(BF16)
    b_hi = b.astype(BF16)
    b_lo = (b - b_hi.astype(F32)).astype(BF16)
    return _dot(a_hi, b_hi, dims) + _dot(a_hi, b_lo, dims) + _dot(a_lo, b_hi, dims)


def _sigmoid(x):
    return 1.0 / (1.0 + jnp.exp(-x))


def _silu(x):
    return x * _sigmoid(x)


def _softplus(x):
    return jnp.maximum(x, 0.0) + jnp.log(1.0 + jnp.exp(-jnp.abs(x)))


def _log_sigmoid(x):
    return jnp.minimum(x, 0.0) - jnp.log(1.0 + jnp.exp(-jnp.abs(x)))


def _params(n_axes):
    return pltpu.CompilerParams(
        dimension_semantics=("arbitrary",) * n_axes,
        vmem_limit_bytes=V7X_VMEM_LIMIT,
    )


def _row_tile(t, want):
    tm = min(want, t)
    assert t % tm == 0, (t, tm)
    return tm


def _ada_kernel(c_ref, w_ref, b_ref, o_ref):
    c = c_ref[...]
    o_ref[0] = _bdot(_silu(c), w_ref[0]) + b_ref[0]


def _ada_call(c, w_ada, b_ada):
    depth, d, d6 = w_ada.shape
    bsz = c.shape[0]
    n_col = d6 // d
    return pl.pallas_call(
        _ada_kernel,
        grid=(depth, n_col),
        in_specs=[
            pl.BlockSpec((bsz, d), lambda l, n: (0, 0)),
            pl.BlockSpec((1, d, d), lambda l, n: (l, 0, n)),
            pl.BlockSpec((1, 1, d), lambda l, n: (l, 0, n)),
        ],
        out_specs=pl.BlockSpec((1, bsz, d), lambda l, n: (l, 0, n)),
        out_shape=jax.ShapeDtypeStruct((depth, bsz, d6), F32),
        compiler_params=_params(2),
        name="ada_mod",
    )(c, w_ada, b_ada.reshape(depth, 1, d6))


def _norm_mod(x, g, shift, scale):
    ms = jnp.mean(x * x, axis=-1, keepdims=True)
    return (x * lax.rsqrt(ms + EPS) * g) * (1.0 + scale) + shift


IN_WIDTHS = (1536, 512, 512, 512, 512, 512, 512, 512)


def _inproj_kernel(x_ref, mod_ref, g_ref, w_ref, wsh_ref, wsl_ref, h_ref, *out_refs):
    h = _norm_mod(x_ref[...], g_ref[...], mod_ref[0, 0:1, :], mod_ref[0, 1:2, :])
    hb = h.astype(BF16)
    h_ref[...] = hb.astype(h_ref.dtype)
    off = 0
    for o_ref, width in zip(out_refs[:-1], IN_WIDTHS):
        o_ref[...] = _dot(hb, w_ref[:, off:off + width]).astype(o_ref.dtype)
        off += width
    hl = (h - hb.astype(F32)).astype(BF16)
    wsh = wsh_ref[...]
    out_refs[-1][...] = _dot(hb, wsh) + _dot(hl, wsh) + _dot(hb, wsl_ref[...])


def _inproj_call(x, mod, g, w_big, ws_hi, ws_lo, seq):
    t, d = x.shape
    tm = _row_tile(seq, 512)
    per_seq = seq // tm
    n_big = w_big.shape[1]
    row = lambda i: (i, 0)
    const = lambda i: (0, 0)
    out_shapes = [jax.ShapeDtypeStruct((t, d), BF16)]
    out_specs = [pl.BlockSpec((tm, d), row)]
    for width in IN_WIDTHS:
        out_shapes.append(jax.ShapeDtypeStruct((t, width), BF16))
        out_specs.append(pl.BlockSpec((tm, width), row))
    out_shapes.append(jax.ShapeDtypeStruct((t, LANES), F32))
    out_specs.append(pl.BlockSpec((tm, LANES), row))
    return pl.pallas_call(
        _inproj_kernel,
        grid=(t // tm,),
        in_specs=[
            pl.BlockSpec((tm, d), row),
            pl.BlockSpec((1, 6, d), lambda i: (i // per_seq, 0, 0)),
            pl.BlockSpec((1, d), const),
            pl.BlockSpec((d, n_big), const),
            pl.BlockSpec((d, LANES), const),
            pl.BlockSpec((d, LANES), const),
        ],
        out_specs=out_specs,
        out_shape=out_shapes,
        compiler_params=_params(1),
        name="in_proj",
    )(x, mod, g, w_big, ws_hi, ws_lo)


def _conv_silu(x_ref, cw_ref, cbuf):
    rows = x_ref.shape[0]
    x = x_ref[...].astype(F32)
    cbuf[SUBLANES:SUBLANES + rows, :] = x
    acc = x * cw_ref[CONV_K - 1:CONV_K, :]
    for s in range(1, CONV_K):
        acc = acc + cbuf[SUBLANES - s:SUBLANES - s + rows, :] * cw_ref[CONV_K - 1 - s:CONV_K - s, :]
    cbuf[0:SUBLANES, :] = cbuf[rows:rows + SUBLANES, :]
    return _silu(acc)


def _chunk_masks(rows):
    r = lax.broadcasted_iota(jnp.int32, (rows, rows), 0)
    c = lax.broadcasted_iota(jnp.int32, (rows, rows), 1)
    same = jnp.right_shift(r, 6) == jnp.right_shift(c, 6)
    causal = jnp.logical_and(same, r >= c)
    strict = jnp.logical_and(same, r > c)
    return causal, strict


def _chunk_end_rows(x, rows):
    parts = []
    for c in range(rows // CHUNK):
        last = x[c * CHUNK + CHUNK - 1:c * CHUNK + CHUNK, :]
        parts.append(jnp.broadcast_to(last, (CHUNK, x.shape[1])))
    return jnp.concatenate(parts, axis=0)


def _head_rms(o, w_row):
    ms = jnp.mean(o * o, axis=-1, keepdims=True)
    return o * lax.rsqrt(ms + EPS) * w_row


def _solve_unit_lower(a, rhs):
    x = rhs - _bdot(a, rhs)
    p = _bdot(a, a)
    for it in range(5):
        x = x + _bdot(p, x)
        if it < 4:
            p = _bdot(p, p)
    return x


def _mixer_a_kernel(qkv_ref, z_ref, small_ref, cw_ref, pa_ref, nw_ref, y_ref, cbuf, s_scr):
    rows = qkv_ref.shape[0]
    n_chunks = rows // CHUNK
    hw = N_HEADS * DK_A

    @pl.when(pl.program_id(1) == 0)
    def _():
        cbuf[0:SUBLANES, :] = jnp.zeros((SUBLANES, cbuf.shape[1]), F32)
        s_scr[...] = jnp.zeros(s_scr.shape, F32)

    qkv = _conv_silu(qkv_ref, cw_ref, cbuf)
    causal, strict = _chunk_masks(rows)

    gates = small_ref[...]
    log_decay = pa_ref[0:1, :] * _softplus(gates + pa_ref[1:2, :])
    beta_all = _sigmoid(gates)
    gc = _dot_exact_lhs(causal.astype(BF16), log_decay)
    gc_t = gc.T
    gc_end = _chunk_end_rows(gc, rows)

    heads = range(N_HEADS)
    a_mats, rhs, attns, qgs, k_ends, s_decays = [], [], [], [], [], []
    for h in heads:
        q = qkv[:, h * DK_A:(h + 1) * DK_A]
        k = qkv[:, hw + h * DK_A:hw + (h + 1) * DK_A]
        v = qkv[:, 2 * hw + h * DV:2 * hw + (h + 1) * DV]
        q = q * (lax.rsqrt(jnp.sum(q * q, axis=-1, keepdims=True) + 1e-6) * (DK_A ** -0.5))
        k = k * lax.rsqrt(jnp.sum(k * k, axis=-1, keepdims=True) + 1e-6)
        beta = beta_all[:, LANE_A_BETA + h:LANE_A_BETA + h + 1]
        g_col = gc[:, LANE_A_DECAY + h:LANE_A_DECAY + h + 1]
        g_row = gc_t[LANE_A_DECAY + h:LANE_A_DECAY + h + 1, :]
        ge_col = gc_end[:, LANE_A_DECAY + h:LANE_A_DECAY + h + 1]
        decay = jnp.exp(jnp.where(causal, g_col - g_row, 0.0))
        kb = k * beta
        k16 = k.astype(BF16)
        a_mats.append(jnp.where(strict, _dot(kb.astype(BF16), k16, NT) * decay, 0.0).astype(BF16))
        eg = jnp.exp(g_col)
        rhs.append(jnp.concatenate([v * beta, kb * eg], axis=1))
        attns.append(jnp.where(causal, _dot(q.astype(BF16), k16, NT) * decay, 0.0).astype(BF16))
        qgs.append((q * eg).astype(BF16))
        k_ends.append((k * jnp.exp(ge_col - g_col)).astype(BF16))
        s_decays.append(jnp.exp(ge_col))

    xs = [rhs[h] - _dot(a_mats[h], rhs[h].astype(BF16)) for h in heads]
    ps = [_dot(a_mats[h], a_mats[h]).astype(BF16) for h in heads]
    for it in range(5):
        xs = [xs[h] + _dot(ps[h], xs[h].astype(BF16)) for h in heads]
        if it < 4:
            ps = [_dot(ps[h], ps[h]).astype(BF16) for h in heads]
    us = [xs[h][:, :DV] for h in heads]
    ws = [xs[h][:, DV:].astype(BF16) for h in heads]

    states = [s_scr[h] for h in heads]
    deltas = [[] for _ in heads]
    inters = [[] for _ in heads]
    for c in range(n_chunks):
        lo, hi = c * CHUNK, (c + 1) * CHUNK
        for h in heads:
            s16 = states[h].astype(BF16)
            delta = us[h][lo:hi] - _dot(ws[h][lo:hi], s16)
            inters[h].append(_dot(qgs[h][lo:hi], s16))
            sd = jnp.concatenate([s_decays[h][lo:hi]] * (DK_A // CHUNK), axis=0)
            states[h] = sd * states[h] + _dot(k_ends[h][lo:hi], delta.astype(BF16), TN)
            deltas[h].append(delta.astype(BF16))
    for h in heads:
        s_scr[h] = states[h]
        o = jnp.concatenate(inters[h], axis=0) + _dot(attns[h], jnp.concatenate(deltas[h], axis=0))
        z = z_ref[:, h * DV:(h + 1) * DV].astype(F32)
        y_ref[:, h * DV:(h + 1) * DV] = (_head_rms(o, nw_ref[...]) * _silu(z)).astype(y_ref.dtype)


def _mixer_a_call(qkv, z, small, conv_w, pa, norm_w, bsz, seq):
    per_seq = seq // MIX_ROWS
    cw = qkv.shape[1]
    row = lambda b, j: (b * per_seq + j, 0)
    const = lambda b, j: (0, 0)
    return pl.pallas_call(
        _mixer_a_kernel,
        grid=(bsz, per_seq),
        in_specs=[
            pl.BlockSpec((MIX_ROWS, cw), row),
            pl.BlockSpec((MIX_ROWS, N_HEADS * DV), row),
            pl.BlockSpec((MIX_ROWS, LANES), row),
            pl.BlockSpec((CONV_K, cw), const),
            pl.BlockSpec((SUBLANES, LANES), const),
            pl.BlockSpec((1, DV), const),
        ],
        out_specs=pl.BlockSpec((MIX_ROWS, N_HEADS * DV), row),
        out_shape=jax.ShapeDtypeStruct((bsz * seq, N_HEADS * DV), BF16),
        scratch_shapes=[
            pltpu.VMEM((MIX_ROWS + 2 * SUBLANES, cw), F32),
            pltpu.VMEM((N_HEADS, DK_A, DV), F32),
        ],
        compiler_params=_params(2),
        name="mixer_deltanet",
    )(qkv, z, small, conv_w, pa, norm_w)


def _mixer_b_kernel(qk_ref, v_ref, og_ref, small_ref, cw_ref, pb_ref, nw_ref, y_ref, cbuf, e_scr, m_scr):
    rows = qk_ref.shape[0]
    n_chunks = rows // CHUNK
    hw = N_HEADS * DK_B

    @pl.when(pl.program_id(1) == 0)
    def _():
        cbuf[0:SUBLANES, :] = jnp.zeros((SUBLANES, cbuf.shape[1]), F32)
        e_scr[...] = jnp.zeros(e_scr.shape, F32)
        m_scr[...] = jnp.zeros(m_scr.shape, F32)

    qk = _conv_silu(qk_ref, cw_ref, cbuf)
    causal, _ = _chunk_masks(rows)

    pre = small_ref[...] + pb_ref[0:1, :]
    log_f = _log_sigmoid(pre)
    bcum = _dot_exact_lhs(causal.astype(BF16), log_f)
    bcum_t = bcum.T
    pre_t = pre.T
    bend = _chunk_end_rows(bcum, rows)
    ones = jnp.ones((rows, DV), BF16)

    heads = range(N_HEADS)
    qs, v_exts, b_cols, be_cols, m_intras, m_chunks, wts, k_ws = [], [], [], [], [], [], [], []
    for h in heads:
        q = qk[:, h * DK_B:(h + 1) * DK_B].astype(BF16)
        k = qk[:, hw + h * DK_B:hw + (h + 1) * DK_B] * (DK_B ** -0.5)
        lf = LANE_B_FORGET + h
        li = LANE_B_IN + h
        b_col = bcum[:, lf:lf + 1]
        b_row = bcum_t[lf:lf + 1, :]
        i_row = pre_t[li:li + 1, :]
        i_col = pre[:, li:li + 1]
        be_col = bend[:, lf:lf + 1]
        dlog = jnp.where(causal, b_col - b_row + i_row, -jnp.inf)
        m_intra = jnp.max(dlog, axis=-1, keepdims=True)
        m_chunk = _chunk_end_rows(m_intra, rows)
        wts.append((jnp.exp(dlog - m_intra) * _dot(q, k.astype(BF16), NT)).astype(BF16))
        k_ws.append((k * jnp.exp(be_col - b_col + i_col - m_chunk)).astype(BF16))
        qs.append(q)
        v_exts.append(jnp.concatenate([v_ref[:, h * DV:(h + 1) * DV], ones], axis=1))
        b_cols.append(b_col)
        be_cols.append(be_col)
        m_intras.append(m_intra)
        m_chunks.append(m_chunk)

    intras = [_dot(wts[h], v_exts[h]) for h in heads]
    kvs = [[_dot(k_ws[h][c * CHUNK:(c + 1) * CHUNK], v_exts[h][c * CHUNK:(c + 1) * CHUNK], TN)
            for c in range(n_chunks)] for h in heads]

    es = [e_scr[h] for h in heads]
    m_ss = [m_scr[:, h:h + 1] for h in heads]
    outs = [[] for _ in heads]
    for c in range(n_chunks):
        lo, hi = c * CHUNK, (c + 1) * CHUNK
        for h in heads:
            m_s = m_ss[h]
            m_tot = jnp.maximum(b_cols[h][lo:hi] + m_s, m_intras[h][lo:hi])
            s_inter = jnp.exp(b_cols[h][lo:hi] + m_s - m_tot)
            s_intra = jnp.exp(m_intras[h][lo:hi] - m_tot)
            both = s_inter * _dot(qs[h][lo:hi], es[h].astype(BF16)) + s_intra * intras[h][lo:hi]
            den = jnp.maximum(jnp.abs(both[:, DV:]), jnp.exp(-m_tot))
            outs[h].append(both[:, :DV] / den)
            m_new = jnp.maximum(be_cols[h][lo:hi] + m_s, m_chunks[h][lo:hi])
            s_prev = jnp.exp(be_cols[h][lo:hi] + m_s - m_new)
            s_cur = jnp.exp(m_chunks[h][lo:hi] - m_new)
            es[h] = s_prev * es[h] + s_cur * kvs[h][c]
            m_ss[h] = m_new
    for h in heads:
        e_scr[h] = es[h]
        m_scr[:, h:h + 1] = m_ss[h]
        o = jnp.concatenate(outs[h], axis=0)
        og = og_ref[:, h * DV:(h + 1) * DV].astype(F32)
        y_ref[:, h * DV:(h + 1) * DV] = (_sigmoid(og) * _head_rms(o, nw_ref[...])).astype(y_ref.dtype)


def _mixer_b_call(qk, v, og, small, conv_w, pb, norm_w, bsz, seq):
    per_seq = seq // MIX_ROWS
    cw = qk.shape[1]
    row = lambda b, j: (b * per_seq + j, 0)
    const = lambda b, j: (0, 0)
    return pl.pallas_call(
        _mixer_b_kernel,
        grid=(bsz, per_seq),
        in_specs=[
            pl.BlockSpec((MIX_ROWS, cw), row),
            pl.BlockSpec((MIX_ROWS, N_HEADS * DV), row),
            pl.BlockSpec((MIX_ROWS, N_HEADS * DV), row),
            pl.BlockSpec((MIX_ROWS, LANES), row),
            pl.BlockSpec((CONV_K, cw), const),
            pl.BlockSpec((SUBLANES, LANES), const),
            pl.BlockSpec((1, DV), const),
        ],
        out_specs=pl.BlockSpec((MIX_ROWS, N_HEADS * DV), row),
        out_shape=jax.ShapeDtypeStruct((bsz * seq, N_HEADS * DV), BF16),
        scratch_shapes=[
            pltpu.VMEM((MIX_ROWS + 2 * SUBLANES, cw), F32),
            pltpu.VMEM((N_HEADS, DK_B, 2 * DV), F32),
            pltpu.VMEM((CHUNK, LANES), F32),
        ],
        compiler_params=_params(2),
        name="mixer_mlstm",
    )(qk, v, og, small, conv_w, pb, norm_w)


def _mixer_c_kernel(qk_ref, v_ref, z_ref, small_ref, wg_ref, bg_ref, nw_ref, y_ref, st_scr):
    rows = qk_ref.shape[0]
    n_chunks = rows // CHUNK
    hw = N_HEADS * DK_C

    @pl.when(pl.program_id(1) == 0)
    def _():
        st_scr[...] = jnp.zeros(st_scr.shape, F32)

    causal, _ = _chunk_masks(rows)
    log_a = _log_sigmoid(_dot_hi(small_ref[...], wg_ref[...]) + bg_ref[...]) * (1.0 / GLA_TAU)
    bc = _dot_exact_lhs(causal.astype(BF16), log_a)
    parts = []
    for c in range(n_chunks):
        mid = bc[c * CHUNK + CHUNK // 2:c * CHUNK + CHUNK // 2 + 1, :]
        parts.append(jnp.broadcast_to(mid, (CHUNK, hw)))
    rel = bc - jnp.concatenate(parts, axis=0)
    bc_end = _chunk_end_rows(bc, rows)
    qk = qk_ref[...].astype(F32)
    q_all = qk[:, :hw] * (DK_C ** -0.5)
    k_all = qk[:, hw:]
    q_rel = (q_all * jnp.exp(rel)).astype(BF16)
    k_rel = (k_all * jnp.exp(-rel)).astype(BF16)
    k_end = (k_all * jnp.exp(bc_end - bc)).astype(BF16)
    q_abs = (q_all * jnp.exp(bc)).astype(BF16)
    a_end = jnp.exp(bc_end)

    heads = range(N_HEADS)
    sls = [slice(h * DK_C, (h + 1) * DK_C) for h in heads]
    vs = [v_ref[:, h * DV:(h + 1) * DV] for h in heads]
    attns = [jnp.where(causal, _dot(q_rel[:, sls[h]], k_rel[:, sls[h]], NT), 0.0).astype(BF16) for h in heads]
    o_intras = [_dot(attns[h], vs[h]) for h in heads]
    kvs = [[_dot(vs[h][c * CHUNK:(c + 1) * CHUNK], k_end[c * CHUNK:(c + 1) * CHUNK, sls[h]], TN)
            for c in range(n_chunks)] for h in heads]
    sts = [st_scr[h] for h in heads]
    inters = [[] for _ in heads]
    for c in range(n_chunks):
        lo, hi = c * CHUNK, (c + 1) * CHUNK
        for h in heads:
            inters[h].append(_dot(q_abs[lo:hi, sls[h]], sts[h].astype(BF16), NT))
            sts[h] = sts[h] * a_end[lo:lo + 1, sls[h]] + kvs[h][c]
    for h in heads:
        st_scr[h] = sts[h]
        o = jnp.concatenate(inters[h], axis=0) + o_intras[h]
        z = z_ref[:, h * DV:(h + 1) * DV].astype(F32)
        y_ref[:, h * DV:(h + 1) * DV] = (_head_rms(o, nw_ref[...]) * _silu(z)).astype(y_ref.dtype)


def _mixer_c_call(qk, v, z, small, wg, bg, norm_w, bsz, seq):
    per_seq = seq // MIX_ROWS
    cw = qk.shape[1]
    row = lambda b, j: (b * per_seq + j, 0)
    const = lambda b, j: (0, 0)
    return pl.pallas_call(
        _mixer_c_kernel,
        grid=(bsz, per_seq),
        in_specs=[
            pl.BlockSpec((MIX_ROWS, cw), row),
            pl.BlockSpec((MIX_ROWS, N_HEADS * DV), row),
            pl.BlockSpec((MIX_ROWS, N_HEADS * DV), row),
            pl.BlockSpec((MIX_ROWS, LANES), row),
            pl.BlockSpec((LANES, N_HEADS * DK_C), const),
            pl.BlockSpec((1, N_HEADS * DK_C), const),
            pl.BlockSpec((1, DV), const),
        ],
        out_specs=pl.BlockSpec((MIX_ROWS, N_HEADS * DV), row),
        out_shape=jax.ShapeDtypeStruct((bsz * seq, N_HEADS * DV), BF16),
        scratch_shapes=[pltpu.VMEM((N_HEADS, DV, DK_C), F32)],
        compiler_params=_params(2),
        name="mixer_gla",
    )(qk, v, z, small, wg, bg, norm_w)


def _merge_kernel(x_ref, h_ref, ya_ref, yb_ref, yc_ref, mod_ref, wmg_ref, bmg_ref, wbr_ref, wo_ref, o_ref):
    h = h_ref[...]
    merged = None
    for g, y_ref in enumerate((ya_ref, yb_ref, yc_ref)):
        gate = _sigmoid(_dot(h, wmg_ref[g]) + bmg_ref[g])
        term = gate * _dot(y_ref[...], wbr_ref[g])
        merged = term if merged is None else merged + term
    o_ref[...] = x_ref[...] + mod_ref[0, 2:3, :] * _bdot(merged, wo_ref[...])


def _merge_call(x, h, ya, yb, yc, mod, w_mg, b_mg, w_br, w_o, seq):
    t, d = x.shape
    tm = _row_tile(seq, 512)
    per_seq = seq // tm
    bw = ya.shape[1]
    row = lambda i: (i, 0)
    return pl.pallas_call(
        _merge_kernel,
        grid=(t // tm,),
        in_specs=[
            pl.BlockSpec((tm, d), row),
            pl.BlockSpec((tm, d), row),
            pl.BlockSpec((tm, bw), row),
            pl.BlockSpec((tm, bw), row),
            pl.BlockSpec((tm, bw), row),
            pl.BlockSpec((1, 6, d), lambda i: (i // per_seq, 0, 0)),
            pl.BlockSpec((3, d, d), lambda i: (0, 0, 0)),
            pl.BlockSpec((3, 1, d), lambda i: (0, 0, 0)),
            pl.BlockSpec((3, bw, d), lambda i: (0, 0, 0)),
            pl.BlockSpec((d, d), lambda i: (0, 0)),
        ],
        out_specs=pl.BlockSpec((tm, d), row),
        out_shape=jax.ShapeDtypeStruct((t, d), F32),
        compiler_params=_params(1),
        name="merge",
    )(x, h, ya, yb, yc, mod, w_mg, b_mg, w_br, w_o)


def _prenorm_kernel(x_ref, mod_ref, g_ref, h_ref):
    h_ref[...] = _norm_mod(x_ref[...], g_ref[...], mod_ref[0, 3:4, :], mod_ref[0, 4:5, :]).astype(h_ref.dtype)


ROUTE_E1, ROUTE_E2, ROUTE_W1, ROUTE_W2, ROUTE_R1, ROUTE_R2 = range(6)

def _router_kernel(n_experts, x_ref, mod_ref, g_ref, wr_ref, br_ref, h_ref, route_ref, count_ref, carry):
    @pl.when(pl.program_id(0) == 0)
    def _():
        carry[...] = jnp.zeros(carry.shape, F32)

    h = _norm_mod(x_ref[...], g_ref[...], mod_ref[0, 3:4, :], mod_ref[0, 4:5, :])
    h_ref[...] = h
    logits = _dot_hi(h, wr_ref[...]) + br_ref[...]
    tm = logits.shape[0]
    lane = lax.broadcasted_iota(jnp.int32, logits.shape, 1)
    logits = jnp.where(lane < n_experts, logits, -jnp.inf)
    m1 = jnp.max(logits, axis=-1, keepdims=True)
    i1 = jnp.min(jnp.where(logits == m1, lane, LANES), axis=-1, keepdims=True)
    rest = jnp.where(lane == i1, -jnp.inf, logits)
    m2 = jnp.max(rest, axis=-1, keepdims=True)
    i2 = jnp.min(jnp.where(rest == m2, lane, LANES), axis=-1, keepdims=True)
    e2 = jnp.exp(m2 - m1)
    w1 = 1.0 / (1.0 + e2)
    w2 = e2 * w1

    sel1 = lane == i1
    sel2 = lane == i2
    onehot = jnp.where(jnp.logical_or(sel1, sel2), 1.0, 0.0)
    r = lax.broadcasted_iota(jnp.int32, (tm, tm), 0)
    c = lax.broadcasted_iota(jnp.int32, (tm, tm), 1)
    before = jnp.where(r > c, 1.0, 0.0).astype(BF16)
    rank = _dot(before, onehot.astype(BF16)) + carry[0:1, :]
    carry[...] = carry[...] + jnp.sum(onehot, axis=0, keepdims=True)
    count_ref[...] = carry[...]
    r1 = jnp.sum(jnp.where(sel1, rank, 0.0), axis=-1, keepdims=True)
    r2 = jnp.sum(jnp.where(sel2, rank, 0.0), axis=-1, keepdims=True)
    rec = jnp.zeros(logits.shape, F32)
    for pos, val in ((ROUTE_E1, i1.astype(F32)), (ROUTE_E2, i2.astype(F32)), (ROUTE_W1, w1), (ROUTE_W2, w2),
                     (ROUTE_R1, r1), (ROUTE_R2, r2)):
        rec = jnp.where(lane == pos, val, rec)
    route_ref[...] = rec


def _prenorm_call(x, mod, g, seq, router=None):
    t, d = x.shape
    tm = _row_tile(seq, 512)
    per_seq = seq // tm
    row = lambda i: (i, 0)
    const = lambda i: (0, 0)
    in_specs = [
        pl.BlockSpec((tm, d), row),
        pl.BlockSpec((1, 6, d), lambda i: (i // per_seq, 0, 0)),
        pl.BlockSpec((1, d), const),
    ]
    if router is None:
        return pl.pallas_call(
            _prenorm_kernel,
            grid=(t // tm,),
            in_specs=in_specs,
            out_specs=pl.BlockSpec((tm, d), row),
            out_shape=jax.ShapeDtypeStruct((t, d), BF16),
            compiler_params=_params(1),
            name="prenorm",
        )(x, mod, g)
    wr, br, n_experts = router
    return pl.pallas_call(
        functools.partial(_router_kernel, n_experts),
        grid=(t // tm,),
        in_specs=in_specs + [pl.BlockSpec((d, LANES), const), pl.BlockSpec((1, LANES), const)],
        out_specs=[pl.BlockSpec((tm, d), row), pl.BlockSpec((tm, LANES), row),
                   pl.BlockSpec((SUBLANES, LANES), const)],
        out_shape=[jax.ShapeDtypeStruct((t, d), F32), jax.ShapeDtypeStruct((t, LANES), F32),
                   jax.ShapeDtypeStruct((SUBLANES, LANES), F32)],
        scratch_shapes=[pltpu.VMEM((SUBLANES, LANES), F32)],
        compiler_params=_params(1),
        name="prenorm_router",
    )(x, mod, g, wr, br)


MOE_ROWS = 1024
COMBINE_ROWS = 256


def _grouped(x):
    rows, d = x.shape
    return x.reshape(rows // SUBLANES, SUBLANES, d)


def _row_gather_start(grp_ref, sub_ref, src_hbm, dst_ref, sem):
    def issue(i, carry):
        for k in range(SUBLANES):
            r = i * SUBLANES + k
            pltpu.make_async_copy(src_hbm.at[grp_ref[r], pl.ds(sub_ref[r], 1), :],
                                  dst_ref.at[i, pl.ds(k, 1), :], sem).start(priority=k % 2)
        return carry

    lax.fori_loop(0, dst_ref.shape[0], issue, 0)


def _row_gather_wait(src_hbm, dst_ref, sem):
    pltpu.make_async_copy(src_hbm.at[pl.ds(0, dst_ref.shape[0])], dst_ref, sem).wait()


def _moe_gather_kernel(grp_ref, sub_ref, src_hbm, o_ref, sem):
    _row_gather_start(grp_ref, sub_ref, src_hbm, o_ref, sem)
    _row_gather_wait(src_hbm, o_ref, sem)


def _moe_gather_call(tok, src):
    p = tok.shape[0]
    d = src.shape[1]
    idx = lambda i: (i,)
    out = pl.pallas_call(
        _moe_gather_kernel,
        grid=(p // MOE_ROWS,),
        in_specs=[
            pl.BlockSpec((MOE_ROWS,), idx, memory_space=pltpu.SMEM),
            pl.BlockSpec((MOE_ROWS,), idx, memory_space=pltpu.SMEM),
            pl.BlockSpec(memory_space=pl.ANY),
        ],
        out_specs=pl.BlockSpec((MOE_ROWS // SUBLANES, SUBLANES, d), lambda i: (i, 0, 0)),
        out_shape=jax.ShapeDtypeStruct((p // SUBLANES, SUBLANES, d), src.dtype),
        scratch_shapes=[pltpu.SemaphoreType.DMA(())],
        compiler_params=_params(1),
        name="moe_gather",
    )(tok // SUBLANES, tok % SUBLANES, _grouped(src))
    return out.reshape(p, d)


def _moe_ffn_kernel(te_ref, nu_ref, xs_ref, w1_ref, w3_ref, w2_ref, ys_ref, hb_scr, acc_ref):
    i = pl.program_id(0)
    f = pl.program_id(1)
    last = pl.num_programs(1) - 1
    used = i < nu_ref[0]

    @pl.when(jnp.logical_and(used, f == 0))
    def _():
        hb_scr[...] = xs_ref[...].astype(BF16)
        acc_ref[...] = jnp.zeros(acc_ref.shape, F32)

    @pl.when(used)
    def _():
        h = hb_scr[...]
        g = _silu(_dot(h, w1_ref[0])) * _dot(h, w3_ref[0])
        acc_ref[...] += _bdot(g, w2_ref[0])

    @pl.when(jnp.logical_and(used, f == last))
    def _():
        ys_ref[...] = acc_ref[...]

    @pl.when(jnp.logical_and(jnp.logical_not(used), f == last))
    def _():
        ys_ref[...] = jnp.zeros(ys_ref.shape, F32)


def _moe_ffn_call(tile_expert, n_used, xs, w1, w3, w2):
    p, d = xs.shape
    d_ff = w1.shape[2]
    tf = _ffn_tile(d_ff)
    grid_spec = pltpu.PrefetchScalarGridSpec(
        num_scalar_prefetch=2,
        grid=(p // MOE_ROWS, d_ff // tf),
        in_specs=[
            pl.BlockSpec((MOE_ROWS, d), lambda i, f, te, nu: (i, 0)),
            pl.BlockSpec((1, d, tf), lambda i, f, te, nu: (te[i], 0, f)),
            pl.BlockSpec((1, d, tf), lambda i, f, te, nu: (te[i], 0, f)),
            pl.BlockSpec((1, tf, d), lambda i, f, te, nu: (te[i], f, 0)),
        ],
        out_specs=pl.BlockSpec((MOE_ROWS, d), lambda i, f, te, nu: (i, 0)),
        scratch_shapes=[pltpu.VMEM((MOE_ROWS, d), BF16), pltpu.VMEM((MOE_ROWS, d), F32)],
    )
    return pl.pallas_call(
        _moe_ffn_kernel,
        grid_spec=grid_spec,
        out_shape=jax.ShapeDtypeStruct((p, d), F32),
        compiler_params=_params(2),
        name="moe_ffn",
    )(tile_expert, n_used, xs, w1, w3, w2)


def _moe_combine_kernel(final, g1_ref, s1_ref, g2_ref, s2_ref, ys_hbm, route_ref, x_ref, mod_ref, gf_ref,
                        o_ref, buf, sem):
    rows, d = x_ref.shape
    _row_gather_start(g1_ref, s1_ref, ys_hbm, buf.at[0], sem.at[0])
    _row_gather_start(g2_ref, s2_ref, ys_hbm, buf.at[1], sem.at[1])
    _row_gather_wait(ys_hbm, buf.at[0], sem.at[0])
    _row_gather_wait(ys_hbm, buf.at[1], sem.at[1])
    route = route_ref[...]
    f = (route[:, ROUTE_W1:ROUTE_W1 + 1] * buf[0].reshape(rows, d)
         + route[:, ROUTE_W2:ROUTE_W2 + 1] * buf[1].reshape(rows, d))
    y = x_ref[...] + mod_ref[0, 5:6, :] * f
    if final:
        ms = jnp.mean(y * y, axis=-1, keepdims=True)
        y = y * lax.rsqrt(ms + EPS) * gf_ref[...]
    o_ref[...] = y


def _moe_combine_call(d1, d2, ys, route, x, mod, g_final, seq, final):
    t, d = x.shape
    tm = _row_tile(seq, COMBINE_ROWS)
    per_seq = seq // tm
    row = lambda i: (i, 0)
    idx_spec = pl.BlockSpec((tm,), lambda i: (i,), memory_space=pltpu.SMEM)
    return pl.pallas_call(
        functools.partial(_moe_combine_kernel, final),
        grid=(t // tm,),
        in_specs=[
            idx_spec, idx_spec, idx_spec, idx_spec,
            pl.BlockSpec(memory_space=pl.ANY),
            pl.BlockSpec((tm, LANES), row),
            pl.BlockSpec((tm, d), row),
            pl.BlockSpec((1, 6, d), lambda i: (i // per_seq, 0, 0)),
            pl.BlockSpec((1, d), lambda i: (0, 0)),
        ],
        out_specs=pl.BlockSpec((tm, d), row),
        out_shape=jax.ShapeDtypeStruct((t, d), F32),
        scratch_shapes=[pltpu.VMEM((2, tm // SUBLANES, SUBLANES, d), F32), pltpu.SemaphoreType.DMA((2,))],
        compiler_params=_params(1),
        name="moe_combine",
    )(d1 // SUBLANES, d1 % SUBLANES, d2 // SUBLANES, d2 % SUBLANES, _grouped(ys), route, x, mod, g_final)


def _moe_layout(route, counts, n_experts, t):
    cnt = counts[0, :n_experts].astype(jnp.int32)
    padded = ((cnt + MOE_ROWS - 1) // MOE_ROWS) * MOE_ROWS
    ends = jnp.cumsum(padded)
    offs = ends - padded
    e1 = route[:, ROUTE_E1].astype(jnp.int32)
    e2 = route[:, ROUTE_E2].astype(jnp.int32)
    d1 = offs[e1] + route[:, ROUTE_R1].astype(jnp.int32)
    d2 = offs[e2] + route[:, ROUTE_R2].astype(jnp.int32)
    p = (TOP_K * t // MOE_ROWS + n_experts) * MOE_ROWS
    ids = jnp.arange(t, dtype=jnp.int32)
    tok = jnp.zeros((p,), jnp.int32).at[jnp.concatenate([d1, d2])].set(jnp.concatenate([ids, ids]))
    starts = jnp.arange(p // MOE_ROWS, dtype=jnp.int32) * MOE_ROWS
    tile_expert = jnp.minimum(jnp.searchsorted(ends, starts, side="right"), n_experts - 1).astype(jnp.int32)
    n_used = (ends[-1:] // MOE_ROWS).astype(jnp.int32)
    return d1, d2, tok, tile_expert, n_used


def _ffn_kernel(h_ref, w1_ref, w3_ref, w2_ref, x_ref, mod_ref, o_ref, acc_ref):
    f = pl.program_id(1)

    @pl.when(f == 0)
    def _():
        acc_ref[...] = jnp.zeros(acc_ref.shape, F32)

    h = h_ref[...]
    g = _silu(_dot(h, w1_ref[...])) * _dot(h, w3_ref[...])
    acc_ref[...] += _bdot(g, w2_ref[...])

    @pl.when(f == pl.num_programs(1) - 1)
    def _():
        o_ref[...] = x_ref[...] + mod_ref[0, 5:6, :] * acc_ref[...]


def _ffn_tile(d_ff):
    for tf in (512, 256, 128):
        if d_ff % tf == 0:
            return tf
    raise ValueError(d_ff)


def _ffn_call(h, w1, w3, w2, x, mod, seq):
    t, d = x.shape
    d_ff = w1.shape[1]
    tm = _row_tile(seq, 1024)
    per_seq = seq // tm
    tf = _ffn_tile(d_ff)
    row = lambda i, f: (i, 0)
    return pl.pallas_call(
        _ffn_kernel,
        grid=(t // tm, d_ff // tf),
        in_specs=[
            pl.BlockSpec((tm, d), row),
            pl.BlockSpec((d, tf), lambda i, f: (0, f)),
            pl.BlockSpec((d, tf), lambda i, f: (0, f)),
            pl.BlockSpec((tf, d), lambda i, f: (f, 0)),
            pl.BlockSpec((tm, d), row),
            pl.BlockSpec((1, 6, d), lambda i, f: (i // per_seq, 0, 0)),
        ],
        out_specs=pl.BlockSpec((tm, d), row),
        out_shape=jax.ShapeDtypeStruct((t, d), F32),
        scratch_shapes=[pltpu.VMEM((tm, d), F32)],
        compiler_params=_params(2),
        name="ffn_dense",
    )(h, w1, w3, w2, x, mod)


def _final_kernel(x_ref, g_ref, o_ref):
    x = x_ref[...]
    ms = jnp.mean(x * x, axis=-1, keepdims=True)
    o_ref[...] = x * lax.rsqrt(ms + EPS) * g_ref[...]


def _final_call(x, g):
    t, d = x.shape
    tm = _row_tile(t, 512)
    return pl.pallas_call(
        _final_kernel,
        grid=(t // tm,),
        in_specs=[pl.BlockSpec((tm, d), lambda i: (i, 0)), pl.BlockSpec((1, d), lambda i: (0, 0))],
        out_specs=pl.BlockSpec((tm, d), lambda i: (i, 0)),
        out_shape=jax.ShapeDtypeStruct((t, d), F32),
        compiler_params=_params(1),
        name="final_norm",
    )(x, g)


def _lane_row(values, lane0, n_rows=1, row=0):
    out = jnp.zeros((n_rows, LANES), F32)
    return out.at[row, lane0:lane0 + values.shape[0]].set(values.astype(F32))


def _split_w_in(w_in):
    qk_a, v_a = N_HEADS * DK_A, N_HEADS * DV
    qk_b, qk_c, vw = N_HEADS * DK_B, N_HEADS * DK_C, N_HEADS * DV
    sizes = (qk_a, qk_a, v_a, v_a, N_HEADS, N_HEADS,
             qk_b, qk_b, vw, vw, N_HEADS, N_HEADS,
             qk_c, qk_c, vw, vw, GLA_RANK)
    names = ("q_a", "k_a", "v_a", "z_a", "a_a", "b_a", "q_b", "k_b", "v_b", "o_b", "i_b", "f_b",
             "q_c", "k_c", "v_c", "z_c", "g_c")
    assert sum(sizes) == w_in.shape[1]
    cols, off = {}, 0
    for name, size in zip(names, sizes):
        cols[name] = w_in[:, off:off + size]
        off += size
    big = jnp.concatenate([cols[n] for n in ("q_a", "k_a", "v_a", "z_a", "q_b", "k_b", "v_b", "o_b",
                                             "q_c", "k_c", "v_c", "z_c")], axis=1).astype(BF16)
    small = jnp.concatenate([cols[n] for n in ("a_a", "b_a", "i_b", "f_b", "g_c")], axis=1)
    small = jnp.pad(small, ((0, 0), (0, LANES - small.shape[1])))
    s_hi = small.astype(BF16)
    s_lo = (small - s_hi.astype(F32)).astype(BF16)
    return big, s_hi, s_lo


def kernel(x, c, w_ada, b_ada, g_mix, g_ffn, g_final, w_in, conv_a, a_log, dt_bias, norm_a, conv_b, b_i, b_f,
           norm_b, w_gla2, b_gla, norm_c, w_br, w_mg, b_mg, w_o, w1_d, w3_d, w2_d, w_router, b_router,
           w1_e, w3_e, w2_e):
    bsz, seq, d = x.shape
    depth = w_ada.shape[0]
    assert seq % MIX_ROWS == 0 and d % LANES == 0
    t = bsz * seq
    xf = x.reshape(t, d)
    mod_all = _ada_call(c, w_ada, b_ada).reshape(depth, bsz, 6, d)

    for layer in range(depth):
        mod = mod_all[layer]
        w_big, ws_hi, ws_lo = _split_w_in(w_in[layer])
        (h, qkv_a, z_a, qk_b, v_b, o_b, qk_c, v_c, z_c, small) = _inproj_call(
            xf, mod, g_mix[layer].reshape(1, d), w_big, ws_hi, ws_lo, seq)

        pa = (_lane_row(-jnp.exp(a_log[layer]), LANE_A_DECAY, SUBLANES, 0)
              + _lane_row(dt_bias[layer], LANE_A_DECAY, SUBLANES, 1))
        y_a = _mixer_a_call(qkv_a, z_a, small, conv_a[layer], pa, norm_a[layer].reshape(1, DV), bsz, seq)

        pb = (_lane_row(b_i[layer], LANE_B_IN, SUBLANES, 0) + _lane_row(b_f[layer], LANE_B_FORGET, SUBLANES, 0))
        y_b = _mixer_b_call(qk_b, v_b, o_b, small, conv_b[layer], pb, norm_b[layer].reshape(1, DV), bsz, seq)

        wg = jnp.zeros((LANES, N_HEADS * DK_C), F32).at[LANE_C_GATE:LANE_C_GATE + GLA_RANK].set(w_gla2[layer])
        y_c = _mixer_c_call(qk_c, v_c, z_c, small, wg, b_gla[layer].reshape(1, -1),
                            norm_c[layer].reshape(1, DV), bsz, seq)

        xf = _merge_call(xf, h, y_a, y_b, y_c, mod, w_mg[layer].astype(BF16), b_mg[layer][:, None, :],
                         w_br[layer].astype(BF16), w_o[layer].astype(BF16), seq)

        j = layer // 2
        last = layer == depth - 1
        if layer % 2 == 0:
            h2 = _prenorm_call(xf, mod, g_ffn[layer].reshape(1, d), seq)
            xf = _ffn_call(h2, w1_d[j].astype(BF16), w3_d[j].astype(BF16), w2_d[j].astype(BF16), xf, mod, seq)
            if last:
                xf = _final_call(xf, g_final.reshape(1, d))
        else:
            n_experts = w_router.shape[2]
            wr = jnp.pad(w_router[j], ((0, 0), (0, LANES - n_experts)))
            br = jnp.pad(b_router[j], (0, LANES - n_experts)).reshape(1, LANES)
            h2, route, counts = _prenorm_call(xf, mod, g_ffn[layer].reshape(1, d), seq, router=(wr, br, n_experts))
            d1, d2, tok, tile_expert, n_used = _moe_layout(route, counts, n_experts, t)
            xs = _moe_gather_call(tok, h2)
            ys = _moe_ffn_call(tile_expert, n_used, xs, w1_e[j].astype(BF16), w3_e[j].astype(BF16),
                               w2_e[j].astype(BF16))
            xf = _moe_combine_call(d1, d2, ys, route, xf, mod, g_final.reshape(1, d), seq, last)

    return xf.reshape(bsz, seq, d)
```

```python
import functools

import jax
import jax.numpy as jnp
from jax import lax
from jax.experimental import pallas as pl
from jax.experimental.pallas import tpu as pltpu

F32 = jnp.float32
BF16 = jnp.bfloat16

EPS = 1e-6
CHUNK = 64
CONV_K = 4
N_HEADS = 4
DV = 128
DK_A = 128
DK_B = 64
DK_C = 64
GLA_RANK = 16
GLA_TAU = 16.0
TOP_K = 2

V7X_MXU_DIM = 256
MIX_ROWS = V7X_MXU_DIM
LANES = 128
SUBLANES = 8
V7X_VMEM_LIMIT = 56 * 1024 * 1024

LANE_A_DECAY = 0
LANE_A_BETA = 4
LANE_B_IN = 8
LANE_B_FORGET = 12
LANE_C_GATE = 16

NN = (((1,), (0,)), ((), ()))
NT = (((1,), (1,)), ((), ()))
TN = (((0,), (0,)), ((), ()))


def _dot(a, b, dims=NN):
    return lax.dot_general(a, b, dims, preferred_element_type=F32)


def _bdot(a, b, dims=NN):
    return _dot(a.astype(BF16), b.astype(BF16), dims)


def _split3(x):
    hi = x.astype(BF16)
    r1 = x - hi.astype(F32)
    mid = r1.astype(BF16)
    lo = (r1 - mid.astype(F32)).astype(BF16)
    return hi, mid, lo


def _dot_exact_lhs(a, b, dims=NN):
    hi, mid, lo = _split3(b)
    return _dot(a, hi, dims) + _dot(a, mid, dims) + _dot(a, lo, dims)


def _dot_hi(a, b, dims=NN):
    a_hi = a.astype(BF16)
    a_lo = (a - a_hi.astype(F32)).astype(BF16)
    b_hi = b.astype(BF16)
    b_lo = (b - b_hi.astype(F32)).astype(BF16)
    return _dot(a_hi, b_hi, dims) + _dot(a_hi, b_lo, dims) + _dot(a_lo, b_hi, dims)


def _sigmoid(x):
    return 1.0 / (1.0 + jnp.exp(-x))


def _silu(x):
    return x * _sigmoid(x)


def _softplus(x):
    return jnp.maximum(x, 0.0) + jnp.log(1.0 + jnp.exp(-jnp.abs(x)))


def _log_sigmoid(x):
    return jnp.minimum(x, 0.0) - jnp.log(1.0 + jnp.exp(-jnp.abs(x)))


def _params(n_axes):
    return pltpu.CompilerParams(
        dimension_semantics=("arbitrary",) * n_axes,
        vmem_limit_bytes=V7X_VMEM_LIMIT,
    )


def _row_tile(t, want):
    tm = min(want, t)
    assert t % tm == 0, (t, tm)
    return tm


def _ada_kernel(c_ref, w_ref, b_ref, o_ref):
    c = c_ref[...]
    o_ref[0] = _bdot(_silu(c), w_ref[0]) + b_ref[0]


def _ada_call(c, w_ada, b_ada):
    depth, d, d6 = w_ada.shape
    bsz = c.shape[0]
    n_col = d6 // d
    return pl.pallas_call(
        _ada_kernel,
        grid=(depth, n_col),
        in_specs=[
            pl.BlockSpec((bsz, d), lambda l, n: (0, 0)),
            pl.BlockSpec((1, d, d), lambda l, n: (l, 0, n)),
            pl.BlockSpec((1, 1, d), lambda l, n: (l, 0, n)),
        ],
        out_specs=pl.BlockSpec((1, bsz, d), lambda l, n: (l, 0, n)),
        out_shape=jax.ShapeDtypeStruct((depth, bsz, d6), F32),
        compiler_params=_params(2),
        name="ada_mod",
    )(c, w_ada, b_ada.reshape(depth, 1, d6))


def _norm_mod(x, g, shift, scale):
    ms = jnp.mean(x * x, axis=-1, keepdims=True)
    return (x * lax.rsqrt(ms + EPS) * g) * (1.0 + scale) + shift


IN_WIDTHS = (1536, 512, 512, 512, 512, 512, 512, 512)


def _inproj_kernel(x_ref, mod_ref, g_ref, w_ref, wsh_ref, wsl_ref, h_ref, *out_refs):
    h = _norm_mod(x_ref[...], g_ref[...], mod_ref[0, 0:1, :], mod_ref[0, 1:2, :])
    hb = h.astype(BF16)
    h_ref[...] = hb.astype(h_ref.dtype)
    off = 0
    for o_ref, width in zip(out_refs[:-1], IN_WIDTHS):
        o_ref[...] = _dot(hb, w_ref[:, off:off + width]).astype(o_ref.dtype)
        off += width
    hl = (h - hb.astype(F32)).astype(BF16)
    wsh = wsh_ref[...]
    out_refs[-1][...] = _dot(hb, wsh) + _dot(hl, wsh) + _dot(hb, wsl_ref[...])


def _inproj_call(x, mod, g, w_big, ws_hi, ws_lo, seq):
    t, d = x.shape
    tm = _row_tile(seq, 512)
    per_seq = seq // tm
    n_big = w_big.shape[1]
    row = lambda i: (i, 0)
    const = lambda i: (0, 0)
    out_shapes = [jax.ShapeDtypeStruct((t, d), BF16)]
    out_specs = [pl.BlockSpec((tm, d), row)]
    for width in IN_WIDTHS:
        out_shapes.append(jax.ShapeDtypeStruct((t, width), BF16))
        out_specs.append(pl.BlockSpec((tm, width), row))
    out_shapes.append(jax.ShapeDtypeStruct((t, LANES), F32))
    out_specs.append(pl.BlockSpec((tm, LANES), row))
    return pl.pallas_call(
        _inproj_kernel,
        grid=(t // tm,),
        in_specs=[
            pl.BlockSpec((tm, d), row),
            pl.BlockSpec((1, 6, d), lambda i: (i // per_seq, 0, 0)),
            pl.BlockSpec((1, d), const),
            pl.BlockSpec((d, n_big), const),
            pl.BlockSpec((d, LANES), const),
            pl.BlockSpec((d, LANES), const),
        ],
        out_specs=out_specs,
        out_shape=out_shapes,
        compiler_params=_params(1),
        name="in_proj",
    )(x, mod, g, w_big, ws_hi, ws_lo)


def _conv_silu(x_ref, cw_ref, cbuf):
    rows = x_ref.shape[0]
    x = x_ref[...].astype(F32)
    cbuf[SUBLANES:SUBLANES + rows, :] = x
    acc = x * cw_ref[CONV_K - 1:CONV_K, :]
    for s in range(1, CONV_K):
        acc = acc + cbuf[SUBLANES - s:SUBLANES - s + rows, :] * cw_ref[CONV_K - 1 - s:CONV_K - s, :]
    cbuf[0:SUBLANES, :] = cbuf[rows:rows + SUBLANES, :]
    return _silu(acc)


def _chunk_masks(rows):
    r = lax.broadcasted_iota(jnp.int32, (rows, rows), 0)
    c = lax.broadcasted_iota(jnp.int32, (rows, rows), 1)
    same = jnp.right_shift(r, 6) == jnp.right_shift(c, 6)
    causal = jnp.logical_and(same, r >= c)
    strict = jnp.logical_and(same, r > c)
    return causal, strict


def _chunk_end_rows(x, rows):
    parts = []
    for c in range(rows // CHUNK):
        last = x[c * CHUNK + CHUNK - 1:c * CHUNK + CHUNK, :]
        parts.append(jnp.broadcast_to(last, (CHUNK, x.shape[1])))
    return jnp.concatenate(parts, axis=0)


def _head_rms(o, w_row):
    ms = jnp.mean(o * o, axis=-1, keepdims=True)
    return o * lax.rsqrt(ms + EPS) * w_row


def _solve_unit_lower(a, rhs):
    x = rhs - _bdot(a, rhs)
    p = _bdot(a, a)
    for it in range(5):
        x = x + _bdot(p, x)
        if it < 4:
            p = _bdot(p, p)
    return x


def _mixer_a_kernel(qkv_ref, z_ref, small_ref, cw_ref, pa_ref, nw_ref, y_ref, cbuf, s_scr):
    rows = qkv_ref.shape[0]
    n_chunks = rows // CHUNK
    hw = N_HEADS * DK_A

    @pl.when(pl.program_id(1) == 0)
    def _():
        cbuf[0:SUBLANES, :] = jnp.zeros((SUBLANES, cbuf.shape[1]), F32)
        s_scr[...] = jnp.zeros(s_scr.shape, F32)

    qkv = _conv_silu(qkv_ref, cw_ref, cbuf)
    causal, strict = _chunk_masks(rows)

    gates = small_ref[...]
    log_decay = pa_ref[0:1, :] * _softplus(gates + pa_ref[1:2, :])
    beta_all = _sigmoid(gates)
    gc = _dot_exact_lhs(causal.astype(BF16), log_decay)
    gc_t = gc.T
    gc_end = _chunk_end_rows(gc, rows)

    heads = range(N_HEADS)
    a_mats, rhs, attns, qgs, k_ends, s_decays = [], [], [], [], [], []
    for h in heads:
        q = qkv[:, h * DK_A:(h + 1) * DK_A]
        k = qkv[:, hw + h * DK_A:hw + (h + 1) * DK_A]
        v = qkv[:, 2 * hw + h * DV:2 * hw + (h + 1) * DV]
        q = q * (lax.rsqrt(jnp.sum(q * q, axis=-1, keepdims=True) + 1e-6) * (DK_A ** -0.5))
        k = k * lax.rsqrt(jnp.sum(k * k, axis=-1, keepdims=True) + 1e-6)
        beta = beta_all[:, LANE_A_BETA + h:LANE_A_BETA + h + 1]
        g_col = gc[:, LANE_A_DECAY + h:LANE_A_DECAY + h + 1]
        g_row = gc_t[LANE_A_DECAY + h:LANE_A_DECAY + h + 1, :]
        ge_col = gc_end[:, LANE_A_DECAY + h:LANE_A_DECAY + h + 1]
        decay = jnp.exp(jnp.where(causal, g_col - g_row, 0.0))
        kb = k * beta
        k16 = k.astype(BF16)
        a_mats.append(jnp.where(strict, _dot(kb.astype(BF16), k16, NT) * decay, 0.0).astype(BF16))
        eg = jnp.exp(g_col)
        rhs.append(jnp.concatenate([v * beta, kb * eg], axis=1))
        attns.append(jnp.where(causal, _dot(q.astype(BF16), k16, NT) * decay, 0.0).astype(BF16))
        qgs.append((q * eg).astype(BF16))
        k_ends.append((k * jnp.exp(ge_col - g_col)).astype(BF16))
        s_decays.append(jnp.exp(ge_col))

    xs = [rhs[h] - _dot(a_mats[h], rhs[h].astype(BF16)) for h in heads]
    ps = [_dot(a_mats[h], a_mats[h]).astype(BF16) for h in heads]
    for it in range(5):
        xs = [xs[h] + _dot(ps[h], xs[h].astype(BF16)) for h in heads]
        if it < 4:
            ps = [_dot(ps[h], ps[h]).astype(BF16) for h in heads]
    us = [xs[h][:, :DV] for h in heads]
    ws = [xs[h][:, DV:].astype(BF16) for h in heads]

    states = [s_scr[h] for h in heads]
    deltas = [[] for _ in heads]
    inters = [[] for _ in heads]
    for c in range(n_chunks):
        lo, hi = c * CHUNK, (c + 1) * CHUNK
        for h in heads:
            s16 = states[h].astype(BF16)
            delta = us[h][lo:hi] - _dot(ws[h][lo:hi], s16)
            inters[h].append(_dot(qgs[h][lo:hi], s16))
            sd = jnp.concatenate([s_decays[h][lo:hi]] * (DK_A // CHUNK), axis=0)
            states[h] = sd * states[h] + _dot(k_ends[h][lo:hi], delta.astype(BF16), TN)
            deltas[h].append(delta.astype(BF16))
    for h in heads:
        s_scr[h] = states[h]
        o = jnp.concatenate(inters[h], axis=0) + _dot(attns[h], jnp.concatenate(deltas[h], axis=0))
        z = z_ref[:, h * DV:(h + 1) * DV].astype(F32)
        y_ref[:, h * DV:(h + 1) * DV] = (_head_rms(o, nw_ref[...]) * _silu(z)).astype(y_ref.dtype)


def _mixer_a_call(qkv, z, small, conv_w, pa, norm_w, bsz, seq):
    per_seq = seq // MIX_ROWS
    cw = qkv.shape[1]
    row = lambda b, j: (b * per_seq + j, 0)
    const = lambda b, j: (0, 0)
    return pl.pallas_call(
        _mixer_a_kernel,
        grid=(bsz, per_seq),
        in_specs=[
            pl.BlockSpec((MIX_ROWS, cw), row),
            pl.BlockSpec((MIX_ROWS, N_HEADS * DV), row),
            pl.BlockSpec((MIX_ROWS, LANES), row),
            pl.BlockSpec((CONV_K, cw), const),
            pl.BlockSpec((SUBLANES, LANES), const),
            pl.BlockSpec((1, DV), const),
        ],
        out_specs=pl.BlockSpec((MIX_ROWS, N_HEADS * DV), row),
        out_shape=jax.ShapeDtypeStruct((bsz * seq, N_HEADS * DV), BF16),
        scratch_shapes=[
            pltpu.VMEM((MIX_ROWS + 2 * SUBLANES, cw), F32),
            pltpu.VMEM((N_HEADS, DK_A, DV), F32),
        ],
        compiler_params=_params(2),
        name="mixer_deltanet",
    )(qkv, z, small, conv_w, pa, norm_w)


def _mixer_b_kernel(qk_ref, v_ref, og_ref, small_ref, cw_ref, pb_ref, nw_ref, y_ref, cbuf, e_scr, m_scr):
    rows = qk_ref.shape[0]
    n_chunks = rows // CHUNK
    hw = N_HEADS * DK_B

    @pl.when(pl.program_id(1) == 0)
    def _():
        cbuf[0:SUBLANES, :] = jnp.zeros((SUBLANES, cbuf.shape[1]), F32)
        e_scr[...] = jnp.zeros(e_scr.shape, F32)
        m_scr[...] = jnp.zeros(m_scr.shape, F32)

    qk = _conv_silu(qk_ref, cw_ref, cbuf)
    causal, _ = _chunk_masks(rows)

    pre = small_ref[...] + pb_ref[0:1, :]
    log_f = _log_sigmoid(pre)
    bcum = _dot_exact_lhs(causal.astype(BF16), log_f)
    bcum_t = bcum.T
    pre_t = pre.T
    bend = _chunk_end_rows(bcum, rows)
    ones = jnp.ones((rows, DV), BF16)

    heads = range(N_HEADS)
    qs, v_exts, b_cols, be_cols, m_intras, m_chunks, wts, k_ws = [], [], [], [], [], [], [], []
    for h in heads:
        q = qk[:, h * DK_B:(h + 1) * DK_B].astype(BF16)
        k = qk[:, hw + h * DK_B:hw + (h + 1) * DK_B] * (DK_B ** -0.5)
        lf = LANE_B_FORGET + h
        li = LANE_B_IN + h
        b_col = bcum[:, lf:lf + 1]
        b_row = bcum_t[lf:lf + 1, :]
        i_row = pre_t[li:li + 1, :]
        i_col = pre[:, li:li + 1]
        be_col = bend[:, lf:lf + 1]
        dlog = jnp.where(causal, b_col - b_row + i_row, -jnp.inf)
        m_intra = jnp.max(dlog, axis=-1, keepdims=True)
        m_chunk = _chunk_end_rows(m_intra, rows)
        wts.append((jnp.exp(dlog - m_intra) * _dot(q, k.astype(BF16), NT)).astype(BF16))
        k_ws.append((k * jnp.exp(be_col - b_col + i_col - m_chunk)).astype(BF16))
        qs.append(q)
        v_exts.append(jnp.concatenate([v_ref[:, h * DV:(h + 1) * DV], ones], axis=1))
        b_cols.append(b_col)
        be_cols.append(be_col)
        m_intras.append(m_intra)
        m_chunks.append(m_chunk)

    intras = [_dot(wts[h], v_exts[h]) for h in heads]
    kvs = [[_dot(k_ws[h][c * CHUNK:(c + 1) * CHUNK], v_exts[h][c * CHUNK:(c + 1) * CHUNK], TN)
            for c in range(n_chunks)] for h in heads]

    es = [e_scr[h] for h in heads]
    m_ss = [m_scr[:, h:h + 1] for h in heads]
    outs = [[] for _ in heads]
    for c in range(n_chunks):
        lo, hi = c * CHUNK, (c + 1) * CHUNK
        for h in heads:
            m_s = m_ss[h]
            m_tot = jnp.maximum(b_cols[h][lo:hi] + m_s, m_intras[h][lo:hi])
            s_inter = jnp.exp(b_cols[h][lo:hi] + m_s - m_tot)
            s_intra = jnp.exp(m_intras[h][lo:hi] - m_tot)
            both = s_inter * _dot(qs[h][lo:hi], es[h].astype(BF16)) + s_intra * intras[h][lo:hi]
            den = jnp.maximum(jnp.abs(both[:, DV:]), jnp.exp(-m_tot))
            outs[h].append(both[:, :DV] / den)
            m_new = jnp.maximum(be_cols[h][lo:hi] + m_s, m_chunks[h][lo:hi])
            s_prev = jnp.exp(be_cols[h][lo:hi] + m_s - m_new)
            s_cur = jnp.exp(m_chunks[h][lo:hi] - m_new)
            es[h] = s_prev * es[h] + s_cur * kvs[h][c]
            m_ss[h] = m_new
    for h in heads:
        e_scr[h] = es[h]
        m_scr[:, h:h + 1] = m_ss[h]
        o = jnp.concatenate(outs[h], axis=0)
        og = og_ref[:, h * DV:(h + 1) * DV].astype(F32)
        y_ref[:, h * DV:(h + 1) * DV] = (_sigmoid(og) * _head_rms(o, nw_ref[...])).astype(y_ref.dtype)


def _mixer_b_call(qk, v, og, small, conv_w, pb, norm_w, bsz, seq):
    per_seq = seq // MIX_ROWS
    cw = qk.shape[1]
    row = lambda b, j: (b * per_seq + j, 0)
    const = lambda b, j: (0, 0)
    return pl.pallas_call(
        _mixer_b_kernel,
        grid=(bsz, per_seq),
        in_specs=[
            pl.BlockSpec((MIX_ROWS, cw), row),
            pl.BlockSpec((MIX_ROWS, N_HEADS * DV), row),
            pl.BlockSpec((MIX_ROWS, N_HEADS * DV), row),
            pl.BlockSpec((MIX_ROWS, LANES), row),
            pl.BlockSpec((CONV_K, cw), const),
            pl.BlockSpec((SUBLANES, LANES), const),
            pl.BlockSpec((1, DV), const),
        ],
        out_specs=pl.BlockSpec((MIX_ROWS, N_HEADS * DV), row),
        out_shape=jax.ShapeDtypeStruct((bsz * seq, N_HEADS * DV), BF16),
        scratch_shapes=[
            pltpu.VMEM((MIX_ROWS + 2 * SUBLANES, cw), F32),
            pltpu.VMEM((N_HEADS, DK_B, 2 * DV), F32),
            pltpu.VMEM((CHUNK, LANES), F32),
        ],
        compiler_params=_params(2),
        name="mixer_mlstm",
    )(qk, v, og, small, conv_w, pb, norm_w)


def _mixer_c_kernel(qk_ref, v_ref, z_ref, small_ref, wg_ref, bg_ref, nw_ref, y_ref, st_scr):
    rows = qk_ref.shape[0]
    n_chunks = rows // CHUNK
    hw = N_HEADS * DK_C

    @pl.when(pl.program_id(1) == 0)
    def _():
        st_scr[...] = jnp.zeros(st_scr.shape, F32)

    causal, _ = _chunk_masks(rows)
    log_a = _log_sigmoid(_dot_hi(small_ref[...], wg_ref[...]) + bg_ref[...]) * (1.0 / GLA_TAU)
    bc = _dot_exact_lhs(causal.astype(BF16), log_a)
    parts = []
    for c in range(n_chunks):
        mid = bc[c * CHUNK + CHUNK // 2:c * CHUNK + CHUNK // 2 + 1, :]
        parts.append(jnp.broadcast_to(mid, (CHUNK, hw)))
    rel = bc - jnp.concatenate(parts, axis=0)
    bc_end = _chunk_end_rows(bc, rows)
    qk = qk_ref[...].astype(F32)
    q_all = qk[:, :hw] * (DK_C ** -0.5)
    k_all = qk[:, hw:]
    q_rel = (q_all * jnp.exp(rel)).astype(BF16)
    k_rel = (k_all * jnp.exp(-rel)).astype(BF16)
    k_end = (k_all * jnp.exp(bc_end - bc)).astype(BF16)
    q_abs = (q_all * jnp.exp(bc)).astype(BF16)
    a_end = jnp.exp(bc_end)

    heads = range(N_HEADS)
    sls = [slice(h * DK_C, (h + 1) * DK_C) for h in heads]
    vs = [v_ref[:, h * DV:(h + 1) * DV] for h in heads]
    attns = [jnp.where(causal, _dot(q_rel[:, sls[h]], k_rel[:, sls[h]], NT), 0.0).astype(BF16) for h in heads]
    o_intras = [_dot(attns[h], vs[h]) for h in heads]
    kvs = [[_dot(vs[h][c * CHUNK:(c + 1) * CHUNK], k_end[c * CHUNK:(c + 1) * CHUNK, sls[h]], TN)
            for c in range(n_chunks)] for h in heads]
    sts = [st_scr[h] for h in heads]
    inters = [[] for _ in heads]
    for c in range(n_chunks):
        lo, hi = c * CHUNK, (c + 1) * CHUNK
        for h in heads:
            inters[h].append(_dot(q_abs[lo:hi, sls[h]], sts[h].astype(BF16), NT))
            sts[h] = sts[h] * a_end[lo:lo + 1, sls[h]] + kvs[h][c]
    for h in heads:
        st_scr[h] = sts[h]
        o = jnp.concatenate(inters[h], axis=0) + o_intras[h]
        z = z_ref[:, h * DV:(h + 1) * DV].astype(F32)
        y_ref[:, h * DV:(h + 1) * DV] = (_head_rms(o, nw_ref[...]) * _silu(z)).astype(y_ref.dtype)


def _mixer_c_call(qk, v, z, small, wg, bg, norm_w, bsz, seq):
    per_seq = seq // MIX_ROWS
    cw = qk.shape[1]
    row = lambda b, j: (b * per_seq + j, 0)
    const = lambda b, j: (0, 0)
    return pl.pallas_call(
        _mixer_c_kernel,
        grid=(bsz, per_seq),
        in_specs=[
            pl.BlockSpec((MIX_ROWS, cw), row),
            pl.BlockSpec((MIX_ROWS, N_HEADS * DV), row),
            pl.BlockSpec((MIX_ROWS, N_HEADS * DV), row),
            pl.BlockSpec((MIX_ROWS, LANES), row),
            pl.BlockSpec((LANES, N_HEADS * DK_C), const),
            pl.BlockSpec((1, N_HEADS * DK_C), const),
            pl.BlockSpec((1, DV), const),
        ],
        out_specs=pl.BlockSpec((MIX_ROWS, N_HEADS * DV), row),
        out_shape=jax.ShapeDtypeStruct((bsz * seq, N_HEADS * DV), BF16),
        scratch_shapes=[pltpu.VMEM((N_HEADS, DV, DK_C), F32)],
        compiler_params=_params(2),
        name="mixer_gla",
    )(qk, v, z, small, wg, bg, norm_w)


def _merge_kernel(x_ref, h_ref, ya_ref, yb_ref, yc_ref, mod_ref, wmg_ref, bmg_ref, wbr_ref, wo_ref, o_ref):
    h = h_ref[...]
    merged = None
    for g, y_ref in enumerate((ya_ref, yb_ref, yc_ref)):
        gate = _sigmoid(_dot(h, wmg_ref[g]) + bmg_ref[g])
        term = gate * _dot(y_ref[...], wbr_ref[g])
        merged = term if merged is None else merged + term
    o_ref[...] = x_ref[...] + mod_ref[0, 2:3, :] * _bdot(merged, wo_ref[...])


def _merge_call(x, h, ya, yb, yc, mod, w_mg, b_mg, w_br, w_o, seq):
    t, d = x.shape
    tm = _row_tile(seq, 512)
    per_seq = seq // tm
    bw = ya.shape[1]
    row = lambda i: (i, 0)
    return pl.pallas_call(
        _merge_kernel,
        grid=(t // tm,),
        in_specs=[
            pl.BlockSpec((tm, d), row),
            pl.BlockSpec((tm, d), row),
            pl.BlockSpec((tm, bw), row),
            pl.BlockSpec((tm, bw), row),
            pl.BlockSpec((tm, bw), row),
            pl.BlockSpec((1, 6, d), lambda i: (i // per_seq, 0, 0)),
            pl.BlockSpec((3, d, d), lambda i: (0, 0, 0)),
            pl.BlockSpec((3, 1, d), lambda i: (0, 0, 0)),
            pl.BlockSpec((3, bw, d), lambda i: (0, 0, 0)),
            pl.BlockSpec((d, d), lambda i: (0, 0)),
        ],
        out_specs=pl.BlockSpec((tm, d), row),
        out_shape=jax.ShapeDtypeStruct((t, d), F32),
        compiler_params=_params(1),
        name="merge",
    )(x, h, ya, yb, yc, mod, w_mg, b_mg, w_br, w_o)


ROUTE_E1, ROUTE_E2, ROUTE_W1, ROUTE_W2, ROUTE_R1, ROUTE_R2 = range(6)


def _router_kernel(n_experts, x_ref, mod_ref, g_ref, wr_ref, br_ref, h_ref, route_ref, count_ref, carry):
    @pl.when(pl.program_id(0) == 0)
    def _():
        carry[...] = jnp.zeros(carry.shape, F32)

    h = _norm_mod(x_ref[...], g_ref[...], mod_ref[0, 3:4, :], mod_ref[0, 4:5, :])
    _store_token_tiles(h_ref, h)
    logits = _dot_hi(h, wr_ref[...]) + br_ref[...]
    tm = logits.shape[0]
    lane = lax.broadcasted_iota(jnp.int32, logits.shape, 1)
    logits = jnp.where(lane < n_experts, logits, -jnp.inf)
    m1 = jnp.max(logits, axis=-1, keepdims=True)
    i1 = jnp.min(jnp.where(logits == m1, lane, LANES), axis=-1, keepdims=True)
    rest = jnp.where(lane == i1, -jnp.inf, logits)
    m2 = jnp.max(rest, axis=-1, keepdims=True)
    i2 = jnp.min(jnp.where(rest == m2, lane, LANES), axis=-1, keepdims=True)
    e2 = jnp.exp(m2 - m1)
    w1 = 1.0 / (1.0 + e2)
    w2 = e2 * w1

    sel1 = lane == i1
    sel2 = lane == i2
    onehot = jnp.where(jnp.logical_or(sel1, sel2), 1.0, 0.0)
    r = lax.broadcasted_iota(jnp.int32, (tm, tm), 0)
    c = lax.broadcasted_iota(jnp.int32, (tm, tm), 1)
    before = jnp.where(r > c, 1.0, 0.0).astype(BF16)
    rank = _dot(before, onehot.astype(BF16)) + carry[0:1, :]
    carry[...] = carry[...] + jnp.sum(onehot, axis=0, keepdims=True)
    count_ref[...] = carry[...]
    r1 = jnp.sum(jnp.where(sel1, rank, 0.0), axis=-1, keepdims=True)
    r2 = jnp.sum(jnp.where(sel2, rank, 0.0), axis=-1, keepdims=True)
    rec = jnp.zeros(logits.shape, F32)
    for pos, val in ((ROUTE_E1, i1.astype(F32)), (ROUTE_E2, i2.astype(F32)), (ROUTE_W1, w1), (ROUTE_W2, w2),
                     (ROUTE_R1, r1), (ROUTE_R2, r2)):
        rec = jnp.where(lane == pos, val, rec)
    route_ref[...] = rec


def _router_call(x, mod, g, wr, br, n_experts, seq):
    t, d = x.shape
    tm = _row_tile(seq, 512)
    per_seq = seq // tm
    row = lambda i: (i, 0)
    const = lambda i: (0, 0)
    return pl.pallas_call(
        functools.partial(_router_kernel, n_experts),
        grid=(t // tm,),
        in_specs=[
            pl.BlockSpec((tm, d), row),
            pl.BlockSpec((1, 6, d), lambda i: (i // per_seq, 0, 0)),
            pl.BlockSpec((1, d), const),
            pl.BlockSpec((d, LANES), const),
            pl.BlockSpec((1, LANES), const),
        ],
        out_specs=[pl.BlockSpec((tm * SUBLANES, LANES), row), pl.BlockSpec((tm, LANES), row),
                   pl.BlockSpec((SUBLANES, LANES), const)],
        out_shape=[jax.ShapeDtypeStruct((t * SUBLANES, LANES), F32), jax.ShapeDtypeStruct((t, LANES), F32),
                   jax.ShapeDtypeStruct((SUBLANES, LANES), F32)],
        scratch_shapes=[pltpu.VMEM((SUBLANES, LANES), F32)],
        compiler_params=_params(1),
        name="prenorm_router",
    )(x, mod, g, wr, br)


MOE_ROWS = 1024
COMBINE_ROWS = 256


TOKEN_TILE = SUBLANES * LANES


def _store_token_tiles(ref, x):
    n = x.shape[0]
    for j in range(SUBLANES):
        ref[pl.ds(j, n, stride=SUBLANES), :] = x[:, j * LANES:(j + 1) * LANES]


def _load_token_tiles(ref):
    n = ref.shape[0] // SUBLANES
    return jnp.concatenate([ref[pl.ds(j, n, stride=SUBLANES), :] for j in range(SUBLANES)], axis=1)


def _tile_gather_start(row_ref, src_hbm, dst_ref, sem):
    def issue(i, carry):
        for k in range(SUBLANES):
            r = i * SUBLANES + k
            src_row = pl.multiple_of(row_ref[r], SUBLANES)
            dst_row = pl.multiple_of(r * SUBLANES, SUBLANES)
            pltpu.make_async_copy(src_hbm.at[pl.ds(src_row, SUBLANES)], dst_ref.at[pl.ds(dst_row, SUBLANES)],
                                  sem).start(priority=k % 2)
        return carry

    lax.fori_loop(0, dst_ref.shape[0] // (SUBLANES * SUBLANES), issue, 0)


def _tile_gather_wait(src_hbm, dst_ref, sem):
    pltpu.make_async_copy(src_hbm.at[pl.ds(0, dst_ref.shape[0])], dst_ref, sem).wait()


def _moe_gather_kernel(row_ref, src_hbm, o_ref, sem):
    _tile_gather_start(row_ref, src_hbm, o_ref, sem)
    _tile_gather_wait(src_hbm, o_ref, sem)


def _moe_gather_call(tok, src):
    p = tok.shape[0]
    return pl.pallas_call(
        _moe_gather_kernel,
        grid=(p // MOE_ROWS,),
        in_specs=[
            pl.BlockSpec((MOE_ROWS,), lambda i: (i,), memory_space=pltpu.SMEM),
            pl.BlockSpec(memory_space=pl.ANY),
        ],
        out_specs=pl.BlockSpec((MOE_ROWS * SUBLANES, LANES), lambda i: (i, 0)),
        out_shape=jax.ShapeDtypeStruct((p * SUBLANES, LANES), src.dtype),
        scratch_shapes=[pltpu.SemaphoreType.DMA(())],
        compiler_params=_params(1),
        name="moe_gather",
    )(tok * SUBLANES, src)


def _moe_ffn_kernel(te_ref, nu_ref, xs_ref, w1_ref, w3_ref, w2_ref, ys_ref, hb_scr, acc_ref):
    i = pl.program_id(0)
    f = pl.program_id(1)
    last = pl.num_programs(1) - 1
    used = i < nu_ref[0]

    @pl.when(jnp.logical_and(used, f == 0))
    def _():
        hb_scr[...] = _load_token_tiles(xs_ref).astype(BF16)
        acc_ref[...] = jnp.zeros(acc_ref.shape, F32)

    @pl.when(used)
    def _():
        h = hb_scr[...]
        g = _silu(_dot(h, w1_ref[0])) * _dot(h, w3_ref[0])
        acc_ref[...] += _bdot(g, w2_ref[0])

    @pl.when(jnp.logical_and(used, f == last))
    def _():
        _store_token_tiles(ys_ref, acc_ref[...])

    @pl.when(jnp.logical_and(jnp.logical_not(used), f == last))
    def _():
        ys_ref[...] = jnp.zeros(ys_ref.shape, F32)


def _moe_ffn_call(tile_expert, n_used, xs, w1, w3, w2):
    p = xs.shape[0] // SUBLANES
    d = TOKEN_TILE
    d_ff = w1.shape[2]
    tf = _ffn_tile(d_ff, FFN_MOE_TILE_CAP)
    tiles =pl.BlockSpec((MOE_ROWS * SUBLANES, LANES), lambda i, f, te, nu: (i, 0))
    grid_spec = pltpu.PrefetchScalarGridSpec(
        num_scalar_prefetch=2,
        grid=(p // MOE_ROWS, d_ff // tf),
        in_specs=[
            tiles,
            pl.BlockSpec((1, d, tf), lambda i, f, te, nu: (te[i], 0, f)),
            pl.BlockSpec((1, d, tf), lambda i, f, te, nu: (te[i], 0, f)),
            pl.BlockSpec((1, tf, d), lambda i, f, te, nu: (te[i], f, 0)),
        ],
        out_specs=tiles,
        scratch_shapes=[pltpu.VMEM((MOE_ROWS, d), BF16), pltpu.VMEM((MOE_ROWS, d), F32)],
    )
    return pl.pallas_call(
        _moe_ffn_kernel,
        grid_spec=grid_spec,
        out_shape=jax.ShapeDtypeStruct((p * SUBLANES, LANES), F32),
        compiler_params=_params(2),
        name="moe_ffn",
    )(tile_expert, n_used, xs, w1, w3, w2)


def _moe_combine_kernel(final, r1_ref, r2_ref, ys_hbm, route_ref, x_ref, mod_ref, gf_ref, o_ref, buf, sem):
    _tile_gather_start(r1_ref, ys_hbm, buf.at[0], sem.at[0])
    _tile_gather_start(r2_ref, ys_hbm, buf.at[1], sem.at[1])
    _tile_gather_wait(ys_hbm, buf.at[0], sem.at[0])
    _tile_gather_wait(ys_hbm, buf.at[1], sem.at[1])
    route = route_ref[...]
    f = (route[:, ROUTE_W1:ROUTE_W1 + 1] * _load_token_tiles(buf.at[0])
         + route[:, ROUTE_W2:ROUTE_W2 + 1] * _load_token_tiles(buf.at[1]))
    y = x_ref[...] + mod_ref[0, 5:6, :] * f
    if final:
        ms = jnp.mean(y * y, axis=-1, keepdims=True)
        y = y * lax.rsqrt(ms + EPS) * gf_ref[...]
    o_ref[...] = y


def _moe_combine_call(d1, d2, ys, route, x, mod, g_final, seq, final):
    t, d = x.shape
    tm = _row_tile(seq, COMBINE_ROWS)
    per_seq = seq // tm
    row = lambda i: (i, 0)
    idx_spec = pl.BlockSpec((tm,), lambda i: (i,), memory_space=pltpu.SMEM)
    return pl.pallas_call(
        functools.partial(_moe_combine_kernel, final),
        grid=(t // tm,),
        in_specs=[
            idx_spec, idx_spec,
            pl.BlockSpec(memory_space=pl.ANY),
            pl.BlockSpec((tm, LANES), row),
            pl.BlockSpec((tm, d), row),
            pl.BlockSpec((1, 6, d), lambda i: (i // per_seq, 0, 0)),
            pl.BlockSpec((1, d), lambda i: (0, 0)),
        ],
        out_specs=pl.BlockSpec((tm, d), row),
        out_shape=jax.ShapeDtypeStruct((t, d), F32),
        scratch_shapes=[pltpu.VMEM((2, tm * SUBLANES, LANES), F32), pltpu.SemaphoreType.DMA((2,))],
        compiler_params=_params(1),
        name="moe_combine",
    )(d1 * SUBLANES, d2 * SUBLANES, ys, route, x, mod, g_final)


def _moe_layout(route, counts, n_experts, t):
    cnt = counts[0, :n_experts].astype(jnp.int32)
    padded = ((cnt + MOE_ROWS - 1) // MOE_ROWS) * MOE_ROWS
    ends = jnp.cumsum(padded)
    offs = ends - padded
    e1 = route[:, ROUTE_E1].astype(jnp.int32)
    e2 = route[:, ROUTE_E2].astype(jnp.int32)
    d1 = offs[e1] + route[:, ROUTE_R1].astype(jnp.int32)
    d2 = offs[e2] + route[:, ROUTE_R2].astype(jnp.int32)
    p = (TOP_K * t // MOE_ROWS + n_experts) * MOE_ROWS
    ids = jnp.arange(t, dtype=jnp.int32)
    tok = jnp.zeros((p,), jnp.int32).at[jnp.concatenate([d1, d2])].set(jnp.concatenate([ids, ids]))
    starts = jnp.arange(p // MOE_ROWS, dtype=jnp.int32) * MOE_ROWS
    tile_expert = jnp.minimum(jnp.searchsorted(ends, starts, side="right"), n_experts - 1).astype(jnp.int32)
    n_used = (ends[-1:] // MOE_ROWS).astype(jnp.int32)
    return d1, d2, tok, tile_expert, n_used


def _ffn_kernel(x_ref, mod_ref, g_ref, w1_ref, w3_ref, w2_ref, o_ref):
    x = x_ref[...]
    h = _norm_mod(x, g_ref[...], mod_ref[0, 3:4, :], mod_ref[0, 4:5, :]).astype(BF16)
    d_ff = w1_ref.shape[1]
    tf = _ffn_tile(d_ff, FFN_DENSE_TILE_CAP)
    acc = None
    for f0 in range(0, d_ff, tf):
        g = _silu(_dot(h, w1_ref[:, f0:f0 + tf])) * _dot(h, w3_ref[:, f0:f0 + tf])
        part = _bdot(g, w2_ref[f0:f0 + tf, :])
        acc = part if acc is None else acc + part
    o_ref[...] = x + mod_ref[0, 5:6, :] * acc


FFN_DENSE_TILE_CAP = 1536
FFN_MOE_TILE_CAP = 1024


def _ffn_tile(d_ff, cap):
    best = None
    for tf in range(LANES, min(cap, d_ff) + 1, LANES):
        if d_ff % tf == 0:
            best = tf
    if best is None:
        raise ValueError(d_ff)
    return best


def _ffn_call(x, mod, g, w1, w3, w2, seq):
    t, d = x.shape
    d_ff = w1.shape[1]
    tm = _row_tile(seq, 512)
    per_seq = seq // tm
    row = lambda i: (i, 0)
    const = lambda i: (0, 0)
    resident = dict(pipeline_mode=pl.Buffered(1))
    return pl.pallas_call(
        _ffn_kernel,
        grid=(t // tm,),
        in_specs=[
            pl.BlockSpec((tm, d), row),
            pl.BlockSpec((1, 6, d), lambda i: (i // per_seq, 0, 0)),
            pl.BlockSpec((1, d), const),
            pl.BlockSpec((d, d_ff), const, **resident),
            pl.BlockSpec((d, d_ff), const, **resident),
            pl.BlockSpec((d_ff, d), const, **resident),
        ],
        out_specs=pl.BlockSpec((tm, d), row),
        out_shape=jax.ShapeDtypeStruct((t, d), F32),
        compiler_params=_params(1),
        name="ffn_dense",
    )(x, mod, g, w1, w3, w2)


def _final_kernel(x_ref, g_ref, o_ref):
    x = x_ref[...]
    ms = jnp.mean(x * x, axis=-1, keepdims=True)
    o_ref[...] = x * lax.rsqrt(ms + EPS) * g_ref[...]


def _final_call(x, g):
    t, d = x.shape
    tm = _row_tile(t, 512)
    return pl.pallas_call(
        _final_kernel,
        grid=(t // tm,),
        in_specs=[pl.BlockSpec((tm, d), lambda i: (i, 0)), pl.BlockSpec((1, d), lambda i: (0, 0))],
        out_specs=pl.BlockSpec((tm, d), lambda i: (i, 0)),
        out_shape=jax.ShapeDtypeStruct((t, d), F32),
        compiler_params=_params(1),
        name="final_norm",
    )(x, g)


def _lane_row(values, lane0, n_rows=1, row=0):
    out = jnp.zeros((n_rows, LANES), F32)
    return out.at[row, lane0:lane0 + values.shape[0]].set(values.astype(F32))


def _split_w_in(w_in):
    qk_a, v_a = N_HEADS * DK_A, N_HEADS * DV
    qk_b, qk_c, vw = N_HEADS * DK_B, N_HEADS * DK_C, N_HEADS * DV
    sizes = (qk_a, qk_a, v_a, v_a, N_HEADS, N_HEADS,
             qk_b, qk_b, vw, vw, N_HEADS, N_HEADS,
             qk_c, qk_c, vw, vw, GLA_RANK)
    names = ("q_a", "k_a", "v_a", "z_a", "a_a", "b_a", "q_b", "k_b", "v_b", "o_b", "i_b", "f_b",
             "q_c", "k_c", "v_c", "z_c", "g_c")
    assert sum(sizes) == w_in.shape[1]
    cols, off = {}, 0
    for name, size in zip(names, sizes):
        cols[name] = w_in[:, off:off + size]
        off += size
    big = jnp.concatenate([cols[n] for n in ("q_a", "k_a", "v_a", "z_a", "q_b", "k_b", "v_b", "o_b",
                                             "q_c", "k_c", "v_c", "z_c")], axis=1).astype(BF16)
    small = jnp.concatenate([cols[n] for n in ("a_a", "b_a", "i_b", "f_b", "g_c")], axis=1)
    small = jnp.pad(small, ((0, 0), (0, LANES - small.shape[1])))
    s_hi = small.astype(BF16)
    s_lo = (small - s_hi.astype(F32)).astype(BF16)
    return big, s_hi, s_lo


def kernel(x, c, w_ada, b_ada, g_mix, g_ffn, g_final, w_in, conv_a, a_log, dt_bias, norm_a, conv_b, b_i, b_f,
           norm_b, w_gla2, b_gla, norm_c, w_br, w_mg, b_mg, w_o, w1_d, w3_d, w2_d, w_router, b_router,
           w1_e, w3_e, w2_e):
    bsz, seq, d = x.shape
    depth = w_ada.shape[0]
    assert seq % MIX_ROWS == 0 and d % LANES == 0
    t = bsz * seq
    xf = x.reshape(t, d)
    mod_all = _ada_call(c, w_ada, b_ada).reshape(depth, bsz, 6, d)

    for layer in range(depth):
        mod = mod_all[layer]
        w_big, ws_hi, ws_lo = _split_w_in(w_in[layer])
        (h, qkv_a, z_a, qk_b, v_b, o_b, qk_c, v_c, z_c, small) = _inproj_call(
            xf, mod, g_mix[layer].reshape(1, d), w_big, ws_hi, ws_lo, seq)

        pa = (_lane_row(-jnp.exp(a_log[layer]), LANE_A_DECAY, SUBLANES, 0)
              + _lane_row(dt_bias[layer], LANE_A_DECAY, SUBLANES, 1))
        y_a = _mixer_a_call(qkv_a, z_a, small, conv_a[layer], pa, norm_a[layer].reshape(1, DV), bsz, seq)

        pb = (_lane_row(b_i[layer], LANE_B_IN, SUBLANES, 0) + _lane_row(b_f[layer], LANE_B_FORGET, SUBLANES, 0))
        y_b = _mixer_b_call(qk_b, v_b, o_b, small, conv_b[layer], pb, norm_b[layer].reshape(1, DV), bsz, seq)

        wg = jnp.zeros((LANES, N_HEADS * DK_C), F32).at[LANE_C_GATE:LANE_C_GATE + GLA_RANK].set(w_gla2[layer])
        y_c = _mixer_c_call(qk_c, v_c, z_c, small, wg, b_gla[layer].reshape(1, -1),
                            norm_c[layer].reshape(1, DV), bsz, seq)

        xf = _merge_call(xf, h, y_a, y_b, y_c, mod, w_mg[layer].astype(BF16), b_mg[layer][:, None, :],
                         w_br[layer].astype(BF16), w_o[layer].astype(BF16), seq)

        j = layer // 2
        last = layer == depth - 1
        if layer % 2 == 0:
            xf = _ffn_call(xf, mod, g_ffn[layer].reshape(1, d), w1_d[j].astype(BF16), w3_d[j].astype(BF16),
                           w2_d[j].astype(BF16), seq)
            if last:
                xf = _final_call(xf, g_final.reshape(1, d))
        else:
            n_experts = w_router.shape[2]
            wr = jnp.pad(w_router[j], ((0, 0), (0, LANES - n_experts)))
            br = jnp.pad(b_router[j], (0, LANES - n_experts)).reshape(1, LANES)
            assert d == TOKEN_TILE, "gathered rows are stored as one (SUBLANES, LANES) tile per token"
            h2, route, counts = _router_call(xf, mod, g_ffn[layer].reshape(1, d), wr, br, n_experts, seq)
            d1, d2, tok, tile_expert, n_used = _moe_layout(route, counts, n_experts, t)
            xs = _moe_gather_call(tok, h2)
            ys = _moe_ffn_call(tile_expert, n_used, xs, w1_e[j].astype(BF16), w3_e[j].astype(BF16),
                               w2_e[j].astype(BF16))
            xf = _moe_combine_call(d1, d2, ys, route, xf, mod, g_final.reshape(1, d), seq, last)

    return xf.reshape(bsz, seq, d)
```

```python
import functools

import jax
import jax.numpy as jnp
from jax import lax
from jax.experimental import pallas as pl
from jax.experimental.pallas import tpu as pltpu

F32 = jnp.float32
BF16 = jnp.bfloat16

EPS = 1e-6
CHUNK = 64
CONV_K = 4
N_HEADS = 4
DV = 128
DK_A = 128
DK_B = 64
DK_C = 64
GLA_RANK = 16
GLA_TAU = 16.0
TOP_K = 2

V7X_MXU_DIM = 256
MIX_ROWS = V7X_MXU_DIM
LANES = 128
SUBLANES = 8
V7X_VMEM_LIMIT = 56 * 1024 * 1024

LANE_A_DECAY = 0
LANE_A_BETA = 4
LANE_B_IN = 8
LANE_B_FORGET = 12
LANE_C_GATE = 16

NN = (((1,), (0,)), ((), ()))
NT = (((1,), (1,)), ((), ()))
TN = (((0,), (0,)), ((), ()))


def _dot(a, b, dims=NN):
    return lax.dot_general(a, b, dims, preferred_element_type=F32)


def _bdot(a, b, dims=NN):
    return _dot(a.astype(BF16), b.astype(BF16), dims)


def _split3(x):
    hi = x.astype(BF16)
    r1 = x - hi.astype(F32)
    mid = r1.astype(BF16)
    lo = (r1 - mid.astype(F32)).astype(BF16)
    return hi, mid, lo


def _dot_exact_lhs(a, b, dims=NN):
    hi, mid, lo = _split3(b)
    return _dot(a, hi, dims) + _dot(a, mid, dims) + _dot(a, lo, dims)


def _dot_hi(a, b, dims=NN):
    a_hi = a.astype(BF16)
    a_lo = (a - a_hi.astype(F32)).astype(BF16)
    b_hi = b.astype(BF16)
    b_lo = (b - b_hi.astype(F32)).astype(BF16)
    return _dot(a_hi, b_hi, dims) + _dot(a_hi, b_lo, dims) + _dot(a_lo, b_hi, dims)


def _sigmoid(x):
    return 1.0 / (1.0 + jnp.exp(-x))


def _silu(x):
    return x * _sigmoid(x)


def _softplus(x):
    return jnp.maximum(x, 0.0) + jnp.log(1.0 + jnp.exp(-jnp.abs(x)))


def _log_sigmoid(x):
    return jnp.minimum(x, 0.0) - jnp.log(1.0 + jnp.exp(-jnp.abs(x)))


def _params(n_axes):
    return pltpu.CompilerParams(
        dimension_semantics=("arbitrary",) * n_axes,
        vmem_limit_bytes=V7X_VMEM_LIMIT,
    )


def _row_tile(t, want):
    tm = min(want, t)
    assert t % tm == 0, (t, tm)
    return tm


def _ada_kernel(c_ref, w_ref, b_ref, o_ref):
    c = c_ref[...]
    o_ref[0] = _bdot(_silu(c), w_ref[0]) + b_ref[0]


def _ada_call(c, w_ada, b_ada):
    depth, d, d6 = w_ada.shape
    bsz = c.shape[0]
    n_col = d6 // d
    return pl.pallas_call(
        _ada_kernel,
        grid=(depth, n_col),
        in_specs=[
            pl.BlockSpec((bsz, d), lambda l, n: (0, 0)),
            pl.BlockSpec((1, d, d), lambda l, n: (l, 0, n)),
            pl.BlockSpec((1, 1, d), lambda l, n: (l, 0, n)),
        ],
        out_specs=pl.BlockSpec((1, bsz, d), lambda l, n: (l, 0, n)),
        out_shape=jax.ShapeDtypeStruct((depth, bsz, d6), F32),
        compiler_params=_params(2),
        name="ada_mod",
    )(c, w_ada, b_ada.reshape(depth, 1, d6))


def _norm_mod(x, g, shift, scale):
    ms = jnp.mean(x * x, axis=-1, keepdims=True)
    return (x * lax.rsqrt(ms + EPS) * g) * (1.0 + scale) + shift


IN_WIDTHS = (1536, 512, 512, 512, 512, 512, 512, 512)


def _inproj_kernel(x_ref, mod_ref, g_ref, w_ref, wsh_ref, wsl_ref, h_ref, *out_refs):
    h = _norm_mod(x_ref[...], g_ref[...], mod_ref[0, 0:1, :], mod_ref[0, 1:2, :])
    hb = h.astype(BF16)
    h_ref[...] = hb.astype(h_ref.dtype)
    off = 0
    for o_ref, width in zip(out_refs[:-1], IN_WIDTHS):
        o_ref[...] = _dot(hb, w_ref[:, off:off + width]).astype(o_ref.dtype)
        off += width
    hl = (h - hb.astype(F32)).astype(BF16)
    wsh = wsh_ref[...]
    out_refs[-1][...] = _dot(hb, wsh) + _dot(hl, wsh) + _dot(hb, wsl_ref[...])


def _inproj_call(x, mod, g, w_big, ws_hi, ws_lo, seq):
    t, d = x.shape
    tm = _row_tile(seq, 512)
    per_seq = seq // tm
    n_big = w_big.shape[1]
    row = lambda i: (i, 0)
    const = lambda i: (0, 0)
    out_shapes = [jax.ShapeDtypeStruct((t, d), BF16)]
    out_specs = [pl.BlockSpec((tm, d), row)]
    for width in IN_WIDTHS:
        out_shapes.append(jax.ShapeDtypeStruct((t, width), BF16))
        out_specs.append(pl.BlockSpec((tm, width), row))
    out_shapes.append(jax.ShapeDtypeStruct((t, LANES), F32))
    out_specs.append(pl.BlockSpec((tm, LANES), row))
    return pl.pallas_call(
        _inproj_kernel,
        grid=(t // tm,),
        in_specs=[
            pl.BlockSpec((tm, d), row),
            pl.BlockSpec((1, 6, d), lambda i: (i // per_seq, 0, 0)),
            pl.BlockSpec((1, d), const),
            pl.BlockSpec((d, n_big), const),
            pl.BlockSpec((d, LANES), const),
            pl.BlockSpec((d, LANES), const),
        ],
        out_specs=out_specs,
        out_shape=out_shapes,
        compiler_params=_params(1),
        name="in_proj",
    )(x, mod, g, w_big, ws_hi, ws_lo)


def _conv_silu(x_ref, cw_ref, cbuf):
    rows = x_ref.shape[0]
    x = x_ref[...].astype(F32)
    cbuf[SUBLANES:SUBLANES + rows, :] = x
    acc = x * cw_ref[CONV_K - 1:CONV_K, :]
    for s in range(1, CONV_K):
        acc = acc + cbuf[SUBLANES - s:SUBLANES - s + rows, :] * cw_ref[CONV_K - 1 - s:CONV_K - s, :]
    cbuf[0:SUBLANES, :] = cbuf[rows:rows + SUBLANES, :]
    return _silu(acc)


def _chunk_masks(rows):
    r = lax.broadcasted_iota(jnp.int32, (rows, rows), 0)
    c = lax.broadcasted_iota(jnp.int32, (rows, rows), 1)
    same = jnp.right_shift(r, 6) == jnp.right_shift(c, 6)
    causal = jnp.logical_and(same, r >= c)
    strict = jnp.logical_and(same, r > c)
    return causal, strict


def _chunk_end_rows(x, rows):
    parts = []
    for c in range(rows // CHUNK):
        last = x[c * CHUNK + CHUNK - 1:c * CHUNK + CHUNK, :]
        parts.append(jnp.broadcast_to(last, (CHUNK, x.shape[1])))
    return jnp.concatenate(parts, axis=0)


def _head_rms(o, w_row):
    ms = jnp.mean(o * o, axis=-1, keepdims=True)
    return o * lax.rsqrt(ms + EPS) * w_row


def _solve_unit_lower(a, rhs):
    x = rhs - _bdot(a, rhs)
    p = _bdot(a, a)
    for it in range(5):
        x = x + _bdot(p, x)
        if it < 4:
            p = _bdot(p, p)
    return x


def _mixer_a_kernel(qkv_ref, z_ref, small_ref, cw_ref, pa_ref, nw_ref, y_ref, cbuf, s_scr):
    rows = qkv_ref.shape[0]
    n_chunks = rows // CHUNK
    hw = N_HEADS * DK_A

    @pl.when(pl.program_id(1) == 0)
    def _():
        cbuf[0:SUBLANES, :] = jnp.zeros((SUBLANES, cbuf.shape[1]), F32)
        s_scr[...] = jnp.zeros(s_scr.shape, F32)

    qkv = _conv_silu(qkv_ref, cw_ref, cbuf)
    causal, strict = _chunk_masks(rows)

    gates = small_ref[...]
    log_decay = pa_ref[0:1, :] * _softplus(gates + pa_ref[1:2, :])
    beta_all = _sigmoid(gates)
    gc = _dot_exact_lhs(causal.astype(BF16), log_decay)
    gc_t = gc.T
    gc_end = _chunk_end_rows(gc, rows)

    heads = range(N_HEADS)
    a_mats, rhs, attns, qgs, k_ends, s_decays = [], [], [], [], [], []
    for h in heads:
        q = qkv[:, h * DK_A:(h + 1) * DK_A]
        k = qkv[:, hw + h * DK_A:hw + (h + 1) * DK_A]
        v = qkv[:, 2 * hw + h * DV:2 * hw + (h + 1) * DV]
        q = q * (lax.rsqrt(jnp.sum(q * q, axis=-1, keepdims=True) + 1e-6) * (DK_A ** -0.5))
        k = k * lax.rsqrt(jnp.sum(k * k, axis=-1, keepdims=True) + 1e-6)
        beta = beta_all[:, LANE_A_BETA + h:LANE_A_BETA + h + 1]
        g_col = gc[:, LANE_A_DECAY + h:LANE_A_DECAY + h + 1]
        g_row = gc_t[LANE_A_DECAY + h:LANE_A_DECAY + h + 1, :]
        ge_col = gc_end[:, LANE_A_DECAY + h:LANE_A_DECAY + h + 1]
        decay = jnp.exp(jnp.where(causal, g_col - g_row, 0.0))
        kb = k * beta
        k16 = k.astype(BF16)
        a_mats.append(jnp.where(strict, _dot(kb.astype(BF16), k16, NT) * decay, 0.0).astype(BF16))
        eg = jnp.exp(g_col)
        rhs.append(jnp.concatenate([v * beta, kb * eg], axis=1))
        attns.append(jnp.where(causal, _dot(q.astype(BF16), k16, NT) * decay, 0.0).astype(BF16))
        qgs.append((q * eg).astype(BF16))
        k_ends.append((k * jnp.exp(ge_col - g_col)).astype(BF16))
        s_decays.append(jnp.exp(ge_col))

    xs = [rhs[h] - _dot(a_mats[h], rhs[h].astype(BF16)) for h in heads]
    ps = [_dot(a_mats[h], a_mats[h]).astype(BF16) for h in heads]
    for it in range(5):
        xs = [xs[h] + _dot(ps[h], xs[h].astype(BF16)) for h in heads]
        if it < 4:
            ps = [_dot(ps[h], ps[h]).astype(BF16) for h in heads]
    us = [xs[h][:, :DV] for h in heads]
    ws = [xs[h][:, DV:].astype(BF16) for h in heads]

    states = [s_scr[h] for h in heads]
    deltas = [[] for _ in heads]
    inters = [[] for _ in heads]
    for c in range(n_chunks):
        lo, hi = c * CHUNK, (c + 1) * CHUNK
        for h in heads:
            s16 = states[h].astype(BF16)
            delta = us[h][lo:hi] - _dot(ws[h][lo:hi], s16)
            inters[h].append(_dot(qgs[h][lo:hi], s16))
            sd = jnp.concatenate([s_decays[h][lo:hi]] * (DK_A // CHUNK), axis=0)
            states[h] = sd * states[h] + _dot(k_ends[h][lo:hi], delta.astype(BF16), TN)
            deltas[h].append(delta.astype(BF16))
    for h in heads:
        s_scr[h] = states[h]
        o = jnp.concatenate(inters[h], axis=0) + _dot(attns[h], jnp.concatenate(deltas[h], axis=0))
        z = z_ref[:, h * DV:(h + 1) * DV].astype(F32)
        y_ref[:, h * DV:(h + 1) * DV] = (_head_rms(o, nw_ref[...]) * _silu(z)).astype(y_ref.dtype)


def _mixer_a_call(qkv, z, small, conv_w, pa, norm_w, bsz, seq):
    per_seq = seq // MIX_ROWS
    cw = qkv.shape[1]
    row = lambda b, j: (b * per_seq + j, 0)
    const = lambda b, j: (0, 0)
    return pl.pallas_call(
        _mixer_a_kernel,
        grid=(bsz, per_seq),
        in_specs=[
            pl.BlockSpec((MIX_ROWS, cw), row),
            pl.BlockSpec((MIX_ROWS, N_HEADS * DV), row),
            pl.BlockSpec((MIX_ROWS, LANES), row),
            pl.BlockSpec((CONV_K, cw), const),
            pl.BlockSpec((SUBLANES, LANES), const),
            pl.BlockSpec((1, DV), const),
        ],
        out_specs=pl.BlockSpec((MIX_ROWS, N_HEADS * DV), row),
        out_shape=jax.ShapeDtypeStruct((bsz * seq, N_HEADS * DV), BF16),
        scratch_shapes=[
            pltpu.VMEM((MIX_ROWS + 2 * SUBLANES, cw), F32),
            pltpu.VMEM((N_HEADS, DK_A, DV), F32),
        ],
        compiler_params=_params(2),
        name="mixer_deltanet",
    )(qkv, z, small, conv_w, pa, norm_w)


def _mixer_b_kernel(qk_ref, v_ref, og_ref, small_ref, cw_ref, pb_ref, nw_ref, y_ref, cbuf, e_scr, m_scr):
    rows = qk_ref.shape[0]
    n_chunks = rows // CHUNK
    hw = N_HEADS * DK_B

    @pl.when(pl.program_id(1) == 0)
    def _():
        cbuf[0:SUBLANES, :] = jnp.zeros((SUBLANES, cbuf.shape[1]), F32)
        e_scr[...] = jnp.zeros(e_scr.shape, F32)
        m_scr[...] = jnp.zeros(m_scr.shape, F32)

    qk = _conv_silu(qk_ref, cw_ref, cbuf)
    causal, _ = _chunk_masks(rows)

    pre = small_ref[...] + pb_ref[0:1, :]
    log_f = _log_sigmoid(pre)
    bcum = _dot_exact_lhs(causal.astype(BF16), log_f)
    bcum_t = bcum.T
    pre_t = pre.T
    bend = _chunk_end_rows(bcum, rows)
    ones = jnp.ones((rows, DV), BF16)

    heads = range(N_HEADS)
    qs, v_exts, b_cols, be_cols, m_intras, m_chunks, wts, k_ws = [], [], [], [], [], [], [], []
    for h in heads:
        q = qk[:, h * DK_B:(h + 1) * DK_B].astype(BF16)
        k = qk[:, hw + h * DK_B:hw + (h + 1) * DK_B] * (DK_B ** -0.5)
        lf = LANE_B_FORGET + h
        li = LANE_B_IN + h
        b_col = bcum[:, lf:lf + 1]
        b_row = bcum_t[lf:lf + 1, :]
        i_row = pre_t[li:li + 1, :]
        i_col = pre[:, li:li + 1]
        be_col = bend[:, lf:lf + 1]
        dlog = jnp.where(causal, b_col - b_row + i_row, -jnp.inf)
        m_intra = jnp.max(dlog, axis=-1, keepdims=True)
        m_chunk = _chunk_end_rows(m_intra, rows)
        wts.append((jnp.exp(dlog - m_intra) * _dot(q, k.astype(BF16), NT)).astype(BF16))
        k_ws.append((k * jnp.exp(be_col - b_col + i_col - m_chunk)).astype(BF16))
        qs.append(q)
        v_exts.append(jnp.concatenate([v_ref[:, h * DV:(h + 1) * DV], ones], axis=1))
        b_cols.append(b_col)
        be_cols.append(be_col)
        m_intras.append(m_intra)
        m_chunks.append(m_chunk)

    intras = [_dot(wts[h], v_exts[h]) for h in heads]
    kvs = [[_dot(k_ws[h][c * CHUNK:(c + 1) * CHUNK], v_exts[h][c * CHUNK:(c + 1) * CHUNK], TN)
            for c in range(n_chunks)] for h in heads]

    es = [e_scr[h] for h in heads]
    m_ss = [m_scr[:, h:h + 1] for h in heads]
    outs = [[] for _ in heads]
    for c in range(n_chunks):
        lo, hi = c * CHUNK, (c + 1) * CHUNK
        for h in heads:
            m_s = m_ss[h]
            m_tot = jnp.maximum(b_cols[h][lo:hi] + m_s, m_intras[h][lo:hi])
            s_inter = jnp.exp(b_cols[h][lo:hi] + m_s - m_tot)
            s_intra = jnp.exp(m_intras[h][lo:hi] - m_tot)
            both = s_inter * _dot(qs[h][lo:hi], es[h].astype(BF16)) + s_intra * intras[h][lo:hi]
            den = jnp.maximum(jnp.abs(both[:, DV:]), jnp.exp(-m_tot))
            outs[h].append(both[:, :DV] / den)
            m_new = jnp.maximum(be_cols[h][lo:hi] + m_s, m_chunks[h][lo:hi])
            s_prev = jnp.exp(be_cols[h][lo:hi] + m_s - m_new)
            s_cur = jnp.exp(m_chunks[h][lo:hi] - m_new)
            es[h] = s_prev * es[h] + s_cur * kvs[h][c]
            m_ss[h] = m_new
    for h in heads:
        e_scr[h] = es[h]
        m_scr[:, h:h + 1] = m_ss[h]
        o = jnp.concatenate(outs[h], axis=0)
        og = og_ref[:, h * DV:(h + 1) * DV].astype(F32)
        y_ref[:, h * DV:(h + 1) * DV] = (_sigmoid(og) * _head_rms(o, nw_ref[...])).astype(y_ref.dtype)


def _mixer_b_call(qk, v, og, small, conv_w, pb, norm_w, bsz, seq):
    per_seq = seq // MIX_ROWS
    cw = qk.shape[1]
    row = lambda b, j: (b * per_seq + j, 0)
    const = lambda b, j: (0, 0)
    return pl.pallas_call(
        _mixer_b_kernel,
        grid=(bsz, per_seq),
        in_specs=[
            pl.BlockSpec((MIX_ROWS, cw), row),
            pl.BlockSpec((MIX_ROWS, N_HEADS * DV), row),
            pl.BlockSpec((MIX_ROWS, N_HEADS * DV), row),
            pl.BlockSpec((MIX_ROWS, LANES), row),
            pl.BlockSpec((CONV_K, cw), const),
            pl.BlockSpec((SUBLANES, LANES), const),
            pl.BlockSpec((1, DV), const),
        ],
        out_specs=pl.BlockSpec((MIX_ROWS, N_HEADS * DV), row),
        out_shape=jax.ShapeDtypeStruct((bsz * seq, N_HEADS * DV), BF16),
        scratch_shapes=[
            pltpu.VMEM((MIX_ROWS + 2 * SUBLANES, cw), F32),
            pltpu.VMEM((N_HEADS, DK_B, 2 * DV), F32),
            pltpu.VMEM((CHUNK, LANES), F32),
        ],
        compiler_params=_params(2),
        name="mixer_mlstm",
    )(qk, v, og, small, conv_w, pb, norm_w)


def _mixer_c_kernel(qk_ref, v_ref, z_ref, small_ref, wg_ref, bg_ref, nw_ref, y_ref, st_scr):
    rows = qk_ref.shape[0]
    n_chunks = rows // CHUNK
    hw = N_HEADS * DK_C

    @pl.when(pl.program_id(1) == 0)
    def _():
        st_scr[...] = jnp.zeros(st_scr.shape, F32)

    causal, _ = _chunk_masks(rows)
    log_a = _log_sigmoid(_dot_hi(small_ref[...], wg_ref[...]) + bg_ref[...]) * (1.0 / GLA_TAU)
    bc = _dot_exact_lhs(causal.astype(BF16), log_a)
    parts = []
    for c in range(n_chunks):
        mid = bc[c * CHUNK + CHUNK // 2:c * CHUNK + CHUNK // 2 + 1, :]
        parts.append(jnp.broadcast_to(mid, (CHUNK, hw)))
    rel = bc - jnp.concatenate(parts, axis=0)
    bc_end = _chunk_end_rows(bc, rows)
    qk = qk_ref[...].astype(F32)
    q_all = qk[:, :hw] * (DK_C ** -0.5)
    k_all = qk[:, hw:]
    q_rel = (q_all * jnp.exp(rel)).astype(BF16)
    k_rel = (k_all * jnp.exp(-rel)).astype(BF16)
    k_end = (k_all * jnp.exp(bc_end - bc)).astype(BF16)
    q_abs = (q_all * jnp.exp(bc)).astype(BF16)
    a_end = jnp.exp(bc_end)

    heads = range(N_HEADS)
    sls = [slice(h * DK_C, (h + 1) * DK_C) for h in heads]
    vs = [v_ref[:, h * DV:(h + 1) * DV] for h in heads]
    attns = [jnp.where(causal, _dot(q_rel[:, sls[h]], k_rel[:, sls[h]], NT), 0.0).astype(BF16) for h in heads]
    o_intras = [_dot(attns[h], vs[h]) for h in heads]
    kvs = [[_dot(vs[h][c * CHUNK:(c + 1) * CHUNK], k_end[c * CHUNK:(c + 1) * CHUNK, sls[h]], TN)
            for c in range(n_chunks)] for h in heads]
    sts = [st_scr[h] for h in heads]
    inters = [[] for _ in heads]
    for c in range(n_chunks):
        lo, hi = c * CHUNK, (c + 1) * CHUNK
        for h in heads:
            inters[h].append(_dot(q_abs[lo:hi, sls[h]], sts[h].astype(BF16), NT))
            sts[h] = sts[h] * a_end[lo:lo + 1, sls[h]] + kvs[h][c]
    for h in heads:
        st_scr[h] = sts[h]
        o = jnp.concatenate(inters[h], axis=0) + o_intras[h]
        z = z_ref[:, h * DV:(h + 1) * DV].astype(F32)
        y_ref[:, h * DV:(h + 1) * DV] = (_head_rms(o, nw_ref[...]) * _silu(z)).astype(y_ref.dtype)


def _mixer_c_call(qk, v, z, small, wg, bg, norm_w, bsz, seq):
    per_seq = seq // MIX_ROWS
    cw = qk.shape[1]
    row = lambda b, j: (b * per_seq + j, 0)
    const = lambda b, j: (0, 0)
    return pl.pallas_call(
        _mixer_c_kernel,
        grid=(bsz, per_seq),
        in_specs=[
            pl.BlockSpec((MIX_ROWS, cw), row),
            pl.BlockSpec((MIX_ROWS, N_HEADS * DV), row),
            pl.BlockSpec((MIX_ROWS, N_HEADS * DV), row),
            pl.BlockSpec((MIX_ROWS, LANES), row),
            pl.BlockSpec((LANES, N_HEADS * DK_C), const),
            pl.BlockSpec((1, N_HEADS * DK_C), const),
            pl.BlockSpec((1, DV), const),
        ],
        out_specs=pl.BlockSpec((MIX_ROWS, N_HEADS * DV), row),
        out_shape=jax.ShapeDtypeStruct((bsz * seq, N_HEADS * DV), BF16),
        scratch_shapes=[pltpu.VMEM((N_HEADS, DV, DK_C), F32)],
        compiler_params=_params(2),
        name="mixer_gla",
    )(qk, v, z, small, wg, bg, norm_w)


def _merge_kernel(x_ref, h_ref, ya_ref, yb_ref, yc_ref, mod_ref, wmg_ref, bmg_ref, wbr_ref, wo_ref, o_ref):
    h = h_ref[...]
    merged = None
    for g, y_ref in enumerate((ya_ref, yb_ref, yc_ref)):
        gate = _sigmoid(_dot(h, wmg_ref[g]) + bmg_ref[g])
        term = gate * _dot(y_ref[...], wbr_ref[g])
        merged = term if merged is None else merged + term
    o_ref[...] = x_ref[...] + mod_ref[0, 2:3, :] * _bdot(merged, wo_ref[...])


def _merge_call(x, h, ya, yb, yc, mod, w_mg, b_mg, w_br, w_o, seq):
    t, d = x.shape
    tm = _row_tile(seq, 512)
    per_seq = seq // tm
    bw = ya.shape[1]
    row = lambda i: (i, 0)
    return pl.pallas_call(
        _merge_kernel,
        grid=(t // tm,),
        in_specs=[
            pl.BlockSpec((tm, d), row),
            pl.BlockSpec((tm, d), row),
            pl.BlockSpec((tm, bw), row),
            pl.BlockSpec((tm, bw), row),
            pl.BlockSpec((tm, bw), row),
            pl.BlockSpec((1, 6, d), lambda i: (i // per_seq, 0, 0)),
            pl.BlockSpec((3, d, d), lambda i: (0, 0, 0)),
            pl.BlockSpec((3, 1, d), lambda i: (0, 0, 0)),
            pl.BlockSpec((3, bw, d), lambda i: (0, 0, 0)),
            pl.BlockSpec((d, d), lambda i: (0, 0)),
        ],
        out_specs=pl.BlockSpec((tm, d), row),
        out_shape=jax.ShapeDtypeStruct((t, d), F32),
        compiler_params=_params(1),
        name="merge",
    )(x, h, ya, yb, yc, mod, w_mg, b_mg, w_br, w_o)


ROUTE_E1, ROUTE_E2, ROUTE_W1, ROUTE_W2, ROUTE_R1, ROUTE_R2 = range(6)


def _router_kernel(n_experts, x_ref, mod_ref, g_ref, wr_ref, br_ref, h_ref, route_ref, count_ref, carry):
    @pl.when(pl.program_id(0) == 0)
    def _():
        carry[...] = jnp.zeros(carry.shape, F32)

    h = _norm_mod(x_ref[...], g_ref[...], mod_ref[0, 3:4, :], mod_ref[0, 4:5, :])
    _store_token_tiles(h_ref, h)
    logits = _dot_hi(h, wr_ref[...]) + br_ref[...]
    tm = logits.shape[0]
    lane = lax.broadcasted_iota(jnp.int32, logits.shape, 1)
    logits = jnp.where(lane < n_experts, logits, -jnp.inf)
    m1 = jnp.max(logits, axis=-1, keepdims=True)
    i1 = jnp.min(jnp.where(logits == m1, lane, LANES), axis=-1, keepdims=True)
    rest = jnp.where(lane == i1, -jnp.inf, logits)
    m2 = jnp.max(rest, axis=-1, keepdims=True)
    i2 = jnp.min(jnp.where(rest == m2, lane, LANES), axis=-1, keepdims=True)
    e2 = jnp.exp(m2 - m1)
    w1 = 1.0 / (1.0 + e2)
    w2 = e2 * w1

    sel1 = lane == i1
    sel2 = lane == i2
    onehot = jnp.where(jnp.logical_or(sel1, sel2), 1.0, 0.0)
    r = lax.broadcasted_iota(jnp.int32, (tm, tm), 0)
    c = lax.broadcasted_iota(jnp.int32, (tm, tm), 1)
    before = jnp.where(r > c, 1.0, 0.0).astype(BF16)
    rank = _dot(before, onehot.astype(BF16)) + carry[0:1, :]
    carry[...] = carry[...] + jnp.sum(onehot, axis=0, keepdims=True)
    count_ref[...] = carry[...]
    r1 = jnp.sum(jnp.where(sel1, rank, 0.0), axis=-1, keepdims=True)
    r2 = jnp.sum(jnp.where(sel2, rank, 0.0), axis=-1, keepdims=True)
    rec = jnp.zeros(logits.shape, F32)
    for pos, val in ((ROUTE_E1, i1.astype(F32)), (ROUTE_E2, i2.astype(F32)), (ROUTE_W1, w1), (ROUTE_W2, w2),
                     (ROUTE_R1, r1), (ROUTE_R2, r2)):
        rec = jnp.where(lane == pos, val, rec)
    route_ref[...] = rec


def _router_call(x, mod, g, wr, br, n_experts, seq):
    t, d = x.shape
    tm = _row_tile(seq, 512)
    per_seq = seq // tm
    row = lambda i: (i, 0)
    const = lambda i: (0, 0)
    return pl.pallas_call(
        functools.partial(_router_kernel, n_experts),
        grid=(t // tm,),
        in_specs=[
            pl.BlockSpec((tm, d), row),
            pl.BlockSpec((1, 6, d), lambda i: (i // per_seq, 0, 0)),
            pl.BlockSpec((1, d), const),
            pl.BlockSpec((d, LANES), const),
            pl.BlockSpec((1, LANES), const),
        ],
        out_specs=[pl.BlockSpec((tm * SUBLANES, LANES), row), pl.BlockSpec((tm, LANES), row),
                   pl.BlockSpec((SUBLANES, LANES), const)],
        out_shape=[jax.ShapeDtypeStruct((t * SUBLANES, LANES), F32), jax.ShapeDtypeStruct((t, LANES), F32),
                   jax.ShapeDtypeStruct((SUBLANES, LANES), F32)],
        scratch_shapes=[pltpu.VMEM((SUBLANES, LANES), F32)],
        compiler_params=_params(1),
        name="prenorm_router",
    )(x, mod, g, wr, br)


MOE_ROWS = 1024
COMBINE_ROWS = 256


TOKEN_TILE = SUBLANES * LANES


def _store_token_tiles(ref, x):
    n = x.shape[0]
    for j in range(SUBLANES):
        ref[pl.ds(j, n, stride=SUBLANES), :] = x[:, j * LANES:(j + 1) * LANES]


def _load_token_tiles(ref):
    n = ref.shape[0] // SUBLANES
    return jnp.concatenate([ref[pl.ds(j, n, stride=SUBLANES), :] for j in range(SUBLANES)], axis=1)


def _tile_gather_start(row_ref, src_hbm, dst_ref, sem):
    def issue(i, carry):
        for k in range(SUBLANES):
            r = i * SUBLANES + k
            src_row = pl.multiple_of(row_ref[r], SUBLANES)
            dst_row = pl.multiple_of(r * SUBLANES, SUBLANES)
            pltpu.make_async_copy(src_hbm.at[pl.ds(src_row, SUBLANES)], dst_ref.at[pl.ds(dst_row, SUBLANES)],
                                  sem).start(priority=k % 2)
        return carry

    lax.fori_loop(0, dst_ref.shape[0] // (SUBLANES * SUBLANES), issue, 0)


def _tile_gather_wait(src_hbm, dst_ref, sem):
    pltpu.make_async_copy(src_hbm.at[pl.ds(0, dst_ref.shape[0])], dst_ref, sem).wait()


def _moe_ffn_kernel(te_ref, nu_ref, rows_ref, rows_next_ref, h_hbm, w1_ref, w3_ref, w2_ref, ys_ref,
                    xbuf, sem, hb_scr, acc_ref):
    i = pl.program_id(0)
    f = pl.program_id(1)
    last = pl.num_programs(1) - 1
    n_used = nu_ref[0]
    used = i < n_used
    slot = lax.rem(i, 2)

    @pl.when(jnp.logical_and(f == 0, jnp.logical_and(i == 0, used)))
    def _():
        _tile_gather_start(rows_ref, h_hbm, xbuf.at[0], sem.at[0])

    @pl.when(jnp.logical_and(f == 0, i + 1 < n_used))
    def _():
        _tile_gather_start(rows_next_ref, h_hbm, xbuf.at[1 - slot], sem.at[1 - slot])

    @pl.when(jnp.logical_and(used, f == 0))
    def _():
        _tile_gather_wait(h_hbm, xbuf.at[slot], sem.at[slot])
        hb_scr[...] = _load_token_tiles(xbuf.at[slot]).astype(BF16)
        acc_ref[...] = jnp.zeros(acc_ref.shape, F32)

    @pl.when(used)
    def _():
        h = hb_scr[...]
        g = _silu(_dot(h, w1_ref[0])) * _dot(h, w3_ref[0])
        acc_ref[...] += _bdot(g, w2_ref[0])

    @pl.when(jnp.logical_and(used, f == last))
    def _():
        _store_token_tiles(ys_ref, acc_ref[...])

    @pl.when(jnp.logical_and(jnp.logical_not(used), f == last))
    def _():
        ys_ref[...] = jnp.zeros(ys_ref.shape, F32)


def _moe_ffn_call(tile_expert, n_used, tok, h_tiles, w1, w3, w2):
    p = tok.shape[0]
    n_tiles = p // MOE_ROWS
    d = TOKEN_TILE
    d_ff = w1.shape[2]
    tf = _ffn_tile(d_ff, FFN_MOE_TILE_CAP)
    rows = tok * SUBLANES
    grid_spec = pltpu.PrefetchScalarGridSpec(
        num_scalar_prefetch=2,
        grid=(n_tiles, d_ff // tf),
        in_specs=[
            pl.BlockSpec((MOE_ROWS,), lambda i, f, te, nu: (i,), memory_space=pltpu.SMEM),
            pl.BlockSpec((MOE_ROWS,), lambda i, f, te, nu: (jnp.minimum(i + 1, n_tiles - 1),),
                         memory_space=pltpu.SMEM),
            pl.BlockSpec(memory_space=pl.ANY),
            pl.BlockSpec((1, d, tf), lambda i, f, te, nu: (te[i], 0, f)),
            pl.BlockSpec((1, d, tf), lambda i, f, te, nu: (te[i], 0, f)),
            pl.BlockSpec((1, tf, d), lambda i, f, te, nu: (te[i], f, 0)),
        ],
        out_specs=pl.BlockSpec((MOE_ROWS * SUBLANES, LANES), lambda i, f, te, nu: (i, 0)),
        scratch_shapes=[
            pltpu.VMEM((2, MOE_ROWS * SUBLANES, LANES), F32),
            pltpu.SemaphoreType.DMA((2,)),
            pltpu.VMEM((MOE_ROWS, d), BF16),
            pltpu.VMEM((MOE_ROWS, d), F32),
        ],
    )
    return pl.pallas_call(
        _moe_ffn_kernel,
        grid_spec=grid_spec,
        out_shape=jax.ShapeDtypeStruct((p * SUBLANES, LANES), F32),
        compiler_params=_params(2),
        name="moe_ffn",
    )(tile_expert, n_used, rows, rows, h_tiles, w1, w3, w2)


def _moe_combine_kernel(final, r1_ref, r2_ref, ys_hbm, route_ref, x_ref, mod_ref, gf_ref, o_ref, buf, sem):
    _tile_gather_start(r1_ref, ys_hbm, buf.at[0], sem.at[0])
    _tile_gather_start(r2_ref, ys_hbm, buf.at[1], sem.at[1])
    _tile_gather_wait(ys_hbm, buf.at[0], sem.at[0])
    _tile_gather_wait(ys_hbm, buf.at[1], sem.at[1])
    route = route_ref[...]
    f = (route[:, ROUTE_W1:ROUTE_W1 + 1] * _load_token_tiles(buf.at[0])
         + route[:, ROUTE_W2:ROUTE_W2 + 1] * _load_token_tiles(buf.at[1]))
    y = x_ref[...] + mod_ref[0, 5:6, :] * f
    if final:
        ms = jnp.mean(y * y, axis=-1, keepdims=True)
        y = y * lax.rsqrt(ms + EPS) * gf_ref[...]
    o_ref[...] = y


def _moe_combine_call(d1, d2, ys, route, x, mod, g_final, seq, final):
    t, d = x.shape
    tm = _row_tile(seq, COMBINE_ROWS)
    per_seq = seq // tm
    row = lambda i: (i, 0)
    idx_spec = pl.BlockSpec((tm,), lambda i: (i,), memory_space=pltpu.SMEM)
    return pl.pallas_call(
        functools.partial(_moe_combine_kernel, final),
        grid=(t // tm,),
        in_specs=[
            idx_spec, idx_spec,
            pl.BlockSpec(memory_space=pl.ANY),
            pl.BlockSpec((tm, LANES), row),
            pl.BlockSpec((tm, d), row),
            pl.BlockSpec((1, 6, d), lambda i: (i // per_seq, 0, 0)),
            pl.BlockSpec((1, d), lambda i: (0, 0)),
        ],
        out_specs=pl.BlockSpec((tm, d), row),
        out_shape=jax.ShapeDtypeStruct((t, d), F32),
        scratch_shapes=[pltpu.VMEM((2, tm * SUBLANES, LANES), F32), pltpu.SemaphoreType.DMA((2,))],
        compiler_params=_params(1),
        name="moe_combine",
    )(d1 * SUBLANES, d2 * SUBLANES, ys, route, x, mod, g_final)


def _moe_layout(route, counts, n_experts, t):
    cnt = counts[0, :n_experts].astype(jnp.int32)
    padded = ((cnt + MOE_ROWS - 1) // MOE_ROWS) * MOE_ROWS
    ends = jnp.cumsum(padded)
    offs = ends - padded
    e1 = route[:, ROUTE_E1].astype(jnp.int32)
    e2 = route[:, ROUTE_E2].astype(jnp.int32)
    d1 = offs[e1] + route[:, ROUTE_R1].astype(jnp.int32)
    d2 = offs[e2] + route[:, ROUTE_R2].astype(jnp.int32)
    p = (TOP_K * t // MOE_ROWS + n_experts) * MOE_ROWS
    ids = jnp.arange(t, dtype=jnp.int32)
    tok = jnp.zeros((p,), jnp.int32).at[jnp.concatenate([d1, d2])].set(jnp.concatenate([ids, ids]))
    starts = jnp.arange(p // MOE_ROWS, dtype=jnp.int32) * MOE_ROWS
    tile_expert = jnp.minimum(jnp.searchsorted(ends, starts, side="right"), n_experts - 1).astype(jnp.int32)
    n_used = (ends[-1:] // MOE_ROWS).astype(jnp.int32)
    return d1, d2, tok, tile_expert, n_used


def _ffn_kernel(x_ref, mod_ref, g_ref, w1_ref, w3_ref, w2_ref, o_ref):
    x = x_ref[...]
    h = _norm_mod(x, g_ref[...], mod_ref[0, 3:4, :], mod_ref[0, 4:5, :]).astype(BF16)
    d_ff = w1_ref.shape[1]
    tf = _ffn_tile(d_ff, FFN_DENSE_TILE_CAP)
    acc = None
    for f0 in range(0, d_ff, tf):
        g = _silu(_dot(h, w1_ref[:, f0:f0 + tf])) * _dot(h, w3_ref[:, f0:f0 + tf])
        part = _bdot(g, w2_ref[f0:f0 + tf, :])
        acc = part if acc is None else acc + part
    o_ref[...] = x + mod_ref[0, 5:6, :] * acc


FFN_DENSE_TILE_CAP = 1536
FFN_MOE_TILE_CAP = 512


def _ffn_tile(d_ff, cap):
    best = None
    for tf in range(LANES, min(cap, d_ff) + 1, LANES):
        if d_ff % tf == 0:
            best = tf
    if best is None:
        raise ValueError(d_ff)
    return best


def _ffn_call(x, mod, g, w1, w3, w2, seq):
    t, d = x.shape
    d_ff = w1.shape[1]
    tm = _row_tile(seq, 512)
    per_seq = seq // tm
    row = lambda i: (i, 0)
    const = lambda i: (0, 0)
    resident = dict(pipeline_mode=pl.Buffered(1))
    return pl.pallas_call(
        _ffn_kernel,
        grid=(t // tm,),
        in_specs=[
            pl.BlockSpec((tm, d), row),
            pl.BlockSpec((1, 6, d), lambda i: (i // per_seq, 0, 0)),
            pl.BlockSpec((1, d), const),
            pl.BlockSpec((d, d_ff), const, **resident),
            pl.BlockSpec((d, d_ff), const, **resident),
            pl.BlockSpec((d_ff, d), const, **resident),
        ],
        out_specs=pl.BlockSpec((tm, d), row),
        out_shape=jax.ShapeDtypeStruct((t, d), F32),
        compiler_params=_params(1),
        name="ffn_dense",
    )(x, mod, g, w1, w3, w2)


def _final_kernel(x_ref, g_ref, o_ref):
    x = x_ref[...]
    ms = jnp.mean(x * x, axis=-1, keepdims=True)
    o_ref[...] = x * lax.rsqrt(ms + EPS) * g_ref[...]


def _final_call(x, g):
    t, d = x.shape
    tm = _row_tile(t, 512)
    return pl.pallas_call(
        _final_kernel,
        grid=(t // tm,),
        in_specs=[pl.BlockSpec((tm, d), lambda i: (i, 0)), pl.BlockSpec((1, d), lambda i: (0, 0))],
        out_specs=pl.BlockSpec((tm, d), lambda i: (i, 0)),
        out_shape=jax.ShapeDtypeStruct((t, d), F32),
        compiler_params=_params(1),
        name="final_norm",
    )(x, g)


def _lane_row(values, lane0, n_rows=1, row=0):
    out = jnp.zeros((n_rows, LANES), F32)
    return out.at[row, lane0:lane0 + values.shape[0]].set(values.astype(F32))


def _split_w_in(w_in):
    qk_a, v_a = N_HEADS * DK_A, N_HEADS * DV
    qk_b, qk_c, vw = N_HEADS * DK_B, N_HEADS * DK_C, N_HEADS * DV
    sizes = (qk_a, qk_a, v_a, v_a, N_HEADS, N_HEADS,
             qk_b, qk_b, vw, vw, N_HEADS, N_HEADS,
             qk_c, qk_c, vw, vw, GLA_RANK)
    names = ("q_a", "k_a", "v_a", "z_a", "a_a", "b_a", "q_b", "k_b", "v_b", "o_b", "i_b", "f_b",
             "q_c", "k_c", "v_c", "z_c", "g_c")
    assert sum(sizes) == w_in.shape[1]
    cols, off = {}, 0
    for name, size in zip(names, sizes):
        cols[name] = w_in[:, off:off + size]
        off += size
    big = jnp.concatenate([cols[n] for n in ("q_a", "k_a", "v_a", "z_a", "q_b", "k_b", "v_b", "o_b",
                                             "q_c", "k_c", "v_c", "z_c")], axis=1).astype(BF16)
    small = jnp.concatenate([cols[n] for n in ("a_a", "b_a", "i_b", "f_b", "g_c")], axis=1)
    small = jnp.pad(small, ((0, 0), (0, LANES - small.shape[1])))
    s_hi = small.astype(BF16)
    s_lo = (small - s_hi.astype(F32)).astype(BF16)
    return big, s_hi, s_lo


def kernel(x, c, w_ada, b_ada, g_mix, g_ffn, g_final, w_in, conv_a, a_log, dt_bias, norm_a, conv_b, b_i, b_f,
           norm_b, w_gla2, b_gla, norm_c, w_br, w_mg, b_mg, w_o, w1_d, w3_d, w2_d, w_router, b_router,
           w1_e, w3_e, w2_e):
    bsz, seq, d = x.shape
    depth = w_ada.shape[0]
    assert seq % MIX_ROWS == 0 and d % LANES == 0
    t = bsz * seq
    xf = x.reshape(t, d)
    mod_all = _ada_call(c, w_ada, b_ada).reshape(depth, bsz, 6, d)

    for layer in range(depth):
        mod = mod_all[layer]
        w_big, ws_hi, ws_lo = _split_w_in(w_in[layer])
        (h, qkv_a, z_a, qk_b, v_b, o_b, qk_c, v_c, z_c, small) = _inproj_call(
            xf, mod, g_mix[layer].reshape(1, d), w_big, ws_hi, ws_lo, seq)

        pa = (_lane_row(-jnp.exp(a_log[layer]), LANE_A_DECAY, SUBLANES, 0)
              + _lane_row(dt_bias[layer], LANE_A_DECAY, SUBLANES, 1))
        y_a = _mixer_a_call(qkv_a, z_a, small, conv_a[layer], pa, norm_a[layer].reshape(1, DV), bsz, seq)

        pb = (_lane_row(b_i[layer], LANE_B_IN, SUBLANES, 0) + _lane_row(b_f[layer], LANE_B_FORGET, SUBLANES, 0))
        y_b = _mixer_b_call(qk_b, v_b, o_b, small, conv_b[layer], pb, norm_b[layer].reshape(1, DV), bsz, seq)

        wg = jnp.zeros((LANES, N_HEADS * DK_C), F32).at[LANE_C_GATE:LANE_C_GATE + GLA_RANK].set(w_gla2[layer])
        y_c = _mixer_c_call(qk_c, v_c, z_c, small, wg, b_gla[layer].reshape(1, -1),
                            norm_c[layer].reshape(1, DV), bsz, seq)

        xf = _merge_call(xf, h, y_a, y_b, y_c, mod, w_mg[layer].astype(BF16), b_mg[layer][:, None, :],
                         w_br[layer].astype(BF16), w_o[layer].astype(BF16), seq)

        j = layer // 2
        last = layer == depth - 1
        if layer % 2 == 0:
            xf = _ffn_call(xf, mod, g_ffn[layer].reshape(1, d), w1_d[j].astype(BF16), w3_d[j].astype(BF16),
                           w2_d[j].astype(BF16), seq)
            if last:
                xf = _final_call(xf, g_final.reshape(1, d))
        else:
            n_experts = w_router.shape[2]
            wr = jnp.pad(w_router[j], ((0, 0), (0, LANES - n_experts)))
            br = jnp.pad(b_router[j], (0, LANES - n_experts)).reshape(1, LANES)
            assert d == TOKEN_TILE, "gathered rows are stored as one (SUBLANES, LANES) tile per token"
            h2, route, counts = _router_call(xf, mod, g_ffn[layer].reshape(1, d), wr, br, n_experts, seq)
            d1, d2, tok, tile_expert, n_used = _moe_layout(route, counts, n_experts, t)
            ys = _moe_ffn_call(tile_expert, n_used, tok, h2, w1_e[j].astype(BF16), w3_e[j].astype(BF16),
                               w2_e[j].astype(BF16))
            xf = _moe_combine_call(d1, d2, ys, route, xf, mod, g_final.reshape(1, d), seq, last)

    return xf.reshape(bsz, seq, d)
```

```python
import functools

import jax
import jax.numpy as jnp
from jax import lax
from jax.experimental import pallas as pl
from jax.experimental.pallas import tpu as pltpu

F32 = jnp.float32
BF16 = jnp.bfloat16

EPS = 1e-6
CHUNK = 64
CONV_K = 4
N_HEADS = 4
DV = 128
DK_A = 128
DK_B = 64
DK_C = 64
GLA_RANK = 16
GLA_TAU = 16.0
TOP_K = 2

V7X_MXU_DIM = 256
MIX_ROWS = V7X_MXU_DIM
LANES = 128
SUBLANES = 8
V7X_VMEM_LIMIT = 56 * 1024 * 1024

LANE_A_DECAY = 0
LANE_A_BETA = 4
LANE_B_IN = 8
LANE_B_FORGET = 12
LANE_C_GATE = 16

NN = (((1,), (0,)), ((), ()))
NT = (((1,), (1,)), ((), ()))
TN = (((0,), (0,)), ((), ()))


def _dot(a, b, dims=NN):
    return lax.dot_general(a, b, dims, preferred_element_type=F32)


def _bdot(a, b, dims=NN):
    return _dot(a.astype(BF16), b.astype(BF16), dims)


def _split3(x):
    hi = x.astype(BF16)
    r1 = x - hi.astype(F32)
    mid = r1.astype(BF16)
    lo = (r1 - mid.astype(F32)).astype(BF16)
    return hi, mid, lo


def _dot_exact_lhs(a, b, dims=NN):
    hi, mid, lo = _split3(b)
    return _dot(a, hi, dims) + _dot(a, mid, dims) + _dot(a, lo, dims)


def _dot_hi(a, b, dims=NN):
    a_hi = a.astype(BF16)
    a_lo = (a - a_hi.astype(F32)).astype(BF16)
    b_hi = b.astype(BF16)
    b_lo = (b - b_hi.astype(F32)).astype(BF16)
    return _dot(a_hi, b_hi, dims) + _dot(a_hi, b_lo, dims) + _dot(a_lo, b_hi, dims)


def _sigmoid(x):
    return 1.0 / (1.0 + jnp.exp(-x))


def _silu(x):
    return x * _sigmoid(x)


def _softplus(x):
    return jnp.maximum(x, 0.0) + jnp.log(1.0 + jnp.exp(-jnp.abs(x)))


def _log_sigmoid(x):
    return jnp.minimum(x, 0.0) - jnp.log(1.0 + jnp.exp(-jnp.abs(x)))


def _params(n_axes):
    return pltpu.CompilerParams(
        dimension_semantics=("arbitrary",) * n_axes,
        vmem_limit_bytes=V7X_VMEM_LIMIT,
    )


def _row_tile(t, want):
    tm = min(want, t)
    assert t % tm == 0, (t, tm)
    return tm


def _ada_kernel(c_ref, w_ref, b_ref, o_ref):
    c = c_ref[...]
    o_ref[0] = _bdot(_silu(c), w_ref[0]) + b_ref[0]


def _ada_call(c, w_ada, b_ada):
    depth, d, d6 = w_ada.shape
    bsz = c.shape[0]
    n_col = d6 // d
    return pl.pallas_call(
        _ada_kernel,
        grid=(depth, n_col),
        in_specs=[
            pl.BlockSpec((bsz, d), lambda l, n: (0, 0)),
            pl.BlockSpec((1, d, d), lambda l, n: (l, 0, n)),
            pl.BlockSpec((1, 1, d), lambda l, n: (l, 0, n)),
        ],
        out_specs=pl.BlockSpec((1, bsz, d), lambda l, n: (l, 0, n)),
        out_shape=jax.ShapeDtypeStruct((depth, bsz, d6), F32),
        compiler_params=_params(2),
        name="ada_mod",
    )(c, w_ada, b_ada.reshape(depth, 1, d6))


def _norm_mod(x, g, shift, scale):
    ms = jnp.mean(x * x, axis=-1, keepdims=True)
    return (x * lax.rsqrt(ms + EPS) * g) * (1.0 + scale) + shift


IN_WIDTHS = (1536, 512, 512, 512, 512, 512, 512, 512)


ROW_SPLIT = 2


def _row_halves(tm):
    half = tm // ROW_SPLIT
    return [slice(s * half, (s + 1) * half) for s in range(ROW_SPLIT)]


def _inproj_kernel(x_ref, mod_ref, g_ref, w_ref, wsh_ref, wsl_ref, h_ref, *out_refs):
    halves = _row_halves(x_ref.shape[0])
    hs = []
    for rows in halves:
        h = _norm_mod(x_ref[rows, :], g_ref[...], mod_ref[0, 0:1, :], mod_ref[0, 1:2, :])
        hb = h.astype(BF16)
        h_ref[rows, :] = hb.astype(h_ref.dtype)
        hs.append((h, hb))
    for rows, (h, hb) in zip(halves, hs):
        off = 0
        for o_ref, width in zip(out_refs[:-1], IN_WIDTHS):
            o_ref[rows, :] = _dot(hb, w_ref[:, off:off + width]).astype(o_ref.dtype)
            off += width
        hl = (h - hb.astype(F32)).astype(BF16)
        wsh = wsh_ref[...]
        out_refs[-1][rows, :] = _dot(hb, wsh) + _dot(hl, wsh) + _dot(hb, wsl_ref[...])


def _inproj_call(x, mod, g, w_big, ws_hi, ws_lo, seq):
    t, d = x.shape
    tm = _row_tile(seq, 1024)
    per_seq = seq // tm
    n_big = w_big.shape[1]
    row = lambda i: (i, 0)
    const = lambda i: (0, 0)
    resident = dict(pipeline_mode=pl.Buffered(1))
    out_shapes = [jax.ShapeDtypeStruct((t, d), BF16)]
    out_specs = [pl.BlockSpec((tm, d), row)]
    for width in IN_WIDTHS:
        out_shapes.append(jax.ShapeDtypeStruct((t, width), BF16))
        out_specs.append(pl.BlockSpec((tm, width), row))
    out_shapes.append(jax.ShapeDtypeStruct((t, LANES), F32))
    out_specs.append(pl.BlockSpec((tm, LANES), row))
    return pl.pallas_call(
        _inproj_kernel,
        grid=(t // tm,),
        in_specs=[
            pl.BlockSpec((tm, d), row),
            pl.BlockSpec((1, 6, d), lambda i: (i // per_seq, 0, 0)),
            pl.BlockSpec((1, d), const),
            pl.BlockSpec((d, n_big), const, **resident),
            pl.BlockSpec((d, LANES), const, **resident),
            pl.BlockSpec((d, LANES), const, **resident),
        ],
        out_specs=out_specs,
        out_shape=out_shapes,
        compiler_params=_params(1),
        name="in_proj",
    )(x, mod, g, w_big, ws_hi, ws_lo)


def _conv_silu(x_ref, cw_ref, cbuf):
    rows = x_ref.shape[0]
    x = x_ref[...].astype(F32)
    cbuf[SUBLANES:SUBLANES + rows, :] = x
    acc = x * cw_ref[CONV_K - 1:CONV_K, :]
    for s in range(1, CONV_K):
        acc = acc + cbuf[SUBLANES - s:SUBLANES - s + rows, :] * cw_ref[CONV_K - 1 - s:CONV_K - s, :]
    cbuf[0:SUBLANES, :] = cbuf[rows:rows + SUBLANES, :]
    return _silu(acc)


def _chunk_masks(rows):
    r = lax.broadcasted_iota(jnp.int32, (rows, rows), 0)
    c = lax.broadcasted_iota(jnp.int32, (rows, rows), 1)
    same = jnp.right_shift(r, 6) == jnp.right_shift(c, 6)
    causal = jnp.logical_and(same, r >= c)
    strict = jnp.logical_and(same, r > c)
    return causal, strict


def _chunk_end_rows(x, rows):
    parts = []
    for c in range(rows // CHUNK):
        last = x[c * CHUNK + CHUNK - 1:c * CHUNK + CHUNK, :]
        parts.append(jnp.broadcast_to(last, (CHUNK, x.shape[1])))
    return jnp.concatenate(parts, axis=0)


def _head_rms(o, w_row):
    ms = jnp.mean(o * o, axis=-1, keepdims=True)
    return o * lax.rsqrt(ms + EPS) * w_row


def _solve_unit_lower(a, rhs):
    x = rhs - _bdot(a, rhs)
    p = _bdot(a, a)
    for it in range(5):
        x = x + _bdot(p, x)
        if it < 4:
            p = _bdot(p, p)
    return x


def _mixer_a_kernel(qkv_ref, z_ref, small_ref, cw_ref, pa_ref, nw_ref, y_ref, cbuf, s_scr):
    rows = qkv_ref.shape[0]
    n_chunks = rows // CHUNK
    hw = N_HEADS * DK_A

    @pl.when(pl.program_id(1) == 0)
    def _():
        cbuf[0:SUBLANES, :] = jnp.zeros((SUBLANES, cbuf.shape[1]), F32)
        s_scr[...] = jnp.zeros(s_scr.shape, F32)

    qkv = _conv_silu(qkv_ref, cw_ref, cbuf)
    causal, strict = _chunk_masks(rows)

    gates = small_ref[...]
    log_decay = pa_ref[0:1, :] * _softplus(gates + pa_ref[1:2, :])
    beta_all = _sigmoid(gates)
    gc = _dot_exact_lhs(causal.astype(BF16), log_decay)
    gc_t = gc.T
    gc_end = _chunk_end_rows(gc, rows)

    heads = range(N_HEADS)
    a_mats, rhs, attns, qgs, k_ends, s_decays = [], [], [], [], [], []
    for h in heads:
        q = qkv[:, h * DK_A:(h + 1) * DK_A]
        k = qkv[:, hw + h * DK_A:hw + (h + 1) * DK_A]
        v = qkv[:, 2 * hw + h * DV:2 * hw + (h + 1) * DV]
        q = q * (lax.rsqrt(jnp.sum(q * q, axis=-1, keepdims=True) + 1e-6) * (DK_A ** -0.5))
        k = k * lax.rsqrt(jnp.sum(k * k, axis=-1, keepdims=True) + 1e-6)
        beta = beta_all[:, LANE_A_BETA + h:LANE_A_BETA + h + 1]
        g_col = gc[:, LANE_A_DECAY + h:LANE_A_DECAY + h + 1]
        g_row = gc_t[LANE_A_DECAY + h:LANE_A_DECAY + h + 1, :]
        ge_col = gc_end[:, LANE_A_DECAY + h:LANE_A_DECAY + h + 1]
        decay = jnp.exp(jnp.where(causal, g_col - g_row, 0.0))
        kb = k * beta
        k16 = k.astype(BF16)
        a_mats.append(jnp.where(strict, _dot(kb.astype(BF16), k16, NT) * decay, 0.0).astype(BF16))
        eg = jnp.exp(g_col)
        rhs.append(jnp.concatenate([v * beta, kb * eg], axis=1))
        attns.append(jnp.where(causal, _dot(q.astype(BF16), k16, NT) * decay, 0.0).astype(BF16))
        qgs.append((q * eg).astype(BF16))
        k_ends.append((k * jnp.exp(ge_col - g_col)).astype(BF16))
        s_decays.append(jnp.exp(ge_col))

    xs = [rhs[h] - _dot(a_mats[h], rhs[h].astype(BF16)) for h in heads]
    ps = [_dot(a_mats[h], a_mats[h]).astype(BF16) for h in heads]
    for it in range(5):
        xs = [xs[h] + _dot(ps[h], xs[h].astype(BF16)) for h in heads]
        if it < 4:
            ps = [_dot(ps[h], ps[h]).astype(BF16) for h in heads]
    us = [xs[h][:, :DV] for h in heads]
    ws = [xs[h][:, DV:].astype(BF16) for h in heads]

    states = [s_scr[h] for h in heads]
    deltas = [[] for _ in heads]
    inters = [[] for _ in heads]
    for c in range(n_chunks):
        lo, hi = c * CHUNK, (c + 1) * CHUNK
        for h in heads:
            s16 = states[h].astype(BF16)
            delta = us[h][lo:hi] - _dot(ws[h][lo:hi], s16)
            inters[h].append(_dot(qgs[h][lo:hi], s16))
            sd = jnp.concatenate([s_decays[h][lo:hi]] * (DK_A // CHUNK), axis=0)
            states[h] = sd * states[h] + _dot(k_ends[h][lo:hi], delta.astype(BF16), TN)
            deltas[h].append(delta.astype(BF16))
    for h in heads:
        s_scr[h] = states[h]
        o = jnp.concatenate(inters[h], axis=0) + _dot(attns[h], jnp.concatenate(deltas[h], axis=0))
        z = z_ref[:, h * DV:(h + 1) * DV].astype(F32)
        y_ref[:, h * DV:(h + 1) * DV] = (_head_rms(o, nw_ref[...]) * _silu(z)).astype(y_ref.dtype)


def _mixer_a_call(qkv, z, small, conv_w, pa, norm_w, bsz, seq):
    per_seq = seq // MIX_ROWS
    cw = qkv.shape[1]
    row = lambda b, j: (b * per_seq + j, 0)
    const = lambda b, j: (0, 0)
    return pl.pallas_call(
        _mixer_a_kernel,
        grid=(bsz, per_seq),
        in_specs=[
            pl.BlockSpec((MIX_ROWS, cw), row),
            pl.BlockSpec((MIX_ROWS, N_HEADS * DV), row),
            pl.BlockSpec((MIX_ROWS, LANES), row),
            pl.BlockSpec((CONV_K, cw), const),
            pl.BlockSpec((SUBLANES, LANES), const),
            pl.BlockSpec((1, DV), const),
        ],
        out_specs=pl.BlockSpec((MIX_ROWS, N_HEADS * DV), row),
        out_shape=jax.ShapeDtypeStruct((bsz * seq, N_HEADS * DV), BF16),
        scratch_shapes=[
            pltpu.VMEM((MIX_ROWS + 2 * SUBLANES, cw), F32),
            pltpu.VMEM((N_HEADS, DK_A, DV), F32),
        ],
        compiler_params=_params(2),
        name="mixer_deltanet",
    )(qkv, z, small, conv_w, pa, norm_w)


def _mixer_b_kernel(qk_ref, v_ref, og_ref, small_ref, cw_ref, pb_ref, nw_ref, y_ref, cbuf, e_scr, m_scr):
    rows = qk_ref.shape[0]
    n_chunks = rows // CHUNK
    hw = N_HEADS * DK_B

    @pl.when(pl.program_id(1) == 0)
    def _():
        cbuf[0:SUBLANES, :] = jnp.zeros((SUBLANES, cbuf.shape[1]), F32)
        e_scr[...] = jnp.zeros(e_scr.shape, F32)
        m_scr[...] = jnp.zeros(m_scr.shape, F32)

    qk = _conv_silu(qk_ref, cw_ref, cbuf)
    causal, _ = _chunk_masks(rows)

    pre = small_ref[...] + pb_ref[0:1, :]
    log_f = _log_sigmoid(pre)
    bcum = _dot_exact_lhs(causal.astype(BF16), log_f)
    bcum_t = bcum.T
    pre_t = pre.T
    bend = _chunk_end_rows(bcum, rows)
    ones = jnp.ones((rows, DV), BF16)

    heads = range(N_HEADS)
    qs, v_exts, b_cols, be_cols, m_intras, m_chunks, wts, k_ws = [], [], [], [], [], [], [], []
    for h in heads:
        q = qk[:, h * DK_B:(h + 1) * DK_B].astype(BF16)
        k = qk[:, hw + h * DK_B:hw + (h + 1) * DK_B] * (DK_B ** -0.5)
        lf = LANE_B_FORGET + h
        li = LANE_B_IN + h
        b_col = bcum[:, lf:lf + 1]
        b_row = bcum_t[lf:lf + 1, :]
        i_row = pre_t[li:li + 1, :]
        i_col = pre[:, li:li + 1]
        be_col = bend[:, lf:lf + 1]
        dlog = jnp.where(causal, b_col - b_row + i_row, -jnp.inf)
        m_intra = jnp.max(dlog, axis=-1, keepdims=True)
        m_chunk = _chunk_end_rows(m_intra, rows)
        wts.append((jnp.exp(dlog - m_intra) * _dot(q, k.astype(BF16), NT)).astype(BF16))
        k_ws.append((k * jnp.exp(be_col - b_col + i_col - m_chunk)).astype(BF16))
        qs.append(q)
        v_exts.append(jnp.concatenate([v_ref[:, h * DV:(h + 1) * DV], ones], axis=1))
        b_cols.append(b_col)
        be_cols.append(be_col)
        m_intras.append(m_intra)
        m_chunks.append(m_chunk)

    intras = [_dot(wts[h], v_exts[h]) for h in heads]
    kvs = [[_dot(k_ws[h][c * CHUNK:(c + 1) * CHUNK], v_exts[h][c * CHUNK:(c + 1) * CHUNK], TN)
            for c in range(n_chunks)] for h in heads]

    es = [e_scr[h] for h in heads]
    m_ss = [m_scr[:, h:h + 1] for h in heads]
    outs = [[] for _ in heads]
    for c in range(n_chunks):
        lo, hi = c * CHUNK, (c + 1) * CHUNK
        for h in heads:
            m_s = m_ss[h]
            m_tot = jnp.maximum(b_cols[h][lo:hi] + m_s, m_intras[h][lo:hi])
            s_inter = jnp.exp(b_cols[h][lo:hi] + m_s - m_tot)
            s_intra = jnp.exp(m_intras[h][lo:hi] - m_tot)
            both = s_inter * _dot(qs[h][lo:hi], es[h].astype(BF16)) + s_intra * intras[h][lo:hi]
            den = jnp.maximum(jnp.abs(both[:, DV:]), jnp.exp(-m_tot))
            outs[h].append(both[:, :DV] / den)
            m_new = jnp.maximum(be_cols[h][lo:hi] + m_s, m_chunks[h][lo:hi])
            s_prev = jnp.exp(be_cols[h][lo:hi] + m_s - m_new)
            s_cur = jnp.exp(m_chunks[h][lo:hi] - m_new)
            es[h] = s_prev * es[h] + s_cur * kvs[h][c]
            m_ss[h] = m_new
    for h in heads:
        e_scr[h] = es[h]
        m_scr[:, h:h + 1] = m_ss[h]
        o = jnp.concatenate(outs[h], axis=0)
        og = og_ref[:, h * DV:(h + 1) * DV].astype(F32)
        y_ref[:, h * DV:(h + 1) * DV] = (_sigmoid(og) * _head_rms(o, nw_ref[...])).astype(y_ref.dtype)


def _mixer_b_call(qk, v, og, small, conv_w, pb, norm_w, bsz, seq):
    per_seq = seq // MIX_ROWS
    cw = qk.shape[1]
    row = lambda b, j: (b * per_seq + j, 0)
    const = lambda b, j: (0, 0)
    return pl.pallas_call(
        _mixer_b_kernel,
        grid=(bsz, per_seq),
        in_specs=[
            pl.BlockSpec((MIX_ROWS, cw), row),
            pl.BlockSpec((MIX_ROWS, N_HEADS * DV), row),
            pl.BlockSpec((MIX_ROWS, N_HEADS * DV), row),
            pl.BlockSpec((MIX_ROWS, LANES), row),
            pl.BlockSpec((CONV_K, cw), const),
            pl.BlockSpec((SUBLANES, LANES), const),
            pl.BlockSpec((1, DV), const),
        ],
        out_specs=pl.BlockSpec((MIX_ROWS, N_HEADS * DV), row),
        out_shape=jax.ShapeDtypeStruct((bsz * seq, N_HEADS * DV), BF16),
        scratch_shapes=[
            pltpu.VMEM((MIX_ROWS + 2 * SUBLANES, cw), F32),
            pltpu.VMEM((N_HEADS, DK_B, 2 * DV), F32),
            pltpu.VMEM((CHUNK, LANES), F32),
        ],
        compiler_params=_params(2),
        name="mixer_mlstm",
    )(qk, v, og, small, conv_w, pb, norm_w)


def _mixer_c_kernel(qk_ref, v_ref, z_ref, small_ref, wg_ref, bg_ref, nw_ref, y_ref, st_scr):
    rows = qk_ref.shape[0]
    n_chunks = rows // CHUNK
    hw = N_HEADS * DK_C

    @pl.when(pl.program_id(1) == 0)
    def _():
        st_scr[...] = jnp.zeros(st_scr.shape, F32)

    causal, _ = _chunk_masks(rows)
    log_a = _log_sigmoid(_dot_hi(small_ref[...], wg_ref[...]) + bg_ref[...]) * (1.0 / GLA_TAU)
    bc = _dot_exact_lhs(causal.astype(BF16), log_a)
    parts = []
    for c in range(n_chunks):
        mid = bc[c * CHUNK + CHUNK // 2:c * CHUNK + CHUNK // 2 + 1, :]
        parts.append(jnp.broadcast_to(mid, (CHUNK, hw)))
    rel = bc - jnp.concatenate(parts, axis=0)
    bc_end = _chunk_end_rows(bc, rows)
    qk = qk_ref[...].astype(F32)
    q_all = qk[:, :hw] * (DK_C ** -0.5)
    k_all = qk[:, hw:]
    q_rel = (q_all * jnp.exp(rel)).astype(BF16)
    k_rel = (k_all * jnp.exp(-rel)).astype(BF16)
    k_end = (k_all * jnp.exp(bc_end - bc)).astype(BF16)
    q_abs = (q_all * jnp.exp(bc)).astype(BF16)
    a_end = jnp.exp(bc_end)

    heads = range(N_HEADS)
    sls = [slice(h * DK_C, (h + 1) * DK_C) for h in heads]
    vs = [v_ref[:, h * DV:(h + 1) * DV] for h in heads]
    attns = [jnp.where(causal, _dot(q_rel[:, sls[h]], k_rel[:, sls[h]], NT), 0.0).astype(BF16) for h in heads]
    o_intras = [_dot(attns[h], vs[h]) for h in heads]
    kvs = [[_dot(vs[h][c * CHUNK:(c + 1) * CHUNK], k_end[c * CHUNK:(c + 1) * CHUNK, sls[h]], TN)
            for c in range(n_chunks)] for h in heads]
    sts = [st_scr[h] for h in heads]
    inters = [[] for _ in heads]
    for c in range(n_chunks):
        lo, hi = c * CHUNK, (c + 1) * CHUNK
        for h in heads:
            inters[h].append(_dot(q_abs[lo:hi, sls[h]], sts[h].astype(BF16), NT))
            sts[h] = sts[h] * a_end[lo:lo + 1, sls[h]] + kvs[h][c]
    for h in heads:
        st_scr[h] = sts[h]
        o = jnp.concatenate(inters[h], axis=0) + o_intras[h]
        z = z_ref[:, h * DV:(h + 1) * DV].astype(F32)
        y_ref[:, h * DV:(h + 1) * DV] = (_head_rms(o, nw_ref[...]) * _silu(z)).astype(y_ref.dtype)


def _mixer_c_call(qk, v, z, small, wg, bg, norm_w, bsz, seq):
    per_seq = seq // MIX_ROWS
    cw = qk.shape[1]
    row = lambda b, j: (b * per_seq + j, 0)
    const = lambda b, j: (0, 0)
    return pl.pallas_call(
        _mixer_c_kernel,
        grid=(bsz, per_seq),
        in_specs=[
            pl.BlockSpec((MIX_ROWS, cw), row),
            pl.BlockSpec((MIX_ROWS, N_HEADS * DV), row),
            pl.BlockSpec((MIX_ROWS, N_HEADS * DV), row),
            pl.BlockSpec((MIX_ROWS, LANES), row),
            pl.BlockSpec((LANES, N_HEADS * DK_C), const),
            pl.BlockSpec((1, N_HEADS * DK_C), const),
            pl.BlockSpec((1, DV), const),
        ],
        out_specs=pl.BlockSpec((MIX_ROWS, N_HEADS * DV), row),
        out_shape=jax.ShapeDtypeStruct((bsz * seq, N_HEADS * DV), BF16),
        scratch_shapes=[pltpu.VMEM((N_HEADS, DV, DK_C), F32)],
        compiler_params=_params(2),
        name="mixer_gla",
    )(qk, v, z, small, wg, bg, norm_w)


def _merge_kernel(x_ref, h_ref, ya_ref, yb_ref, yc_ref, mod_ref, wmg_ref, bmg_ref, wbr_ref, wo_ref, o_ref):
    h = h_ref[...]
    merged = None
    for g, y_ref in enumerate((ya_ref, yb_ref, yc_ref)):
        gate = _sigmoid(_dot(h, wmg_ref[g]) + bmg_ref[g])
        term = gate * _dot(y_ref[...], wbr_ref[g])
        merged = term if merged is None else merged + term
    o_ref[...] = x_ref[...] + mod_ref[0, 2:3, :] * _bdot(merged, wo_ref[...])


def _merge_call(x, h, ya, yb, yc, mod, w_mg, b_mg, w_br, w_o, seq):
    t, d = x.shape
    tm = _row_tile(seq, 512)
    per_seq = seq // tm
    bw = ya.shape[1]
    row = lambda i: (i, 0)
    return pl.pallas_call(
        _merge_kernel,
        grid=(t // tm,),
        in_specs=[
            pl.BlockSpec((tm, d), row),
            pl.BlockSpec((tm, d), row),
            pl.BlockSpec((tm, bw), row),
            pl.BlockSpec((tm, bw), row),
            pl.BlockSpec((tm, bw), row),
            pl.BlockSpec((1, 6, d), lambda i: (i // per_seq, 0, 0)),
            pl.BlockSpec((3, d, d), lambda i: (0, 0, 0)),
            pl.BlockSpec((3, 1, d), lambda i: (0, 0, 0)),
            pl.BlockSpec((3, bw, d), lambda i: (0, 0, 0)),
            pl.BlockSpec((d, d), lambda i: (0, 0)),
        ],
        out_specs=pl.BlockSpec((tm, d), row),
        out_shape=jax.ShapeDtypeStruct((t, d), F32),
        compiler_params=_params(1),
        name="merge",
    )(x, h, ya, yb, yc, mod, w_mg, b_mg, w_br, w_o)


ROUTE_E1, ROUTE_E2, ROUTE_W1, ROUTE_W2, ROUTE_R1, ROUTE_R2 = range(6)


def _router_kernel(n_experts, x_ref, mod_ref, g_ref, wr_ref, br_ref, h_ref, route_ref, count_ref, carry):
    @pl.when(pl.program_id(0) == 0)
    def _():
        carry[...] = jnp.zeros(carry.shape, F32)

    h = _norm_mod(x_ref[...], g_ref[...], mod_ref[0, 3:4, :], mod_ref[0, 4:5, :])
    _store_token_tiles(h_ref, h)
    logits = _dot_hi(h, wr_ref[...]) + br_ref[...]
    tm = logits.shape[0]
    lane = lax.broadcasted_iota(jnp.int32, logits.shape, 1)
    logits = jnp.where(lane < n_experts, logits, -jnp.inf)
    m1 = jnp.max(logits, axis=-1, keepdims=True)
    i1 = jnp.min(jnp.where(logits == m1, lane, LANES), axis=-1, keepdims=True)
    rest = jnp.where(lane == i1, -jnp.inf, logits)
    m2 = jnp.max(rest, axis=-1, keepdims=True)
    i2 = jnp.min(jnp.where(rest == m2, lane, LANES), axis=-1, keepdims=True)
    e2 = jnp.exp(m2 - m1)
    w1 = 1.0 / (1.0 + e2)
    w2 = e2 * w1

    sel1 = lane == i1
    sel2 = lane == i2
    onehot = jnp.where(jnp.logical_or(sel1, sel2), 1.0, 0.0)
    r = lax.broadcasted_iota(jnp.int32, (tm, tm), 0)
    c = lax.broadcasted_iota(jnp.int32, (tm, tm), 1)
    before = jnp.where(r > c, 1.0, 0.0).astype(BF16)
    rank = _dot(before, onehot.astype(BF16)) + carry[0:1, :]
    carry[...] = carry[...] + jnp.sum(onehot, axis=0, keepdims=True)
    count_ref[...] = carry[...]
    r1 = jnp.sum(jnp.where(sel1, rank, 0.0), axis=-1, keepdims=True)
    r2 = jnp.sum(jnp.where(sel2, rank, 0.0), axis=-1, keepdims=True)
    rec = jnp.zeros(logits.shape, F32)
    for pos, val in ((ROUTE_E1, i1.astype(F32)), (ROUTE_E2, i2.astype(F32)), (ROUTE_W1, w1), (ROUTE_W2, w2),
                     (ROUTE_R1, r1), (ROUTE_R2, r2)):
        rec = jnp.where(lane == pos, val, rec)
    route_ref[...] = rec


def _router_call(x, mod, g, wr, br, n_experts, seq):
    t, d = x.shape
    tm = _row_tile(seq, 512)
    per_seq = seq // tm
    row = lambda i: (i, 0)
    const = lambda i: (0, 0)
    return pl.pallas_call(
        functools.partial(_router_kernel, n_experts),
        grid=(t // tm,),
        in_specs=[
            pl.BlockSpec((tm, d), row),
            pl.BlockSpec((1, 6, d), lambda i: (i // per_seq, 0, 0)),
            pl.BlockSpec((1, d), const),
            pl.BlockSpec((d, LANES), const),
            pl.BlockSpec((1, LANES), const),
        ],
        out_specs=[pl.BlockSpec((tm * SUBLANES, LANES), row), pl.BlockSpec((tm, LANES), row),
                   pl.BlockSpec((SUBLANES, LANES), const)],
        out_shape=[jax.ShapeDtypeStruct((t * SUBLANES, LANES), F32), jax.ShapeDtypeStruct((t, LANES), F32),
                   jax.ShapeDtypeStruct((SUBLANES, LANES), F32)],
        scratch_shapes=[pltpu.VMEM((SUBLANES, LANES), F32)],
        compiler_params=_params(1),
        name="prenorm_router",
    )(x, mod, g, wr, br)


MOE_ROWS = 1024
COMBINE_ROWS = 256


TOKEN_TILE = SUBLANES * LANES


def _store_token_tiles(ref, x):
    n = x.shape[0]
    for j in range(SUBLANES):
        ref[pl.ds(j, n, stride=SUBLANES), :] = x[:, j * LANES:(j + 1) * LANES]


def _load_token_tiles(ref):
    n = ref.shape[0] // SUBLANES
    return jnp.concatenate([ref[pl.ds(j, n, stride=SUBLANES), :] for j in range(SUBLANES)], axis=1)


def _tile_gather_start(row_ref, src_hbm, dst_ref, sem, queues=(0, 1)):
    def issue(i, carry):
        for k in range(SUBLANES):
            r = i * SUBLANES + k
            src_row = pl.multiple_of(row_ref[r], SUBLANES)
            dst_row = pl.multiple_of(r * SUBLANES, SUBLANES)
            pltpu.make_async_copy(src_hbm.at[pl.ds(src_row, SUBLANES)], dst_ref.at[pl.ds(dst_row, SUBLANES)],
                                  sem).start(priority=queues[k % len(queues)])
        return carry

    lax.fori_loop(0, dst_ref.shape[0] // (SUBLANES * SUBLANES), issue, 0)


def _tile_gather_wait(src_hbm, dst_ref, sem):
    pltpu.make_async_copy(src_hbm.at[pl.ds(0, dst_ref.shape[0])], dst_ref, sem).wait()


def _moe_ffn_kernel(te_ref, nu_ref, rows_ref, rows_next_ref, h_hbm, w1_ref, w3_ref, w2_ref, ys_ref,
                    xbuf, sem, hb_scr, acc_ref):
    i = pl.program_id(0)
    f = pl.program_id(1)
    last = pl.num_programs(1) - 1
    n_used = nu_ref[0]
    used = i < n_used
    slot = lax.rem(i, 2)

    @pl.when(jnp.logical_and(f == 0, jnp.logical_and(i == 0, used)))
    def _():
        _tile_gather_start(rows_ref, h_hbm, xbuf.at[0], sem.at[0])

    @pl.when(jnp.logical_and(f == 0, i + 1 < n_used))
    def _():
        _tile_gather_start(rows_next_ref, h_hbm, xbuf.at[1 - slot], sem.at[1 - slot], queues=(1,))

    @pl.when(jnp.logical_and(used, f == 0))
    def _():
        _tile_gather_wait(h_hbm, xbuf.at[slot], sem.at[slot])
        hb_scr[...] = _load_token_tiles(xbuf.at[slot]).astype(BF16)
        acc_ref[...] = jnp.zeros(acc_ref.shape, F32)

    @pl.when(used)
    def _():
        h = hb_scr[...]
        g = _silu(_dot(h, w1_ref[0])) * _dot(h, w3_ref[0])
        acc_ref[...] += _bdot(g, w2_ref[0])

    @pl.when(jnp.logical_and(used, f == last))
    def _():
        _store_token_tiles(ys_ref, acc_ref[...])

    @pl.when(jnp.logical_and(jnp.logical_not(used), f == last))
    def _():
        ys_ref[...] = jnp.zeros(ys_ref.shape, F32)


def _moe_ffn_call(tile_expert, n_used, tok, h_tiles, w1, w3, w2):
    p = tok.shape[0]
    n_tiles = p // MOE_ROWS
    d = TOKEN_TILE
    d_ff = w1.shape[2]
    tf = _ffn_tile(d_ff, FFN_MOE_TILE_CAP)
    rows = tok * SUBLANES
    grid_spec = pltpu.PrefetchScalarGridSpec(
        num_scalar_prefetch=2,
        grid=(n_tiles, d_ff // tf),
        in_specs=[
            pl.BlockSpec((MOE_ROWS,), lambda i, f, te, nu: (i,), memory_space=pltpu.SMEM),
            pl.BlockSpec((MOE_ROWS,), lambda i, f, te, nu: (jnp.minimum(i + 1, n_tiles - 1),),
                         memory_space=pltpu.SMEM),
            pl.BlockSpec(memory_space=pl.ANY),
            pl.BlockSpec((1, d, tf), lambda i, f, te, nu: (te[i], 0, f)),
            pl.BlockSpec((1, d, tf), lambda i, f, te, nu: (te[i], 0, f)),
            pl.BlockSpec((1, tf, d), lambda i, f, te, nu: (te[i], f, 0)),
        ],
        out_specs=pl.BlockSpec((MOE_ROWS * SUBLANES, LANES), lambda i, f, te, nu: (i, 0)),
        scratch_shapes=[
            pltpu.VMEM((2, MOE_ROWS * SUBLANES, LANES), F32),
            pltpu.SemaphoreType.DMA((2,)),
            pltpu.VMEM((MOE_ROWS, d), BF16),
            pltpu.VMEM((MOE_ROWS, d), F32),
        ],
    )
    return pl.pallas_call(
        _moe_ffn_kernel,
        grid_spec=grid_spec,
        out_shape=jax.ShapeDtypeStruct((p * SUBLANES, LANES), F32),
        compiler_params=_params(2),
        name="moe_ffn",
    )(tile_expert, n_used, rows, rows, h_tiles, w1, w3, w2)


def _moe_combine_kernel(final, r1_ref, r2_ref, ys_hbm, route_ref, x_ref, mod_ref, gf_ref, o_ref, buf, sem):
    _tile_gather_start(r1_ref, ys_hbm, buf.at[0], sem.at[0])
    _tile_gather_start(r2_ref, ys_hbm, buf.at[1], sem.at[1])
    _tile_gather_wait(ys_hbm, buf.at[0], sem.at[0])
    _tile_gather_wait(ys_hbm, buf.at[1], sem.at[1])
    route = route_ref[...]
    f = (route[:, ROUTE_W1:ROUTE_W1 + 1] * _load_token_tiles(buf.at[0])
         + route[:, ROUTE_W2:ROUTE_W2 + 1] * _load_token_tiles(buf.at[1]))
    y = x_ref[...] + mod_ref[0, 5:6, :] * f
    if final:
        ms = jnp.mean(y * y, axis=-1, keepdims=True)
        y = y * lax.rsqrt(ms + EPS) * gf_ref[...]
    o_ref[...] = y


def _moe_combine_call(d1, d2, ys, route, x, mod, g_final, seq, final):
    t, d = x.shape
    tm = _row_tile(seq, COMBINE_ROWS)
    per_seq = seq // tm
    row = lambda i: (i, 0)
    idx_spec = pl.BlockSpec((tm,), lambda i: (i,), memory_space=pltpu.SMEM)
    return pl.pallas_call(
        functools.partial(_moe_combine_kernel, final),
        grid=(t // tm,),
        in_specs=[
            idx_spec, idx_spec,
            pl.BlockSpec(memory_space=pl.ANY),
            pl.BlockSpec((tm, LANES), row),
            pl.BlockSpec((tm, d), row),
            pl.BlockSpec((1, 6, d), lambda i: (i // per_seq, 0, 0)),
            pl.BlockSpec((1, d), lambda i: (0, 0)),
        ],
        out_specs=pl.BlockSpec((tm, d), row),
        out_shape=jax.ShapeDtypeStruct((t, d), F32),
        scratch_shapes=[pltpu.VMEM((2, tm * SUBLANES, LANES), F32), pltpu.SemaphoreType.DMA((2,))],
        compiler_params=_params(1),
        name="moe_combine",
    )(d1 * SUBLANES, d2 * SUBLANES, ys, route, x, mod, g_final)


def _moe_layout(route, counts, n_experts, t):
    cnt = counts[0, :n_experts].astype(jnp.int32)
    padded = ((cnt + MOE_ROWS - 1) // MOE_ROWS) * MOE_ROWS
    ends = jnp.cumsum(padded)
    offs = ends - padded
    e1 = route[:, ROUTE_E1].astype(jnp.int32)
    e2 = route[:, ROUTE_E2].astype(jnp.int32)
    d1 = offs[e1] + route[:, ROUTE_R1].astype(jnp.int32)
    d2 = offs[e2] + route[:, ROUTE_R2].astype(jnp.int32)
    p = (TOP_K * t // MOE_ROWS + n_experts) * MOE_ROWS
    ids = jnp.arange(t, dtype=jnp.int32)
    tok = jnp.zeros((p,), jnp.int32).at[jnp.concatenate([d1, d2])].set(jnp.concatenate([ids, ids]))
    starts = jnp.arange(p // MOE_ROWS, dtype=jnp.int32) * MOE_ROWS
    tile_expert = jnp.minimum(jnp.searchsorted(ends, starts, side="right"), n_experts - 1).astype(jnp.int32)
    n_used = (ends[-1:] // MOE_ROWS).astype(jnp.int32)
    return d1, d2, tok, tile_expert, n_used


def _ffn_kernel(x_ref, mod_ref, g_ref, w1_ref, w3_ref, w2_ref, o_ref):
    x = x_ref[...]
    h = _norm_mod(x, g_ref[...], mod_ref[0, 3:4, :], mod_ref[0, 4:5, :]).astype(BF16)
    d_ff = w1_ref.shape[1]
    tf = _ffn_tile(d_ff, FFN_DENSE_TILE_CAP)
    acc = None
    for f0 in range(0, d_ff, tf):
        g = _silu(_dot(h, w1_ref[:, f0:f0 + tf])) * _dot(h, w3_ref[:, f0:f0 + tf])
        part = _bdot(g, w2_ref[f0:f0 + tf, :])
        acc = part if acc is None else acc + part
    o_ref[...] = x + mod_ref[0, 5:6, :] * acc


FFN_DENSE_TILE_CAP = 1536
FFN_MOE_TILE_CAP = 512


def _ffn_tile(d_ff, cap):
    best = None
    for tf in range(LANES, min(cap, d_ff) + 1, LANES):
        if d_ff % tf == 0:
            best = tf
    if best is None:
        raise ValueError(d_ff)
    return best


def _ffn_call(x, mod, g, w1, w3, w2, seq):
    t, d = x.shape
    d_ff = w1.shape[1]
    tm = _row_tile(seq, 512)
    per_seq = seq // tm
    row = lambda i: (i, 0)
    const = lambda i: (0, 0)
    resident = dict(pipeline_mode=pl.Buffered(1))
    return pl.pallas_call(
        _ffn_kernel,
        grid=(t // tm,),
        in_specs=[
            pl.BlockSpec((tm, d), row),
            pl.BlockSpec((1, 6, d), lambda i: (i // per_seq, 0, 0)),
            pl.BlockSpec((1, d), const),
            pl.BlockSpec((d, d_ff), const, **resident),
            pl.BlockSpec((d, d_ff), const, **resident),
            pl.BlockSpec((d_ff, d), const, **resident),
        ],
        out_specs=pl.BlockSpec((tm, d), row),
        out_shape=jax.ShapeDtypeStruct((t, d), F32),
        compiler_params=_params(1),
        name="ffn_dense",
    )(x, mod, g, w1, w3, w2)


def _final_kernel(x_ref, g_ref, o_ref):
    x = x_ref[...]
    ms = jnp.mean(x * x, axis=-1, keepdims=True)
    o_ref[...] = x * lax.rsqrt(ms + EPS) * g_ref[...]


def _final_call(x, g):
    t, d = x.shape
    tm = _row_tile(t, 512)
    return pl.pallas_call(
        _final_kernel,
        grid=(t // tm,),
        in_specs=[pl.BlockSpec((tm, d), lambda i: (i, 0)), pl.BlockSpec((1, d), lambda i: (0, 0))],
        out_specs=pl.BlockSpec((tm, d), lambda i: (i, 0)),
        out_shape=jax.ShapeDtypeStruct((t, d), F32),
        compiler_params=_params(1),
        name="final_norm",
    )(x, g)


def _lane_row(values, lane0, n_rows=1, row=0):
    out = jnp.zeros((n_rows, LANES), F32)
    return out.at[row, lane0:lane0 + values.shape[0]].set(values.astype(F32))


def _split_w_in(w_in):
    qk_a, v_a = N_HEADS * DK_A, N_HEADS * DV
    qk_b, qk_c, vw = N_HEADS * DK_B, N_HEADS * DK_C, N_HEADS * DV
    sizes = (qk_a, qk_a, v_a, v_a, N_HEADS, N_HEADS,
             qk_b, qk_b, vw, vw, N_HEADS, N_HEADS,
             qk_c, qk_c, vw, vw, GLA_RANK)
    names = ("q_a", "k_a", "v_a", "z_a", "a_a", "b_a", "q_b", "k_b", "v_b", "o_b", "i_b", "f_b",
             "q_c", "k_c", "v_c", "z_c", "g_c")
    assert sum(sizes) == w_in.shape[1]
    cols, off = {}, 0
    for name, size in zip(names, sizes):
        cols[name] = w_in[:, off:off + size]
        off += size
    big = jnp.concatenate([cols[n] for n in ("q_a", "k_a", "v_a", "z_a", "q_b", "k_b", "v_b", "o_b",
                                             "q_c", "k_c", "v_c", "z_c")], axis=1).astype(BF16)
    small = jnp.concatenate([cols[n] for n in ("a_a", "b_a", "i_b", "f_b", "g_c")], axis=1)
    small = jnp.pad(small, ((0, 0), (0, LANES - small.shape[1])))
    s_hi = small.astype(BF16)
    s_lo = (small - s_hi.astype(F32)).astype(BF16)
    return big, s_hi, s_lo


def kernel(x, c, w_ada, b_ada, g_mix, g_ffn, g_final, w_in, conv_a, a_log, dt_bias, norm_a, conv_b, b_i, b_f,
           norm_b, w_gla2, b_gla, norm_c, w_br, w_mg, b_mg, w_o, w1_d, w3_d, w2_d, w_router, b_router,
           w1_e, w3_e, w2_e):
    bsz, seq, d = x.shape
    depth = w_ada.shape[0]
    assert seq % MIX_ROWS == 0 and d % LANES == 0
    t = bsz * seq
    xf = x.reshape(t, d)
    mod_all = _ada_call(c, w_ada, b_ada).reshape(depth, bsz, 6, d)

    for layer in range(depth):
        mod = mod_all[layer]
        w_big, ws_hi, ws_lo = _split_w_in(w_in[layer])
        (h, qkv_a, z_a, qk_b, v_b, o_b, qk_c, v_c, z_c, small) = _inproj_call(
            xf, mod, g_mix[layer].reshape(1, d), w_big, ws_hi, ws_lo, seq)

        pa = (_lane_row(-jnp.exp(a_log[layer]), LANE_A_DECAY, SUBLANES, 0)
              + _lane_row(dt_bias[layer], LANE_A_DECAY, SUBLANES, 1))
        y_a = _mixer_a_call(qkv_a, z_a, small, conv_a[layer], pa, norm_a[layer].reshape(1, DV), bsz, seq)

        pb = (_lane_row(b_i[layer], LANE_B_IN, SUBLANES, 0) + _lane_row(b_f[layer], LANE_B_FORGET, SUBLANES, 0))
        y_b = _mixer_b_call(qk_b, v_b, o_b, small, conv_b[layer], pb, norm_b[layer].reshape(1, DV), bsz, seq)

        wg = jnp.zeros((LANES, N_HEADS * DK_C), F32).at[LANE_C_GATE:LANE_C_GATE + GLA_RANK].set(w_gla2[layer])
        y_c = _mixer_c_call(qk_c, v_c, z_c, small, wg, b_gla[layer].reshape(1, -1),
                            norm_c[layer].reshape(1, DV), bsz, seq)

        xf = _merge_call(xf, h, y_a, y_b, y_c, mod, w_mg[layer].astype(BF16), b_mg[layer][:, None, :],
                         w_br[layer].astype(BF16), w_o[layer].astype(BF16), seq)

        j = layer // 2
        last = layer == depth - 1
        if layer % 2 == 0:
            xf = _ffn_call(xf, mod, g_ffn[layer].reshape(1, d), w1_d[j].astype(BF16), w3_d[j].astype(BF16),
                           w2_d[j].astype(BF16), seq)
            if last:
                xf = _final_call(xf, g_final.reshape(1, d))
        else:
            n_experts = w_router.shape[2]
            wr = jnp.pad(w_router[j], ((0, 0), (0, LANES - n_experts)))
            br = jnp.pad(b_router[j], (0, LANES - n_experts)).reshape(1, LANES)
            assert d == TOKEN_TILE, "gathered rows are stored as one (SUBLANES, LANES) tile per token"
            h2, route, counts = _router_call(xf, mod, g_ffn[layer].reshape(1, d), wr, br, n_experts, seq)
            d1, d2, tok, tile_expert, n_used = _moe_layout(route, counts, n_experts, t)
            ys = _moe_ffn_call(tile_expert, n_used, tok, h2, w1_e[j].astype(BF16), w3_e[j].astype(BF16),
                               w2_e[j].astype(BF16))
            xf = _moe_combine_call(d1, d2, ys, route, xf, mod, g_final.reshape(1, d), seq, last)

    return xf.reshape(bsz, seq, d)
```

```python
import functools

import jax
import jax.numpy as jnp
from jax import lax
from jax.experimental import pallas as pl
from jax.experimental.pallas import tpu as pltpu

F32 = jnp.float32
BF16 = jnp.bfloat16

EPS = 1e-6
CHUNK = 64
CONV_K = 4
N_HEADS = 4
DV = 128
DK_A = 128
DK_B = 64
DK_C = 64
GLA_RANK = 16
GLA_TAU = 16.0
TOP_K = 2

V7X_MXU_DIM = 256
MIX_ROWS = V7X_MXU_DIM
LANES = 128
SUBLANES = 8
V7X_VMEM_LIMIT = 56 * 1024 * 1024

LANE_A_DECAY = 0
LANE_A_BETA = 4
LANE_B_IN = 8
LANE_B_FORGET = 12
LANE_C_GATE = 16

NN = (((1,), (0,)), ((), ()))
NT = (((1,), (1,)), ((), ()))
TN = (((0,), (0,)), ((), ()))


def _dot(a, b, dims=NN):
    return lax.dot_general(a, b, dims, preferred_element_type=F32)


def _bdot(a, b, dims=NN):
    return _dot(a.astype(BF16), b.astype(BF16), dims)


def _split3(x):
    hi = x.astype(BF16)
    r1 = x - hi.astype(F32)
    mid = r1.astype(BF16)
    lo = (r1 - mid.astype(F32)).astype(BF16)
    return hi, mid, lo


def _dot_exact_lhs(a, b, dims=NN):
    hi, mid, lo = _split3(b)
    return _dot(a, hi, dims) + _dot(a, mid, dims) + _dot(a, lo, dims)


def _dot_hi(a, b, dims=NN):
    a_hi = a.astype(BF16)
    a_lo = (a - a_hi.astype(F32)).astype(BF16)
    b_hi = b.astype(BF16)
    b_lo = (b - b_hi.astype(F32)).astype(BF16)
    return _dot(a_hi, b_hi, dims) + _dot(a_hi, b_lo, dims) + _dot(a_lo, b_hi, dims)


def _sigmoid(x):
    return 1.0 / (1.0 + jnp.exp(-x))


def _silu(x):
    return x * _sigmoid(x)


def _softplus(x):
    return jnp.maximum(x, 0.0) + jnp.log(1.0 + jnp.exp(-jnp.abs(x)))


def _log_sigmoid(x):
    return jnp.minimum(x, 0.0) - jnp.log(1.0 + jnp.exp(-jnp.abs(x)))


def _params(n_axes):
    return pltpu.CompilerParams(
        dimension_semantics=("arbitrary",) * n_axes,
        vmem_limit_bytes=V7X_VMEM_LIMIT,
    )


def _row_tile(t, want):
    tm = min(want, t)
    assert t % tm == 0, (t, tm)
    return tm


ROW_SPLIT = 2


def _row_halves(tm):
    half = tm // ROW_SPLIT
    return [slice(s * half, (s + 1) * half) for s in range(ROW_SPLIT)]


def _resident():
    return dict(pipeline_mode=pl.Buffered(1))


def _ada_kernel(c_ref, w_ref, b_ref, o_ref):
    c = c_ref[...]
    o_ref[0] = _bdot(_silu(c), w_ref[0]) + b_ref[0]


def _ada_call(c, w_ada, b_ada):
    depth, d, d6 = w_ada.shape
    bsz = c.shape[0]
    n_col = d6 // d
    return pl.pallas_call(
        _ada_kernel,
        grid=(depth, n_col),
        in_specs=[
            pl.BlockSpec((bsz, d), lambda l, n: (0, 0)),
            pl.BlockSpec((1, d, d), lambda l, n: (l, 0, n)),
            pl.BlockSpec((1, 1, d), lambda l, n: (l, 0, n)),
        ],
        out_specs=pl.BlockSpec((1, bsz, d), lambda l, n: (l, 0, n)),
        out_shape=jax.ShapeDtypeStruct((depth, bsz, d6), F32),
        compiler_params=_params(2),
        name="ada_mod",
    )(c, w_ada, b_ada.reshape(depth, 1, d6))


def _norm_mod(x, g, shift, scale):
    ms = jnp.mean(x * x, axis=-1, keepdims=True)
    return (x * lax.rsqrt(ms + EPS) * g) * (1.0 + scale) + shift


IN_WIDTHS = (1536, 512, 512, 512, 512, 512, 512, 512)


def _inproj_kernel(x_ref, mod_ref, g_ref, w_ref, wsh_ref, wsl_ref, h_ref, *out_refs):
    halves = _row_halves(x_ref.shape[0])
    hs = []
    for rows in halves:
        h = _norm_mod(x_ref[rows, :], g_ref[...], mod_ref[0, 0:1, :], mod_ref[0, 1:2, :])
        hb = h.astype(BF16)
        h_ref[rows, :] = hb.astype(h_ref.dtype)
        hs.append((h, hb))
    for rows, (h, hb) in zip(halves, hs):
        off = 0
        for o_ref, width in zip(out_refs[:-1], IN_WIDTHS):
            o_ref[rows, :] = _dot(hb, w_ref[:, off:off + width]).astype(o_ref.dtype)
            off += width
        hl = (h - hb.astype(F32)).astype(BF16)
        wsh = wsh_ref[...]
        out_refs[-1][rows, :] = _dot(hb, wsh) + _dot(hl, wsh) + _dot(hb, wsl_ref[...])


def _inproj_call(x, mod, g, w_big, ws_hi, ws_lo, seq):
    t, d = x.shape
    tm = _row_tile(seq, 1024)
    per_seq = seq // tm
    n_big = w_big.shape[1]
    row = lambda i: (i, 0)
    const = lambda i: (0, 0)
    out_shapes = [jax.ShapeDtypeStruct((t, d), BF16)]
    out_specs = [pl.BlockSpec((tm, d), row)]
    for width in IN_WIDTHS:
        out_shapes.append(jax.ShapeDtypeStruct((t, width), BF16))
        out_specs.append(pl.BlockSpec((tm, width), row))
    out_shapes.append(jax.ShapeDtypeStruct((t, LANES), F32))
    out_specs.append(pl.BlockSpec((tm, LANES), row))
    return pl.pallas_call(
        _inproj_kernel,
        grid=(t // tm,),
        in_specs=[
            pl.BlockSpec((tm, d), row),
            pl.BlockSpec((1, 6, d), lambda i: (i // per_seq, 0, 0)),
            pl.BlockSpec((1, d), const),
            pl.BlockSpec((d, n_big), const, **_resident()),
            pl.BlockSpec((d, LANES), const, **_resident()),
            pl.BlockSpec((d, LANES), const, **_resident()),
        ],
        out_specs=out_specs,
        out_shape=out_shapes,
        compiler_params=_params(1),
        name="in_proj",
    )(x, mod, g, w_big, ws_hi, ws_lo)


def _conv_silu(x_ref, cw_ref, cbuf):
    rows = x_ref.shape[0]
    x = x_ref[...].astype(F32)
    cbuf[SUBLANES:SUBLANES + rows, :] = x
    acc = x * cw_ref[CONV_K - 1:CONV_K, :]
    for s in range(1, CONV_K):
        acc = acc + cbuf[SUBLANES - s:SUBLANES - s + rows, :] * cw_ref[CONV_K - 1 - s:CONV_K - s, :]
    cbuf[0:SUBLANES, :] = cbuf[rows:rows + SUBLANES, :]
    return _silu(acc)


def _chunk_masks(rows):
    r = lax.broadcasted_iota(jnp.int32, (rows, rows), 0)
    c = lax.broadcasted_iota(jnp.int32, (rows, rows), 1)
    same = jnp.right_shift(r, 6) == jnp.right_shift(c, 6)
    causal = jnp.logical_and(same, r >= c)
    strict = jnp.logical_and(same, r > c)
    return causal, strict


def _chunk_end_rows(x, rows):
    parts = []
    for c in range(rows // CHUNK):
        last = x[c * CHUNK + CHUNK - 1:c * CHUNK + CHUNK, :]
        parts.append(jnp.broadcast_to(last, (CHUNK, x.shape[1])))
    return jnp.concatenate(parts, axis=0)


def _head_rms(o, w_row):
    ms = jnp.mean(o * o, axis=-1, keepdims=True)
    return o * lax.rsqrt(ms + EPS) * w_row


def _mixer_a_kernel(qkv_ref, z_ref, small_ref, cw_ref, pa_ref, nw_ref, y_ref, cbuf, s_scr):
    rows = qkv_ref.shape[0]
    n_chunks = rows // CHUNK
    hw = N_HEADS * DK_A

    @pl.when(pl.program_id(1) == 0)
    def _():
        cbuf[0:SUBLANES, :] = jnp.zeros((SUBLANES, cbuf.shape[1]), F32)
        s_scr[...] = jnp.zeros(s_scr.shape, F32)

    qkv = _conv_silu(qkv_ref, cw_ref, cbuf)
    causal, strict = _chunk_masks(rows)

    gates = small_ref[...]
    log_decay = pa_ref[0:1, :] * _softplus(gates + pa_ref[1:2, :])
    beta_all = _sigmoid(gates)
    gc = _dot_exact_lhs(causal.astype(BF16), log_decay)
    gc_t = gc.T
    gc_end = _chunk_end_rows(gc, rows)

    heads = range(N_HEADS)
    a_mats, rhs, attns, qgs, k_ends, s_decays = [], [], [], [], [], []
    for h in heads:
        q = qkv[:, h * DK_A:(h + 1) * DK_A]
        k = qkv[:, hw + h * DK_A:hw + (h + 1) * DK_A]
        v = qkv[:, 2 * hw + h * DV:2 * hw + (h + 1) * DV]
        q = q * (lax.rsqrt(jnp.sum(q * q, axis=-1, keepdims=True) + 1e-6) * (DK_A ** -0.5))
        k = k * lax.rsqrt(jnp.sum(k * k, axis=-1, keepdims=True) + 1e-6)
        beta = beta_all[:, LANE_A_BETA + h:LANE_A_BETA + h + 1]
        g_col = gc[:, LANE_A_DECAY + h:LANE_A_DECAY + h + 1]
        g_row = gc_t[LANE_A_DECAY + h:LANE_A_DECAY + h + 1, :]
        ge_col = gc_end[:, LANE_A_DECAY + h:LANE_A_DECAY + h + 1]
        decay = jnp.exp(jnp.where(causal, g_col - g_row, 0.0))
        kb = k * beta
        k16 = k.astype(BF16)
        a_mats.append(jnp.where(strict, _dot(kb.astype(BF16), k16, NT) * decay, 0.0).astype(BF16))
        eg = jnp.exp(g_col)
        rhs.append(jnp.concatenate([v * beta, kb * eg], axis=1))
        attns.append(jnp.where(causal, _dot(q.astype(BF16), k16, NT) * decay, 0.0).astype(BF16))
        qgs.append((q * eg).astype(BF16))
        k_ends.append((k * jnp.exp(ge_col - g_col)).astype(BF16))
        s_decays.append(jnp.exp(ge_col))

    xs = [rhs[h] - _dot(a_mats[h], rhs[h].astype(BF16)) for h in heads]
    ps = [_dot(a_mats[h], a_mats[h]).astype(BF16) for h in heads]
    for it in range(5):
        xs = [xs[h] + _dot(ps[h], xs[h].astype(BF16)) for h in heads]
        if it < 4:
            ps = [_dot(ps[h], ps[h]).astype(BF16) for h in heads]
    us = [xs[h][:, :DV] for h in heads]
    ws = [xs[h][:, DV:].astype(BF16) for h in heads]

    states = [s_scr[h] for h in heads]
    deltas = [[] for _ in heads]
    inters = [[] for _ in heads]
    for c in range(n_chunks):
        lo, hi = c * CHUNK, (c + 1) * CHUNK
        for h in heads:
            s16 = states[h].astype(BF16)
            delta = us[h][lo:hi] - _dot(ws[h][lo:hi], s16)
            inters[h].append(_dot(qgs[h][lo:hi], s16))
            sd = jnp.concatenate([s_decays[h][lo:hi]] * (DK_A // CHUNK), axis=0)
            states[h] = sd * states[h] + _dot(k_ends[h][lo:hi], delta.astype(BF16), TN)
            deltas[h].append(delta.astype(BF16))
    for h in heads:
        s_scr[h] = states[h]
        o = jnp.concatenate(inters[h], axis=0) + _dot(attns[h], jnp.concatenate(deltas[h], axis=0))
        z = z_ref[:, h * DV:(h + 1) * DV].astype(F32)
        y_ref[:, h * DV:(h + 1) * DV] = (_head_rms(o, nw_ref[...]) * _silu(z)).astype(y_ref.dtype)


def _mixer_a_call(qkv, z, small, conv_w, pa, norm_w, bsz, seq):
    per_seq = seq // MIX_ROWS
    cw = qkv.shape[1]
    row = lambda b, j: (b * per_seq + j, 0)
    const = lambda b, j: (0, 0)
    return pl.pallas_call(
        _mixer_a_kernel,
        grid=(bsz, per_seq),
        in_specs=[
            pl.BlockSpec((MIX_ROWS, cw), row),
            pl.BlockSpec((MIX_ROWS, N_HEADS * DV), row),
            pl.BlockSpec((MIX_ROWS, LANES), row),
            pl.BlockSpec((CONV_K, cw), const),
            pl.BlockSpec((SUBLANES, LANES), const),
            pl.BlockSpec((1, DV), const),
        ],
        out_specs=pl.BlockSpec((MIX_ROWS, N_HEADS * DV), row),
        out_shape=jax.ShapeDtypeStruct((bsz * seq, N_HEADS * DV), BF16),
        scratch_shapes=[
            pltpu.VMEM((MIX_ROWS + 2 * SUBLANES, cw), F32),
            pltpu.VMEM((N_HEADS, DK_A, DV), F32),
        ],
        compiler_params=_params(2),
        name="mixer_deltanet",
    )(qkv, z, small, conv_w, pa, norm_w)


def _mixer_b_kernel(qk_ref, v_ref, og_ref, small_ref, cw_ref, pb_ref, nw_ref, y_ref, cbuf, e_scr, m_scr):
    rows = qk_ref.shape[0]
    n_chunks = rows // CHUNK
    hw = N_HEADS * DK_B

    @pl.when(pl.program_id(1) == 0)
    def _():
        cbuf[0:SUBLANES, :] = jnp.zeros((SUBLANES, cbuf.shape[1]), F32)
        e_scr[...] = jnp.zeros(e_scr.shape, F32)
        m_scr[...] = jnp.zeros(m_scr.shape, F32)

    qk = _conv_silu(qk_ref, cw_ref, cbuf)
    causal, _ = _chunk_masks(rows)

    pre = small_ref[...] + pb_ref[0:1, :]
    log_f = _log_sigmoid(pre)
    bcum = _dot_exact_lhs(causal.astype(BF16), log_f)
    bcum_t = bcum.T
    pre_t = pre.T
    bend = _chunk_end_rows(bcum, rows)
    ones = jnp.ones((rows, DV), BF16)

    heads = range(N_HEADS)
    qs, v_exts, b_cols, be_cols, m_intras, m_chunks, wts, k_ws = [], [], [], [], [], [], [], []
    for h in heads:
        q = qk[:, h * DK_B:(h + 1) * DK_B].astype(BF16)
        k = qk[:, hw + h * DK_B:hw + (h + 1) * DK_B] * (DK_B ** -0.5)
        lf = LANE_B_FORGET + h
        li = LANE_B_IN + h
        b_col = bcum[:, lf:lf + 1]
        b_row = bcum_t[lf:lf + 1, :]
        i_row = pre_t[li:li + 1, :]
        i_col = pre[:, li:li + 1]
        be_col = bend[:, lf:lf + 1]
        dlog = jnp.where(causal, b_col - b_row + i_row, -jnp.inf)
        m_intra = jnp.max(dlog, axis=-1, keepdims=True)
        m_chunk = _chunk_end_rows(m_intra, rows)
        wts.append((jnp.exp(dlog - m_intra) * _dot(q, k.astype(BF16), NT)).astype(BF16))
        k_ws.append((k * jnp.exp(be_col - b_col + i_col - m_chunk)).astype(BF16))
        qs.append(q)
        v_exts.append(jnp.concatenate([v_ref[:, h * DV:(h + 1) * DV], ones], axis=1))
        b_cols.append(b_col)
        be_cols.append(be_col)
        m_intras.append(m_intra)
        m_chunks.append(m_chunk)

    intras = [_dot(wts[h], v_exts[h]) for h in heads]
    kvs = [[_dot(k_ws[h][c * CHUNK:(c + 1) * CHUNK], v_exts[h][c * CHUNK:(c + 1) * CHUNK], TN)
            for c in range(n_chunks)] for h in heads]

    es = [e_scr[h] for h in heads]
    m_ss = [m_scr[:, h:h + 1] for h in heads]
    outs = [[] for _ in heads]
    for c in range(n_chunks):
        lo, hi = c * CHUNK, (c + 1) * CHUNK
        for h in heads:
            m_s = m_ss[h]
            m_tot = jnp.maximum(b_cols[h][lo:hi] + m_s, m_intras[h][lo:hi])
            s_inter = jnp.exp(b_cols[h][lo:hi] + m_s - m_tot)
            s_intra = jnp.exp(m_intras[h][lo:hi] - m_tot)
            both = s_inter * _dot(qs[h][lo:hi], es[h].astype(BF16)) + s_intra * intras[h][lo:hi]
            den = jnp.maximum(jnp.abs(both[:, DV:]), jnp.exp(-m_tot))
            outs[h].append(both[:, :DV] / den)
            m_new = jnp.maximum(be_cols[h][lo:hi] + m_s, m_chunks[h][lo:hi])
            s_prev = jnp.exp(be_cols[h][lo:hi] + m_s - m_new)
            s_cur = jnp.exp(m_chunks[h][lo:hi] - m_new)
            es[h] = s_prev * es[h] + s_cur * kvs[h][c]
            m_ss[h] = m_new
    for h in heads:
        e_scr[h] = es[h]
        m_scr[:, h:h + 1] = m_ss[h]
        o = jnp.concatenate(outs[h], axis=0)
        og = og_ref[:, h * DV:(h + 1) * DV].astype(F32)
        y_ref[:, h * DV:(h + 1) * DV] = (_sigmoid(og) * _head_rms(o, nw_ref[...])).astype(y_ref.dtype)


def _mixer_b_call(qk, v, og, small, conv_w, pb, norm_w, bsz, seq):
    per_seq = seq // MIX_ROWS
    cw = qk.shape[1]
    row = lambda b, j: (b * per_seq + j, 0)
    const = lambda b, j: (0, 0)
    return pl.pallas_call(
        _mixer_b_kernel,
        grid=(bsz, per_seq),
        in_specs=[
            pl.BlockSpec((MIX_ROWS, cw), row),
            pl.BlockSpec((MIX_ROWS, N_HEADS * DV), row),
            pl.BlockSpec((MIX_ROWS, N_HEADS * DV), row),
            pl.BlockSpec((MIX_ROWS, LANES), row),
            pl.BlockSpec((CONV_K, cw), const),
            pl.BlockSpec((SUBLANES, LANES), const),
            pl.BlockSpec((1, DV), const),
        ],
        out_specs=pl.BlockSpec((MIX_ROWS, N_HEADS * DV), row),
        out_shape=jax.ShapeDtypeStruct((bsz * seq, N_HEADS * DV), BF16),
        scratch_shapes=[
            pltpu.VMEM((MIX_ROWS + 2 * SUBLANES, cw), F32),
            pltpu.VMEM((N_HEADS, DK_B, 2 * DV), F32),
            pltpu.VMEM((CHUNK, LANES), F32),
        ],
        compiler_params=_params(2),
        name="mixer_mlstm",
    )(qk, v, og, small, conv_w, pb, norm_w)


def _mixer_c_kernel(qk_ref, v_ref, z_ref, small_ref, wg_ref, bg_ref, nw_ref, y_ref, st_scr):
    rows = qk_ref.shape[0]
    n_chunks = rows // CHUNK
    hw = N_HEADS * DK_C

    @pl.when(pl.program_id(1) == 0)
    def _():
        st_scr[...] = jnp.zeros(st_scr.shape, F32)

    causal, _ = _chunk_masks(rows)
    log_a = _log_sigmoid(_dot_hi(small_ref[...], wg_ref[...]) + bg_ref[...]) * (1.0 / GLA_TAU)
    bc = _dot_exact_lhs(causal.astype(BF16), log_a)
    parts = []
    for c in range(n_chunks):
        mid = bc[c * CHUNK + CHUNK // 2:c * CHUNK + CHUNK // 2 + 1, :]
        parts.append(jnp.broadcast_to(mid, (CHUNK, hw)))
    rel = bc - jnp.concatenate(parts, axis=0)
    bc_end = _chunk_end_rows(bc, rows)
    qk = qk_ref[...].astype(F32)
    q_all = qk[:, :hw] * (DK_C ** -0.5)
    k_all = qk[:, hw:]
    q_rel = (q_all * jnp.exp(rel)).astype(BF16)
    k_rel = (k_all * jnp.exp(-rel)).astype(BF16)
    k_end = (k_all * jnp.exp(bc_end - bc)).astype(BF16)
    q_abs = (q_all * jnp.exp(bc)).astype(BF16)
    a_end = jnp.exp(bc_end)

    heads = range(N_HEADS)
    sls = [slice(h * DK_C, (h + 1) * DK_C) for h in heads]
    vs = [v_ref[:, h * DV:(h + 1) * DV] for h in heads]
    attns = [jnp.where(causal, _dot(q_rel[:, sls[h]], k_rel[:, sls[h]], NT), 0.0).astype(BF16) for h in heads]
    o_intras = [_dot(attns[h], vs[h]) for h in heads]
    kvs = [[_dot(vs[h][c * CHUNK:(c + 1) * CHUNK], k_end[c * CHUNK:(c + 1) * CHUNK, sls[h]], TN)
            for c in range(n_chunks)] for h in heads]
    sts = [st_scr[h] for h in heads]
    inters = [[] for _ in heads]
    for c in range(n_chunks):
        lo, hi = c * CHUNK, (c + 1) * CHUNK
        for h in heads:
            inters[h].append(_dot(q_abs[lo:hi, sls[h]], sts[h].astype(BF16), NT))
            sts[h] = sts[h] * a_end[lo:lo + 1, sls[h]] + kvs[h][c]
    for h in heads:
        st_scr[h] = sts[h]
        o = jnp.concatenate(inters[h], axis=0) + o_intras[h]
        z = z_ref[:, h * DV:(h + 1) * DV].astype(F32)
        y_ref[:, h * DV:(h + 1) * DV] = (_head_rms(o, nw_ref[...]) * _silu(z)).astype(y_ref.dtype)


def _mixer_c_call(qk, v, z, small, wg, bg, norm_w, bsz, seq):
    per_seq = seq // MIX_ROWS
    cw = qk.shape[1]
    row = lambda b, j: (b * per_seq + j, 0)
    const = lambda b, j: (0, 0)
    return pl.pallas_call(
        _mixer_c_kernel,
        grid=(bsz, per_seq),
        in_specs=[
            pl.BlockSpec((MIX_ROWS, cw), row),
            pl.BlockSpec((MIX_ROWS, N_HEADS * DV), row),
            pl.BlockSpec((MIX_ROWS, N_HEADS * DV), row),
            pl.BlockSpec((MIX_ROWS, LANES), row),
            pl.BlockSpec((LANES, N_HEADS * DK_C), const),
            pl.BlockSpec((1, N_HEADS * DK_C), const),
            pl.BlockSpec((1, DV), const),
        ],
        out_specs=pl.BlockSpec((MIX_ROWS, N_HEADS * DV), row),
        out_shape=jax.ShapeDtypeStruct((bsz * seq, N_HEADS * DV), BF16),
        scratch_shapes=[pltpu.VMEM((N_HEADS, DV, DK_C), F32)],
        compiler_params=_params(2),
        name="mixer_gla",
    )(qk, v, z, small, wg, bg, norm_w)


def _merge_kernel(x_ref, h_ref, ya_ref, yb_ref, yc_ref, mod_ref, wmg_ref, bmg_ref, wbr_ref, wo_ref, o_ref):
    for rows in _row_halves(x_ref.shape[0]):
        h = h_ref[rows, :]
        merged = None
        for g, y_ref in enumerate((ya_ref, yb_ref, yc_ref)):
            gate = _sigmoid(_dot(h, wmg_ref[g]) + bmg_ref[g])
            term = gate * _dot(y_ref[rows, :], wbr_ref[g])
            merged = term if merged is None else merged + term
        o_ref[rows, :] = x_ref[rows, :] + mod_ref[0, 2:3, :] * _bdot(merged, wo_ref[...])


def _merge_call(x, h, ya, yb, yc, mod, w_mg, b_mg, w_br, w_o, seq):
    t, d = x.shape
    tm = _row_tile(seq, 1024)
    per_seq = seq // tm
    bw = ya.shape[1]
    row = lambda i: (i, 0)
    return pl.pallas_call(
        _merge_kernel,
        grid=(t // tm,),
        in_specs=[
            pl.BlockSpec((tm, d), row),
            pl.BlockSpec((tm, d), row),
            pl.BlockSpec((tm, bw), row),
            pl.BlockSpec((tm, bw), row),
            pl.BlockSpec((tm, bw), row),
            pl.BlockSpec((1, 6, d), lambda i: (i // per_seq, 0, 0)),
            pl.BlockSpec((3, d, d), lambda i: (0, 0, 0), **_resident()),
            pl.BlockSpec((3, 1, d), lambda i: (0, 0, 0), **_resident()),
            pl.BlockSpec((3, bw, d), lambda i: (0, 0, 0), **_resident()),
            pl.BlockSpec((d, d), lambda i: (0, 0), **_resident()),
        ],
        out_specs=pl.BlockSpec((tm, d), row),
        out_shape=jax.ShapeDtypeStruct((t, d), F32),
        compiler_params=_params(1),
        name="merge",
    )(x, h, ya, yb, yc, mod, w_mg, b_mg, w_br, w_o)


ROUTE_E1, ROUTE_E2, ROUTE_W1, ROUTE_W2, ROUTE_R1, ROUTE_R2 = range(6)

TOKEN_TILE = SUBLANES * LANES


def _store_token_tiles(ref, x):
    n = x.shape[0]
    for j in range(SUBLANES):
        ref[pl.ds(j, n, stride=SUBLANES), :] = x[:, j * LANES:(j + 1) * LANES]


def _load_token_tiles(ref):
    n = ref.shape[0] // SUBLANES
    return jnp.concatenate([ref[pl.ds(j, n, stride=SUBLANES), :] for j in range(SUBLANES)], axis=1)


def _router_kernel(n_experts, x_ref, mod_ref, g_ref, wr_ref, br_ref, h_ref, route_ref, count_ref, carry):
    @pl.when(pl.program_id(0) == 0)
    def _():
        carry[...] = jnp.zeros(carry.shape, F32)

    h = _norm_mod(x_ref[...], g_ref[...], mod_ref[0, 3:4, :], mod_ref[0, 4:5, :])
    _store_token_tiles(h_ref, h)
    logits = _dot_hi(h, wr_ref[...]) + br_ref[...]
    tm = logits.shape[0]
    lane = lax.broadcasted_iota(jnp.int32, logits.shape, 1)
    logits = jnp.where(lane < n_experts, logits, -jnp.inf)
    m1 = jnp.max(logits, axis=-1, keepdims=True)
    i1 = jnp.min(jnp.where(logits == m1, lane, LANES), axis=-1, keepdims=True)
    rest = jnp.where(lane == i1, -jnp.inf, logits)
    m2 = jnp.max(rest, axis=-1, keepdims=True)
    i2 = jnp.min(jnp.where(rest == m2, lane, LANES), axis=-1, keepdims=True)
    e2 = jnp.exp(m2 - m1)
    w1 = 1.0 / (1.0 + e2)
    w2 = e2 * w1

    sel1 = lane == i1
    sel2 = lane == i2
    onehot = jnp.where(jnp.logical_or(sel1, sel2), 1.0, 0.0)
    r = lax.broadcasted_iota(jnp.int32, (tm, tm), 0)
    c = lax.broadcasted_iota(jnp.int32, (tm, tm), 1)
    before = jnp.where(r > c, 1.0, 0.0).astype(BF16)
    rank = _dot(before, onehot.astype(BF16)) + carry[0:1, :]
    carry[...] = carry[...] + jnp.sum(onehot, axis=0, keepdims=True)
    count_ref[...] = carry[...]
    r1 = jnp.sum(jnp.where(sel1, rank, 0.0), axis=-1, keepdims=True)
    r2 = jnp.sum(jnp.where(sel2, rank, 0.0), axis=-1, keepdims=True)
    rec = jnp.zeros(logits.shape, F32)
    for pos, val in ((ROUTE_E1, i1.astype(F32)), (ROUTE_E2, i2.astype(F32)), (ROUTE_W1, w1), (ROUTE_W2, w2),
                     (ROUTE_R1, r1), (ROUTE_R2, r2)):
        rec = jnp.where(lane == pos, val, rec)
    route_ref[...] = rec


def _router_call(x, mod, g, wr, br, n_experts, seq):
    t, d = x.shape
    tm = _row_tile(seq, 512)
    per_seq = seq // tm
    row = lambda i: (i, 0)
    const = lambda i: (0, 0)
    return pl.pallas_call(
        functools.partial(_router_kernel, n_experts),
        grid=(t // tm,),
        in_specs=[
            pl.BlockSpec((tm, d), row),
            pl.BlockSpec((1, 6, d), lambda i: (i // per_seq, 0, 0)),
            pl.BlockSpec((1, d), const),
            pl.BlockSpec((d, LANES), const),
            pl.BlockSpec((1, LANES), const),
        ],
        out_specs=[pl.BlockSpec((tm * SUBLANES, LANES), row), pl.BlockSpec((tm, LANES), row),
                   pl.BlockSpec((SUBLANES, LANES), const)],
        out_shape=[jax.ShapeDtypeStruct((t * SUBLANES, LANES), F32), jax.ShapeDtypeStruct((t, LANES), F32),
                   jax.ShapeDtypeStruct((SUBLANES, LANES), F32)],
        scratch_shapes=[pltpu.VMEM((SUBLANES, LANES), F32)],
        compiler_params=_params(1),
        name="prenorm_router",
    )(x, mod, g, wr, br)


MOE_ROWS = 1024
COMBINE_ROWS = 512
FFN_DENSE_TILE_CAP = 1536
FFN_MOE_TILE_CAP = 512


def _ffn_tile(d_ff, cap):
    best = None
    for tf in range(LANES, min(cap, d_ff) + 1, LANES):
        if d_ff % tf == 0:
            best = tf
    if best is None:
        raise ValueError(d_ff)
    return best


def _tile_gather_start(row_ref, src_hbm, dst_ref, sem, queues=(0, 1)):
    def issue(i, carry):
        for k in range(SUBLANES):
            r = i * SUBLANES + k
            src_row = pl.multiple_of(row_ref[r], SUBLANES)
            dst_row = pl.multiple_of(r * SUBLANES, SUBLANES)
            pltpu.make_async_copy(src_hbm.at[pl.ds(src_row, SUBLANES)], dst_ref.at[pl.ds(dst_row, SUBLANES)],
                                  sem).start(priority=queues[k % len(queues)])
        return carry

    lax.fori_loop(0, dst_ref.shape[0] // (SUBLANES * SUBLANES), issue, 0)


def _tile_gather_wait(src_hbm, dst_ref, sem):
    pltpu.make_async_copy(src_hbm.at[pl.ds(0, dst_ref.shape[0])], dst_ref, sem).wait()


def _moe_ffn_kernel(te_ref, nu_ref, rows_ref, rows_next_ref, h_hbm, w1_ref, w3_ref, w2_ref, ys_ref,
                    xbuf, sem, hb_scr, acc_ref):
    i = pl.program_id(0)
    f = pl.program_id(1)
    last = pl.num_programs(1) - 1
    n_used = nu_ref[0]
    used = i < n_used
    slot = lax.rem(i, 2)

    @pl.when(jnp.logical_and(f == 0, jnp.logical_and(i == 0, used)))
    def _():
        _tile_gather_start(rows_ref, h_hbm, xbuf.at[0], sem.at[0])

    @pl.when(jnp.logical_and(f == 0, i + 1 < n_used))
    def _():
        _tile_gather_start(rows_next_ref, h_hbm, xbuf.at[1 - slot], sem.at[1 - slot], queues=(1,))

    @pl.when(jnp.logical_and(used, f == 0))
    def _():
        _tile_gather_wait(h_hbm, xbuf.at[slot], sem.at[slot])
        hb_scr[...] = _load_token_tiles(xbuf.at[slot]).astype(BF16)
        acc_ref[...] = jnp.zeros(acc_ref.shape, F32)

    @pl.when(used)
    def _():
        h = hb_scr[...]
        g = _silu(_dot(h, w1_ref[0])) * _dot(h, w3_ref[0])
        acc_ref[...] += _bdot(g, w2_ref[0])

    @pl.when(jnp.logical_and(used, f == last))
    def _():
        _store_token_tiles(ys_ref, acc_ref[...])

    @pl.when(jnp.logical_and(jnp.logical_not(used), f == last))
    def _():
        ys_ref[...] = jnp.zeros(ys_ref.shape, F32)


def _moe_ffn_call(tile_expert, n_used, tok, h_tiles, w1, w3, w2):
    p = tok.shape[0]
    n_tiles = p // MOE_ROWS
    d = TOKEN_TILE
    d_ff = w1.shape[2]
    tf = _ffn_tile(d_ff, FFN_MOE_TILE_CAP)
    rows = tok * SUBLANES
    grid_spec = pltpu.PrefetchScalarGridSpec(
        num_scalar_prefetch=2,
        grid=(n_tiles, d_ff // tf),
        in_specs=[
            pl.BlockSpec((MOE_ROWS,), lambda i, f, te, nu: (i,), memory_space=pltpu.SMEM),
            pl.BlockSpec((MOE_ROWS,), lambda i, f, te, nu: (jnp.minimum(i + 1, n_tiles - 1),),
                         memory_space=pltpu.SMEM),
            pl.BlockSpec(memory_space=pl.ANY),
            pl.BlockSpec((1, d, tf), lambda i, f, te, nu: (te[i], 0, f)),
            pl.BlockSpec((1, d, tf), lambda i, f, te, nu: (te[i], 0, f)),
            pl.BlockSpec((1, tf, d), lambda i, f, te, nu: (te[i], f, 0)),
        ],
        out_specs=pl.BlockSpec((MOE_ROWS * SUBLANES, LANES), lambda i, f, te, nu: (i, 0)),
        scratch_shapes=[
            pltpu.VMEM((2, MOE_ROWS * SUBLANES, LANES), F32),
            pltpu.SemaphoreType.DMA((2,)),
            pltpu.VMEM((MOE_ROWS, d), BF16),
            pltpu.VMEM((MOE_ROWS, d), F32),
        ],
    )
    return pl.pallas_call(
        _moe_ffn_kernel,
        grid_spec=grid_spec,
        out_shape=jax.ShapeDtypeStruct((p * SUBLANES, LANES), F32),
        compiler_params=_params(2),
        name="moe_ffn",
    )(tile_expert, n_used, rows, rows, h_tiles, w1, w3, w2)


def _moe_combine_kernel(final, r1_ref, r2_ref, ys_hbm, route_ref, x_ref, mod_ref, gf_ref, o_ref, buf, sem):
    _tile_gather_start(r1_ref, ys_hbm, buf.at[0], sem.at[0])
    _tile_gather_start(r2_ref, ys_hbm, buf.at[1], sem.at[1])
    _tile_gather_wait(ys_hbm, buf.at[0], sem.at[0])
    _tile_gather_wait(ys_hbm, buf.at[1], sem.at[1])
    route = route_ref[...]
    f = (route[:, ROUTE_W1:ROUTE_W1 + 1] * _load_token_tiles(buf.at[0])
         + route[:, ROUTE_W2:ROUTE_W2 + 1] * _load_token_tiles(buf.at[1]))
    y = x_ref[...] + mod_ref[0, 5:6, :] * f
    if final:
        ms = jnp.mean(y * y, axis=-1, keepdims=True)
        y = y * lax.rsqrt(ms + EPS) * gf_ref[...]
    o_ref[...] = y


def _moe_combine_call(d1, d2, ys, route, x, mod, g_final, seq, final):
    t, d = x.shape
    tm = _row_tile(seq, COMBINE_ROWS)
    per_seq = seq // tm
    row = lambda i: (i, 0)
    idx_spec = pl.BlockSpec((tm,), lambda i: (i,), memory_space=pltpu.SMEM)
    return pl.pallas_call(
        functools.partial(_moe_combine_kernel, final),
        grid=(t // tm,),
        in_specs=[
            idx_spec, idx_spec,
            pl.BlockSpec(memory_space=pl.ANY),
            pl.BlockSpec((tm, LANES), row),
            pl.BlockSpec((tm, d), row),
            pl.BlockSpec((1, 6, d), lambda i: (i // per_seq, 0, 0)),
            pl.BlockSpec((1, d), lambda i: (0, 0)),
        ],
        out_specs=pl.BlockSpec((tm, d), row),
        out_shape=jax.ShapeDtypeStruct((t, d), F32),
        scratch_shapes=[pltpu.VMEM((2, tm * SUBLANES, LANES), F32), pltpu.SemaphoreType.DMA((2,))],
        compiler_params=_params(1),
        name="moe_combine",
    )(d1 * SUBLANES, d2 * SUBLANES, ys, route, x, mod, g_final)


def _moe_layout(route, counts, n_experts, t):
    cnt = counts[0, :n_experts].astype(jnp.int32)
    padded = ((cnt + MOE_ROWS - 1) // MOE_ROWS) * MOE_ROWS
    ends = jnp.cumsum(padded)
    offs = ends - padded
    e1 = route[:, ROUTE_E1].astype(jnp.int32)
    e2 = route[:, ROUTE_E2].astype(jnp.int32)
    d1 = offs[e1] + route[:, ROUTE_R1].astype(jnp.int32)
    d2 = offs[e2] + route[:, ROUTE_R2].astype(jnp.int32)
    p = (TOP_K * t // MOE_ROWS + n_experts) * MOE_ROWS
    ids = jnp.arange(t, dtype=jnp.int32)
    tok = jnp.zeros((p,), jnp.int32).at[jnp.concatenate([d1, d2])].set(jnp.concatenate([ids, ids]))
    starts = jnp.arange(p // MOE_ROWS, dtype=jnp.int32) * MOE_ROWS
    tile_expert = jnp.minimum(jnp.searchsorted(ends, starts, side="right"), n_experts - 1).astype(jnp.int32)
    n_used = (ends[-1:] // MOE_ROWS).astype(jnp.int32)
    return d1, d2, tok, tile_expert, n_used


def _ffn_kernel(x_ref, mod_ref, g_ref, w1_ref, w3_ref, w2_ref, o_ref):
    x = x_ref[...]
    h = _norm_mod(x, g_ref[...], mod_ref[0, 3:4, :], mod_ref[0, 4:5, :]).astype(BF16)
    d_ff = w1_ref.shape[1]
    tf = _ffn_tile(d_ff, FFN_DENSE_TILE_CAP)
    acc = None
    for f0 in range(0, d_ff, tf):
        g = _silu(_dot(h, w1_ref[:, f0:f0 + tf])) * _dot(h, w3_ref[:, f0:f0 + tf])
        part = _bdot(g, w2_ref[f0:f0 + tf, :])
        acc = part if acc is None else acc + part
    o_ref[...] = x + mod_ref[0, 5:6, :] * acc


def _ffn_call(x, mod, g, w1, w3, w2, seq):
    t, d = x.shape
    d_ff = w1.shape[1]
    tm = _row_tile(seq, 512)
    per_seq = seq // tm
    row = lambda i: (i, 0)
    const = lambda i: (0, 0)
    return pl.pallas_call(
        _ffn_kernel,
        grid=(t // tm,),
        in_specs=[
            pl.BlockSpec((tm, d), row),
            pl.BlockSpec((1, 6, d), lambda i: (i // per_seq, 0, 0)),
            pl.BlockSpec((1, d), const),
            pl.BlockSpec((d, d_ff), const, **_resident()),
            pl.BlockSpec((d, d_ff), const, **_resident()),
            pl.BlockSpec((d_ff, d), const, **_resident()),
        ],
        out_specs=pl.BlockSpec((tm, d), row),
        out_shape=jax.ShapeDtypeStruct((t, d), F32),
        compiler_params=_params(1),
        name="ffn_dense",
    )(x, mod, g, w1, w3, w2)


def _final_kernel(x_ref, g_ref, o_ref):
    x = x_ref[...]
    ms = jnp.mean(x * x, axis=-1, keepdims=True)
    o_ref[...] = x * lax.rsqrt(ms + EPS) * g_ref[...]


def _final_call(x, g):
    t, d = x.shape
    tm = _row_tile(t, 512)
    return pl.pallas_call(
        _final_kernel,
        grid=(t // tm,),
        in_specs=[pl.BlockSpec((tm, d), lambda i: (i, 0)), pl.BlockSpec((1, d), lambda i: (0, 0))],
        out_specs=pl.BlockSpec((tm, d), lambda i: (i, 0)),
        out_shape=jax.ShapeDtypeStruct((t, d), F32),
        compiler_params=_params(1),
        name="final_norm",
    )(x, g)


def _lane_row(values, lane0, n_rows=1, row=0):
    out = jnp.zeros((n_rows, LANES), F32)
    return out.at[row, lane0:lane0 + values.shape[0]].set(values.astype(F32))


def _split_w_in(w_in):
    qk_a, v_a = N_HEADS * DK_A, N_HEADS * DV
    qk_b, qk_c, vw = N_HEADS * DK_B, N_HEADS * DK_C, N_HEADS * DV
    sizes = (qk_a, qk_a, v_a, v_a, N_HEADS, N_HEADS,
             qk_b, qk_b, vw, vw, N_HEADS, N_HEADS,
             qk_c, qk_c, vw, vw, GLA_RANK)
    names = ("q_a", "k_a", "v_a", "z_a", "a_a", "b_a", "q_b", "k_b", "v_b", "o_b", "i_b", "f_b",
             "q_c", "k_c", "v_c", "z_c", "g_c")
    assert sum(sizes) == w_in.shape[1]
    cols, off = {}, 0
    for name, size in zip(names, sizes):
        cols[name] = w_in[:, off:off + size]
        off += size
    big = jnp.concatenate([cols[n] for n in ("q_a", "k_a", "v_a", "z_a", "q_b", "k_b", "v_b", "o_b",
                                             "q_c", "k_c", "v_c", "z_c")], axis=1).astype(BF16)
    small = jnp.concatenate([cols[n] for n in ("a_a", "b_a", "i_b", "f_b", "g_c")], axis=1)
    small = jnp.pad(small, ((0, 0), (0, LANES - small.shape[1])))
    s_hi = small.astype(BF16)
    s_lo = (small - s_hi.astype(F32)).astype(BF16)
    return big, s_hi, s_lo


def kernel(x, c, w_ada, b_ada, g_mix, g_ffn, g_final, w_in, conv_a, a_log, dt_bias, norm_a, conv_b, b_i, b_f,
           norm_b, w_gla2, b_gla, norm_c, w_br, w_mg, b_mg, w_o, w1_d, w3_d, w2_d, w_router, b_router,
           w1_e, w3_e, w2_e):
    bsz, seq, d = x.shape
    depth = w_ada.shape[0]
    assert seq % MIX_ROWS == 0 and d % LANES == 0
    t = bsz * seq
    xf = x.reshape(t, d)
    mod_all = _ada_call(c, w_ada, b_ada).reshape(depth, bsz, 6, d)

    for layer in range(depth):
        mod = mod_all[layer]
        w_big, ws_hi, ws_lo = _split_w_in(w_in[layer])
        (h, qkv_a, z_a, qk_b, v_b, o_b, qk_c, v_c, z_c, small) = _inproj_call(
            xf, mod, g_mix[layer].reshape(1, d), w_big, ws_hi, ws_lo, seq)

        pa = (_lane_row(-jnp.exp(a_log[layer]), LANE_A_DECAY, SUBLANES, 0)
              + _lane_row(dt_bias[layer], LANE_A_DECAY, SUBLANES, 1))
        y_a = _mixer_a_call(qkv_a, z_a, small, conv_a[layer], pa, norm_a[layer].reshape(1, DV), bsz, seq)

        pb = (_lane_row(b_i[layer], LANE_B_IN, SUBLANES, 0) + _lane_row(b_f[layer], LANE_B_FORGET, SUBLANES, 0))
        y_b = _mixer_b_call(qk_b, v_b, o_b, small, conv_b[layer], pb, norm_b[layer].reshape(1, DV), bsz, seq)

        wg = jnp.zeros((LANES, N_HEADS * DK_C), F32).at[LANE_C_GATE:LANE_C_GATE + GLA_RANK].set(w_gla2[layer])
        y_c = _mixer_c_call(qk_c, v_c, z_c, small, wg, b_gla[layer].reshape(1, -1),
                            norm_c[layer].reshape(1, DV), bsz, seq)

        xf = _merge_call(xf, h, y_a, y_b, y_c, mod, w_mg[layer].astype(BF16), b_mg[layer][:, None, :],
                         w_br[layer].astype(BF16), w_o[layer].astype(BF16), seq)

        j = layer // 2
        last = layer == depth - 1
        if layer % 2 == 0:
            xf = _ffn_call(xf, mod, g_ffn[layer].reshape(1, d), w1_d[j].astype(BF16), w3_d[j].astype(BF16),
                           w2_d[j].astype(BF16), seq)
            if last:
                xf = _final_call(xf, g_final.reshape(1, d))
        else:
            n_experts = w_router.shape[2]
            wr = jnp.pad(w_router[j], ((0, 0), (0, LANES - n_experts)))
            br = jnp.pad(b_router[j], (0, LANES - n_experts)).reshape(1, LANES)
            assert d == TOKEN_TILE, "gathered rows are stored as one (SUBLANES, LANES) tile per token"
            h2, route, counts = _router_call(xf, mod, g_ffn[layer].reshape(1, d), wr, br, n_experts, seq)
            d1, d2, tok, tile_expert, n_used = _moe_layout(route, counts, n_experts, t)
            ys = _moe_ffn_call(tile_expert, n_used, tok, h2, w1_e[j].astype(BF16), w3_e[j].astype(BF16),
                               w2_e[j].astype(BF16))
            xf = _moe_combine_call(d1, d2, ys, route, xf, mod, g_final.reshape(1, d), seq, last)

    return xf.reshape(bsz, seq, d)
```

```python
import functools

import jax
import jax.numpy as jnp
from jax import lax
from jax.experimental import pallas as pl
from jax.experimental.pallas import tpu as pltpu

F32 = jnp.float32
BF16 = jnp.bfloat16

EPS = 1e-6
CHUNK = 64
CONV_K = 4
N_HEADS = 4
DV = 128
DK_A = 128
DK_B = 64
DK_C = 64
GLA_RANK = 16
GLA_TAU = 16.0
TOP_K = 2

V7X_MXU_DIM = 256
MIX_ROWS = V7X_MXU_DIM
LANES = 128
SUBLANES = 8
V7X_VMEM_LIMIT = 56 * 1024 * 1024

LANE_A_DECAY = 0
LANE_A_BETA = 4
LANE_B_IN = 8
LANE_B_FORGET = 12
LANE_C_GATE = 16

NN = (((1,), (0,)), ((), ()))
NT = (((1,), (1,)), ((), ()))
TN = (((0,), (0,)), ((), ()))


def _dot(a, b, dims=NN):
    return lax.dot_general(a, b, dims, preferred_element_type=F32)


def _bdot(a, b, dims=NN):
    return _dot(a.astype(BF16), b.astype(BF16), dims)


def _split3(x):
    hi = x.astype(BF16)
    r1 = x - hi.astype(F32)
    mid = r1.astype(BF16)
    lo = (r1 - mid.astype(F32)).astype(BF16)
    return hi, mid, lo


def _dot_exact_lhs(a, b, dims=NN):
    hi, mid, lo = _split3(b)
    return _dot(a, hi, dims) + _dot(a, mid, dims) + _dot(a, lo, dims)


def _dot_hi(a, b, dims=NN):
    a_hi = a.astype(BF16)
    a_lo = (a - a_hi.astype(F32)).astype(BF16)
    b_hi = b.astype(BF16)
    b_lo = (b - b_hi.astype(F32)).astype(BF16)
    return _dot(a_hi, b_hi, dims) + _dot(a_hi, b_lo, dims) + _dot(a_lo, b_hi, dims)


def _sigmoid(x):
    return 1.0 / (1.0 + jnp.exp(-x))


def _silu(x):
    return x * _sigmoid(x)


def _softplus(x):
    return jnp.maximum(x, 0.0) + jnp.log(1.0 + jnp.exp(-jnp.abs(x)))


def _log_sigmoid(x):
    return jnp.minimum(x, 0.0) - jnp.log(1.0 + jnp.exp(-jnp.abs(x)))


def _params(n_axes):
    return pltpu.CompilerParams(
        dimension_semantics=("arbitrary",) * n_axes,
        vmem_limit_bytes=V7X_VMEM_LIMIT,
    )


def _row_tile(t, want):
    tm = min(want, t)
    assert t % tm == 0, (t, tm)
    return tm


ROW_SPLIT = 2


def _row_halves(tm):
    half = tm // ROW_SPLIT
    return [slice(s * half, (s + 1) * half) for s in range(ROW_SPLIT)]


def _resident():
    return dict(pipeline_mode=pl.Buffered(1))


def _ada_kernel(c_ref, w_ref, b_ref, o_ref):
    c = c_ref[...]
    o_ref[0] = _bdot(_silu(c), w_ref[0]) + b_ref[0]


def _ada_call(c, w_ada, b_ada):
    depth, d, d6 = w_ada.shape
    bsz = c.shape[0]
    n_col = d6 // d
    return pl.pallas_call(
        _ada_kernel,
        grid=(depth, n_col),
        in_specs=[
            pl.BlockSpec((bsz, d), lambda l, n: (0, 0)),
            pl.BlockSpec((1, d, d), lambda l, n: (l, 0, n)),
            pl.BlockSpec((1, 1, d), lambda l, n: (l, 0, n)),
        ],
        out_specs=pl.BlockSpec((1, bsz, d), lambda l, n: (l, 0, n)),
        out_shape=jax.ShapeDtypeStruct((depth, bsz, d6), F32),
        compiler_params=_params(2),
        name="ada_mod",
    )(c, w_ada, b_ada.reshape(depth, 1, d6))


def _norm_mod(x, g, shift, scale):
    ms = jnp.mean(x * x, axis=-1, keepdims=True)
    return (x * lax.rsqrt(ms + EPS) * g) * (1.0 + scale) + shift


IN_WIDTHS = (1536, 512, 512, 512, 512, 512, 512, 512)


def _inproj_kernel(x_ref, mod_ref, g_ref, w_ref, wsh_ref, wsl_ref, h_ref, *out_refs):
    halves = _row_halves(x_ref.shape[0])
    hs = []
    for rows in halves:
        h = _norm_mod(x_ref[rows, :], g_ref[...], mod_ref[0, 0:1, :], mod_ref[0, 1:2, :])
        hb = h.astype(BF16)
        h_ref[rows, :] = hb.astype(h_ref.dtype)
        hs.append((h, hb))
    for rows, (h, hb) in zip(halves, hs):
        off = 0
        for o_ref, width in zip(out_refs[:-1], IN_WIDTHS):
            o_ref[rows, :] = _dot(hb, w_ref[:, off:off + width]).astype(o_ref.dtype)
            off += width
        hl = (h - hb.astype(F32)).astype(BF16)
        wsh = wsh_ref[...]
        out_refs[-1][rows, :] = _dot(hb, wsh) + _dot(hl, wsh) + _dot(hb, wsl_ref[...])


def _inproj_call(x, mod, g, w_big, ws_hi, ws_lo, seq):
    t, d = x.shape
    tm = _row_tile(seq, 1024)
    per_seq = seq // tm
    n_big = w_big.shape[1]
    row = lambda i: (i, 0)
    const = lambda i: (0, 0)
    out_shapes = [jax.ShapeDtypeStruct((t, d), BF16)]
    out_specs = [pl.BlockSpec((tm, d), row)]
    for width in IN_WIDTHS:
        out_shapes.append(jax.ShapeDtypeStruct((t, width), BF16))
        out_specs.append(pl.BlockSpec((tm, width), row))
    out_shapes.append(jax.ShapeDtypeStruct((t, LANES), F32))
    out_specs.append(pl.BlockSpec((tm, LANES), row))
    return pl.pallas_call(
        _inproj_kernel,
        grid=(t // tm,),
        in_specs=[
            pl.BlockSpec((tm, d), row),
            pl.BlockSpec((1, 6, d), lambda i: (i // per_seq, 0, 0)),
            pl.BlockSpec((1, d), const),
            pl.BlockSpec((d, n_big), const, **_resident()),
            pl.BlockSpec((d, LANES), const, **_resident()),
            pl.BlockSpec((d, LANES), const, **_resident()),
        ],
        out_specs=out_specs,
        out_shape=out_shapes,
        compiler_params=_params(1),
        name="in_proj",
    )(x, mod, g, w_big, ws_hi, ws_lo)


def _conv_silu(x_ref, cw_ref, cbuf):
    rows = x_ref.shape[0]
    x = x_ref[...].astype(F32)
    cbuf[SUBLANES:SUBLANES + rows, :] = x
    acc = x * cw_ref[CONV_K - 1:CONV_K, :]
    for s in range(1, CONV_K):
        acc = acc + cbuf[SUBLANES - s:SUBLANES - s + rows, :] * cw_ref[CONV_K - 1 - s:CONV_K - s, :]
    cbuf[0:SUBLANES, :] = cbuf[rows:rows + SUBLANES, :]
    return _silu(acc)


def _chunk_masks(rows):
    r = lax.broadcasted_iota(jnp.int32, (rows, rows), 0)
    c = lax.broadcasted_iota(jnp.int32, (rows, rows), 1)
    same = jnp.right_shift(r, 6) == jnp.right_shift(c, 6)
    causal = jnp.logical_and(same, r >= c)
    strict = jnp.logical_and(same, r > c)
    return causal, strict


def _chunk_end_rows(x, rows):
    parts = []
    for c in range(rows // CHUNK):
        last = x[c * CHUNK + CHUNK - 1:c * CHUNK + CHUNK, :]
        parts.append(jnp.broadcast_to(last, (CHUNK, x.shape[1])))
    return jnp.concatenate(parts, axis=0)


def _head_rms(o, w_row):
    ms = jnp.mean(o * o, axis=-1, keepdims=True)
    return o * lax.rsqrt(ms + EPS) * w_row


def _mixer_a_kernel(qkv_ref, z_ref, small_ref, cw_ref, pa_ref, nw_ref, y_ref, cbuf, s_scr):
    rows = qkv_ref.shape[0]
    n_chunks = rows // CHUNK
    hw = N_HEADS * DK_A

    @pl.when(pl.program_id(1) == 0)
    def _():
        cbuf[0:SUBLANES, :] = jnp.zeros((SUBLANES, cbuf.shape[1]), F32)
        s_scr[...] = jnp.zeros(s_scr.shape, F32)

    qkv = _conv_silu(qkv_ref, cw_ref, cbuf)
    causal, strict = _chunk_masks(rows)

    gates = small_ref[...]
    log_decay = pa_ref[0:1, :] * _softplus(gates + pa_ref[1:2, :])
    beta_all = _sigmoid(gates)
    gc = _dot_exact_lhs(causal.astype(BF16), log_decay)
    gc_t = gc.T
    gc_end = _chunk_end_rows(gc, rows)

    heads = range(N_HEADS)
    a_mats, rhs, attns, qgs, k_ends, s_decays = [], [], [], [], [], []
    for h in heads:
        q = qkv[:, h * DK_A:(h + 1) * DK_A]
        k = qkv[:, hw + h * DK_A:hw + (h + 1) * DK_A]
        v = qkv[:, 2 * hw + h * DV:2 * hw + (h + 1) * DV]
        q = q * (lax.rsqrt(jnp.sum(q * q, axis=-1, keepdims=True) + 1e-6) * (DK_A ** -0.5))
        k = k * lax.rsqrt(jnp.sum(k * k, axis=-1, keepdims=True) + 1e-6)
        beta = beta_all[:, LANE_A_BETA + h:LANE_A_BETA + h + 1]
        g_col = gc[:, LANE_A_DECAY + h:LANE_A_DECAY + h + 1]
        g_row = gc_t[LANE_A_DECAY + h:LANE_A_DECAY + h + 1, :]
        ge_col = gc_end[:, LANE_A_DECAY + h:LANE_A_DECAY + h + 1]
        decay = jnp.exp(jnp.where(causal, g_col - g_row, 0.0))
        kb = k * beta
        k16 = k.astype(BF16)
        a_mats.append(jnp.where(strict, _dot(kb.astype(BF16), k16, NT) * decay, 0.0).astype(BF16))
        eg = jnp.exp(g_col)
        rhs.append(jnp.concatenate([v * beta, kb * eg], axis=1))
        attns.append(jnp.where(causal, _dot(q.astype(BF16), k16, NT) * decay, 0.0).astype(BF16))
        qgs.append((q * eg).astype(BF16))
        k_ends.append((k * jnp.exp(ge_col - g_col)).astype(BF16))
        s_decays.append(jnp.exp(ge_col))

    xs = [rhs[h] - _dot(a_mats[h], rhs[h].astype(BF16)) for h in heads]
    ps = [_dot(a_mats[h], a_mats[h]).astype(BF16) for h in heads]
    for it in range(5):
        xs = [xs[h] + _dot(ps[h], xs[h].astype(BF16)) for h in heads]
        if it < 4:
            ps = [_dot(ps[h], ps[h]).astype(BF16) for h in heads]
    us = [xs[h][:, :DV] for h in heads]
    ws = [xs[h][:, DV:].astype(BF16) for h in heads]

    states = [s_scr[h] for h in heads]
    deltas = [[] for _ in heads]
    inters = [[] for _ in heads]
    for c in range(n_chunks):
        lo, hi = c * CHUNK, (c + 1) * CHUNK
        for h in heads:
            s16 = states[h].astype(BF16)
            delta = us[h][lo:hi] - _dot(ws[h][lo:hi], s16)
            inters[h].append(_dot(qgs[h][lo:hi], s16))
            sd = jnp.concatenate([s_decays[h][lo:hi]] * (DK_A // CHUNK), axis=0)
            states[h] = sd * states[h] + _dot(k_ends[h][lo:hi], delta.astype(BF16), TN)
            deltas[h].append(delta.astype(BF16))
    for h in heads:
        s_scr[h] = states[h]
        o = jnp.concatenate(inters[h], axis=0) + _dot(attns[h], jnp.concatenate(deltas[h], axis=0))
        z = z_ref[:, h * DV:(h + 1) * DV].astype(F32)
        y_ref[:, h * DV:(h + 1) * DV] = (_head_rms(o, nw_ref[...]) * _silu(z)).astype(y_ref.dtype)


def _mixer_a_call(qkv, z, small, conv_w, pa, norm_w, bsz, seq):
    per_seq = seq // MIX_ROWS
    cw = qkv.shape[1]
    row = lambda b, j: (b * per_seq + j, 0)
    const = lambda b, j: (0, 0)
    return pl.pallas_call(
        _mixer_a_kernel,
        grid=(bsz, per_seq),
        in_specs=[
            pl.BlockSpec((MIX_ROWS, cw), row),
            pl.BlockSpec((MIX_ROWS, N_HEADS * DV), row),
            pl.BlockSpec((MIX_ROWS, LANES), row),
            pl.BlockSpec((CONV_K, cw), const),
            pl.BlockSpec((SUBLANES, LANES), const),
            pl.BlockSpec((1, DV), const),
        ],
        out_specs=pl.BlockSpec((MIX_ROWS, N_HEADS * DV), row),
        out_shape=jax.ShapeDtypeStruct((bsz * seq, N_HEADS * DV), BF16),
        scratch_shapes=[
            pltpu.VMEM((MIX_ROWS + 2 * SUBLANES, cw), F32),
            pltpu.VMEM((N_HEADS, DK_A, DV), F32),
        ],
        compiler_params=_params(2),
        name="mixer_deltanet",
    )(qkv, z, small, conv_w, pa, norm_w)


def _mixer_b_kernel(qk_ref, v_ref, og_ref, small_ref, cw_ref, pb_ref, nw_ref, y_ref, cbuf, e_scr, m_scr):
    rows = qk_ref.shape[0]
    n_chunks = rows // CHUNK
    hw = N_HEADS * DK_B

    @pl.when(pl.program_id(1) == 0)
    def _():
        cbuf[0:SUBLANES, :] = jnp.zeros((SUBLANES, cbuf.shape[1]), F32)
        e_scr[...] = jnp.zeros(e_scr.shape, F32)
        m_scr[...] = jnp.zeros(m_scr.shape, F32)

    qk = _conv_silu(qk_ref, cw_ref, cbuf)
    causal, _ = _chunk_masks(rows)

    pre = small_ref[...] + pb_ref[0:1, :]
    log_f = _log_sigmoid(pre)
    bcum = _dot_exact_lhs(causal.astype(BF16), log_f)
    bcum_t = bcum.T
    pre_t = pre.T
    bend = _chunk_end_rows(bcum, rows)
    ones = jnp.ones((rows, DV), BF16)

    heads = range(N_HEADS)
    qs, v_exts, b_cols, be_cols, m_intras, m_chunks, wts, k_ws = [], [], [], [], [], [], [], []
    for h in heads:
        q = qk[:, h * DK_B:(h + 1) * DK_B].astype(BF16)
        k = qk[:, hw + h * DK_B:hw + (h + 1) * DK_B] * (DK_B ** -0.5)
        lf = LANE_B_FORGET + h
        li = LANE_B_IN + h
        b_col = bcum[:, lf:lf + 1]
        b_row = bcum_t[lf:lf + 1, :]
        i_row = pre_t[li:li + 1, :]
        i_col = pre[:, li:li + 1]
        be_col = bend[:, lf:lf + 1]
        dlog = jnp.where(causal, b_col - b_row + i_row, -jnp.inf)
        m_intra = jnp.max(dlog, axis=-1, keepdims=True)
        m_chunk = _chunk_end_rows(m_intra, rows)
        wts.append((jnp.exp(dlog - m_intra) * _dot(q, k.astype(BF16), NT)).astype(BF16))
        k_ws.append((k * jnp.exp(be_col - b_col + i_col - m_chunk)).astype(BF16))
        qs.append(q)
        v_exts.append(jnp.concatenate([v_ref[:, h * DV:(h + 1) * DV], ones], axis=1))
        b_cols.append(b_col)
        be_cols.append(be_col)
        m_intras.append(m_intra)
        m_chunks.append(m_chunk)

    intras = [_dot(wts[h], v_exts[h]) for h in heads]
    kvs = [[_dot(k_ws[h][c * CHUNK:(c + 1) * CHUNK], v_exts[h][c * CHUNK:(c + 1) * CHUNK], TN)
            for c in range(n_chunks)] for h in heads]

    es = [e_scr[h] for h in heads]
    m_ss = [m_scr[:, h:h + 1] for h in heads]
    outs = [[] for _ in heads]
    for c in range(n_chunks):
        lo, hi = c * CHUNK, (c + 1) * CHUNK
        for h in heads:
            m_s = m_ss[h]
            m_tot = jnp.maximum(b_cols[h][lo:hi] + m_s, m_intras[h][lo:hi])
            s_inter = jnp.exp(b_cols[h][lo:hi] + m_s - m_tot)
            s_intra = jnp.exp(m_intras[h][lo:hi] - m_tot)
            both = s_inter * _dot(qs[h][lo:hi], es[h].astype(BF16)) + s_intra * intras[h][lo:hi]
            den = jnp.maximum(jnp.abs(both[:, DV:]), jnp.exp(-m_tot))
            outs[h].append(both[:, :DV] / den)
            m_new = jnp.maximum(be_cols[h][lo:hi] + m_s, m_chunks[h][lo:hi])
            s_prev = jnp.exp(be_cols[h][lo:hi] + m_s - m_new)
            s_cur = jnp.exp(m_chunks[h][lo:hi] - m_new)
            es[h] = s_prev * es[h] + s_cur * kvs[h][c]
            m_ss[h] = m_new
    for h in heads:
        e_scr[h] = es[h]
        m_scr[:, h:h + 1] = m_ss[h]
        o = jnp.concatenate(outs[h], axis=0)
        og = og_ref[:, h * DV:(h + 1) * DV].astype(F32)
        y_ref[:, h * DV:(h + 1) * DV] = (_sigmoid(og) * _head_rms(o, nw_ref[...])).astype(y_ref.dtype)


def _mixer_b_call(qk, v, og, small, conv_w, pb, norm_w, bsz, seq):
    per_seq = seq // MIX_ROWS
    cw = qk.shape[1]
    row = lambda b, j: (b * per_seq + j, 0)
    const = lambda b, j: (0, 0)
    return pl.pallas_call(
        _mixer_b_kernel,
        grid=(bsz, per_seq),
        in_specs=[
            pl.BlockSpec((MIX_ROWS, cw), row),
            pl.BlockSpec((MIX_ROWS, N_HEADS * DV), row),
            pl.BlockSpec((MIX_ROWS, N_HEADS * DV), row),
            pl.BlockSpec((MIX_ROWS, LANES), row),
            pl.BlockSpec((CONV_K, cw), const),
            pl.BlockSpec((SUBLANES, LANES), const),
            pl.BlockSpec((1, DV), const),
        ],
        out_specs=pl.BlockSpec((MIX_ROWS, N_HEADS * DV), row),
        out_shape=jax.ShapeDtypeStruct((bsz * seq, N_HEADS * DV), BF16),
        scratch_shapes=[
            pltpu.VMEM((MIX_ROWS + 2 * SUBLANES, cw), F32),
            pltpu.VMEM((N_HEADS, DK_B, 2 * DV), F32),
            pltpu.VMEM((CHUNK, LANES), F32),
        ],
        compiler_params=_params(2),
        name="mixer_mlstm",
    )(qk, v, og, small, conv_w, pb, norm_w)


def _mixer_c_kernel(qk_ref, v_ref, z_ref, small_ref, wg_ref, bg_ref, nw_ref, y_ref, st_scr):
    rows = qk_ref.shape[0]
    n_chunks = rows // CHUNK
    hw = N_HEADS * DK_C

    @pl.when(pl.program_id(1) == 0)
    def _():
        st_scr[...] = jnp.zeros(st_scr.shape, F32)

    causal, _ = _chunk_masks(rows)
    log_a = _log_sigmoid(_dot_hi(small_ref[...], wg_ref[...]) + bg_ref[...]) * (1.0 / GLA_TAU)
    bc = _dot_exact_lhs(causal.astype(BF16), log_a)
    parts = []
    for c in range(n_chunks):
        mid = bc[c * CHUNK + CHUNK // 2:c * CHUNK + CHUNK // 2 + 1, :]
        parts.append(jnp.broadcast_to(mid, (CHUNK, hw)))
    rel = bc - jnp.concatenate(parts, axis=0)
    bc_end = _chunk_end_rows(bc, rows)
    qk = qk_ref[...].astype(F32)
    q_all = qk[:, :hw] * (DK_C ** -0.5)
    k_all = qk[:, hw:]
    q_rel = (q_all * jnp.exp(rel)).astype(BF16)
    k_rel = (k_all * jnp.exp(-rel)).astype(BF16)
    k_end = (k_all * jnp.exp(bc_end - bc)).astype(BF16)
    q_abs = (q_all * jnp.exp(bc)).astype(BF16)
    a_end = jnp.exp(bc_end)

    heads = range(N_HEADS)
    sls = [slice(h * DK_C, (h + 1) * DK_C) for h in heads]
    vs = [v_ref[:, h * DV:(h + 1) * DV] for h in heads]
    attns = [jnp.where(causal, _dot(q_rel[:, sls[h]], k_rel[:, sls[h]], NT), 0.0).astype(BF16) for h in heads]
    o_intras = [_dot(attns[h], vs[h]) for h in heads]
    kvs = [[_dot(vs[h][c * CHUNK:(c + 1) * CHUNK], k_end[c * CHUNK:(c + 1) * CHUNK, sls[h]], TN)
            for c in range(n_chunks)] for h in heads]
    sts = [st_scr[h] for h in heads]
    inters = [[] for _ in heads]
    for c in range(n_chunks):
        lo, hi = c * CHUNK, (c + 1) * CHUNK
        for h in heads:
            inters[h].append(_dot(q_abs[lo:hi, sls[h]], sts[h].astype(BF16), NT))
            sts[h] = sts[h] * a_end[lo:lo + 1, sls[h]] + kvs[h][c]
    for h in heads:
        st_scr[h] = sts[h]
        o = jnp.concatenate(inters[h], axis=0) + o_intras[h]
        z = z_ref[:, h * DV:(h + 1) * DV].astype(F32)
        y_ref[:, h * DV:(h + 1) * DV] = (_head_rms(o, nw_ref[...]) * _silu(z)).astype(y_ref.dtype)


def _mixer_c_call(qk, v, z, small, wg, bg, norm_w, bsz, seq):
    per_seq = seq // MIX_ROWS
    cw = qk.shape[1]
    row = lambda b, j: (b * per_seq + j, 0)
    const = lambda b, j: (0, 0)
    return pl.pallas_call(
        _mixer_c_kernel,
        grid=(bsz, per_seq),
        in_specs=[
            pl.BlockSpec((MIX_ROWS, cw), row),
            pl.BlockSpec((MIX_ROWS, N_HEADS * DV), row),
            pl.BlockSpec((MIX_ROWS, N_HEADS * DV), row),
            pl.BlockSpec((MIX_ROWS, LANES), row),
            pl.BlockSpec((LANES, N_HEADS * DK_C), const),
            pl.BlockSpec((1, N_HEADS * DK_C), const),
            pl.BlockSpec((1, DV), const),
        ],
        out_specs=pl.BlockSpec((MIX_ROWS, N_HEADS * DV), row),
        out_shape=jax.ShapeDtypeStruct((bsz * seq, N_HEADS * DV), BF16),
        scratch_shapes=[pltpu.VMEM((N_HEADS, DV, DK_C), F32)],
        compiler_params=_params(2),
        name="mixer_gla",
    )(qk, v, z, small, wg, bg, norm_w)


def _merge_kernel(x_ref, h_ref, ya_ref, yb_ref, yc_ref, mod_ref, wmg_ref, bmg_ref, wbr_ref, wo_ref, o_ref):
    for rows in _row_halves(x_ref.shape[0]):
        h = h_ref[rows, :]
        merged = None
        for g, y_ref in enumerate((ya_ref, yb_ref, yc_ref)):
            gate = _sigmoid(_dot(h, wmg_ref[g]) + bmg_ref[g])
            term = gate * _dot(y_ref[rows, :], wbr_ref[g])
            merged = term if merged is None else merged + term
        o_ref[rows, :] = x_ref[rows, :] + mod_ref[0, 2:3, :] * _bdot(merged, wo_ref[...])


def _merge_call(x, h, ya, yb, yc, mod, w_mg, b_mg, w_br, w_o, seq):
    t, d = x.shape
    tm = _row_tile(seq, 1024)
    per_seq = seq // tm
    bw = ya.shape[1]
    row = lambda i: (i, 0)
    return pl.pallas_call(
        _merge_kernel,
        grid=(t // tm,),
        in_specs=[
            pl.BlockSpec((tm, d), row),
            pl.BlockSpec((tm, d), row),
            pl.BlockSpec((tm, bw), row),
            pl.BlockSpec((tm, bw), row),
            pl.BlockSpec((tm, bw), row),
            pl.BlockSpec((1, 6, d), lambda i: (i // per_seq, 0, 0)),
            pl.BlockSpec((3, d, d), lambda i: (0, 0, 0), **_resident()),
            pl.BlockSpec((3, 1, d), lambda i: (0, 0, 0), **_resident()),
            pl.BlockSpec((3, bw, d), lambda i: (0, 0, 0), **_resident()),
            pl.BlockSpec((d, d), lambda i: (0, 0), **_resident()),
        ],
        out_specs=pl.BlockSpec((tm, d), row),
        out_shape=jax.ShapeDtypeStruct((t, d), F32),
        compiler_params=_params(1),
        name="merge",
    )(x, h, ya, yb, yc, mod, w_mg, b_mg, w_br, w_o)


ROUTE_E1, ROUTE_E2, ROUTE_W1, ROUTE_W2, ROUTE_R1, ROUTE_R2 = range(6)

TOKEN_TILE = SUBLANES * LANES


def _store_token_tiles(ref, x):
    n = x.shape[0]
    for j in range(SUBLANES):
        ref[pl.ds(j, n, stride=SUBLANES), :] = x[:, j * LANES:(j + 1) * LANES]


def _load_token_tiles(ref):
    n = ref.shape[0] // SUBLANES
    return jnp.concatenate([ref[pl.ds(j, n, stride=SUBLANES), :] for j in range(SUBLANES)], axis=1)


def _router_kernel(n_experts, x_ref, mod_ref, g_ref, wr_ref, br_ref, h_ref, route_ref, count_ref, carry):
    @pl.when(pl.program_id(0) == 0)
    def _():
        carry[...] = jnp.zeros(carry.shape, F32)

    h = _norm_mod(x_ref[...], g_ref[...], mod_ref[0, 3:4, :], mod_ref[0, 4:5, :])
    _store_token_tiles(h_ref, h)
    logits = _dot_hi(h, wr_ref[...]) + br_ref[...]
    tm = logits.shape[0]
    lane = lax.broadcasted_iota(jnp.int32, logits.shape, 1)
    logits = jnp.where(lane < n_experts, logits, -jnp.inf)
    m1 = jnp.max(logits, axis=-1, keepdims=True)
    i1 = jnp.min(jnp.where(logits == m1, lane, LANES), axis=-1, keepdims=True)
    rest = jnp.where(lane == i1, -jnp.inf, logits)
    m2 = jnp.max(rest, axis=-1, keepdims=True)
    i2 = jnp.min(jnp.where(rest == m2, lane, LANES), axis=-1, keepdims=True)
    e2 = jnp.exp(m2 - m1)
    w1 = 1.0 / (1.0 + e2)
    w2 = e2 * w1

    sel1 = lane == i1
    sel2 = lane == i2
    onehot = jnp.where(jnp.logical_or(sel1, sel2), 1.0, 0.0)
    r = lax.broadcasted_iota(jnp.int32, (tm, tm), 0)
    c = lax.broadcasted_iota(jnp.int32, (tm, tm), 1)
    before = jnp.where(r > c, 1.0, 0.0).astype(BF16)
    rank = _dot(before, onehot.astype(BF16)) + carry[0:1, :]
    carry[...] = carry[...] + jnp.sum(onehot, axis=0, keepdims=True)
    count_ref[...] = carry[...]
    r1 = jnp.sum(jnp.where(sel1, rank, 0.0), axis=-1, keepdims=True)
    r2 = jnp.sum(jnp.where(sel2, rank, 0.0), axis=-1, keepdims=True)
    rec = jnp.zeros(logits.shape, F32)
    for pos, val in ((ROUTE_E1, i1.astype(F32)), (ROUTE_E2, i2.astype(F32)), (ROUTE_W1, w1), (ROUTE_W2, w2),
                     (ROUTE_R1, r1), (ROUTE_R2, r2)):
        rec = jnp.where(lane == pos, val, rec)
    route_ref[...] = rec


def _router_call(x, mod, g, wr, br, n_experts, seq):
    t, d = x.shape
    tm = _row_tile(seq, 512)
    per_seq = seq // tm
    row = lambda i: (i, 0)
    const = lambda i: (0, 0)
    return pl.pallas_call(
        functools.partial(_router_kernel, n_experts),
        grid=(t // tm,),
        in_specs=[
            pl.BlockSpec((tm, d), row),
            pl.BlockSpec((1, 6, d), lambda i: (i // per_seq, 0, 0)),
            pl.BlockSpec((1, d), const),
            pl.BlockSpec((d, LANES), const),
            pl.BlockSpec((1, LANES), const),
        ],
        out_specs=[pl.BlockSpec((tm * SUBLANES, LANES), row), pl.BlockSpec((tm, LANES), row),
                   pl.BlockSpec((SUBLANES, LANES), const)],
        out_shape=[jax.ShapeDtypeStruct((t * SUBLANES, LANES), F32), jax.ShapeDtypeStruct((t, LANES), F32),
                   jax.ShapeDtypeStruct((SUBLANES, LANES), F32)],
        scratch_shapes=[pltpu.VMEM((SUBLANES, LANES), F32)],
        compiler_params=_params(1),
        name="prenorm_router",
    )(x, mod, g, wr, br)


MOE_ROWS = 1024
COMBINE_ROWS = 512
FFN_DENSE_TILE_CAP = 1536
FFN_MOE_TILE_CAP = 512


def _ffn_tile(d_ff, cap):
    best = None
    for tf in range(LANES, min(cap, d_ff) + 1, LANES):
        if d_ff % tf == 0:
            best = tf
    if best is None:
        raise ValueError(d_ff)
    return best


def _tile_gather_start(row_ref, src_hbm, dst_ref, sem, queues=(0, 1)):
    def issue(i, carry):
        for k in range(SUBLANES):
            r = i * SUBLANES + k
            src_row = pl.multiple_of(row_ref[r], SUBLANES)
            dst_row = pl.multiple_of(r * SUBLANES, SUBLANES)
            pltpu.make_async_copy(src_hbm.at[pl.ds(src_row, SUBLANES)], dst_ref.at[pl.ds(dst_row, SUBLANES)],
                                  sem).start(priority=queues[k % len(queues)])
        return carry

    lax.fori_loop(0, dst_ref.shape[0] // (SUBLANES * SUBLANES), issue, 0)


def _tile_gather_wait(src_hbm, dst_ref, sem):
    pltpu.make_async_copy(src_hbm.at[pl.ds(0, dst_ref.shape[0])], dst_ref, sem).wait()


def _moe_ffn_kernel(te_ref, nu_ref, rows_ref, rows_next_ref, h_hbm, w1_ref, w3_ref, w2_ref, ys_ref,
                    xbuf, sem, hb_scr, acc_ref):
    i = pl.program_id(0)
    f = pl.program_id(1)
    last = pl.num_programs(1) - 1
    n_used = nu_ref[0]
    used = i < n_used
    slot = lax.rem(i, 2)

    @pl.when(jnp.logical_and(f == 0, jnp.logical_and(i == 0, used)))
    def _():
        _tile_gather_start(rows_ref, h_hbm, xbuf.at[0], sem.at[0])

    @pl.when(jnp.logical_and(f == 0, i + 1 < n_used))
    def _():
        _tile_gather_start(rows_next_ref, h_hbm, xbuf.at[1 - slot], sem.at[1 - slot], queues=(1,))

    @pl.when(jnp.logical_and(used, f == 0))
    def _():
        _tile_gather_wait(h_hbm, xbuf.at[slot], sem.at[slot])
        hb_scr[...] = _load_token_tiles(xbuf.at[slot]).astype(BF16)
        acc_ref[...] = jnp.zeros(acc_ref.shape, F32)

    @pl.when(used)
    def _():
        h = hb_scr[...]
        g = _silu(_dot(h, w1_ref[0])) * _dot(h, w3_ref[0])
        acc_ref[...] += _bdot(g, w2_ref[0])

    @pl.when(jnp.logical_and(used, f == last))
    def _():
        _store_token_tiles(ys_ref, acc_ref[...])

    @pl.when(jnp.logical_and(jnp.logical_not(used), f == last))
    def _():
        ys_ref[...] = jnp.zeros(ys_ref.shape, F32)


def _moe_ffn_call(tile_expert, n_used, tok, h_tiles, w1, w3, w2):
    p = tok.shape[0]
    n_tiles = p // MOE_ROWS
    d = TOKEN_TILE
    d_ff = w1.shape[2]
    tf = _ffn_tile(d_ff, FFN_MOE_TILE_CAP)
    rows = tok * SUBLANES
    grid_spec = pltpu.PrefetchScalarGridSpec(
        num_scalar_prefetch=2,
        grid=(n_tiles, d_ff // tf),
        in_specs=[
            pl.BlockSpec((MOE_ROWS,), lambda i, f, te, nu: (i,), memory_space=pltpu.SMEM),
            pl.BlockSpec((MOE_ROWS,), lambda i, f, te, nu: (jnp.minimum(i + 1, n_tiles - 1),),
                         memory_space=pltpu.SMEM),
            pl.BlockSpec(memory_space=pl.ANY),
            pl.BlockSpec((1, d, tf), lambda i, f, te, nu: (te[i], 0, f)),
            pl.BlockSpec((1, d, tf), lambda i, f, te, nu: (te[i], 0, f)),
            pl.BlockSpec((1, tf, d), lambda i, f, te, nu: (te[i], f, 0)),
        ],
        out_specs=pl.BlockSpec((MOE_ROWS * SUBLANES, LANES), lambda i, f, te, nu: (i, 0)),
        scratch_shapes=[
            pltpu.VMEM((2, MOE_ROWS * SUBLANES, LANES), F32),
            pltpu.SemaphoreType.DMA((2,)),
            pltpu.VMEM((MOE_ROWS, d), BF16),
            pltpu.VMEM((MOE_ROWS, d), F32),
        ],
    )
    return pl.pallas_call(
        _moe_ffn_kernel,
        grid_spec=grid_spec,
        out_shape=jax.ShapeDtypeStruct((p * SUBLANES, LANES), F32),
        compiler_params=_params(2),
        name="moe_ffn",
    )(tile_expert, n_used, rows, rows, h_tiles, w1, w3, w2)


def _moe_combine_kernel(final, r1_ref, r2_ref, r1_next_ref, r2_next_ref, ys_hbm, route_ref, x_ref, mod_ref,
                        gf_ref, o_ref, buf, sem):
    i = pl.program_id(0)
    slot = lax.rem(i, 2)

    @pl.when(i == 0)
    def _():
        _tile_gather_start(r1_ref, ys_hbm, buf.at[0, 0], sem.at[0, 0])
        _tile_gather_start(r2_ref, ys_hbm, buf.at[0, 1], sem.at[0, 1])

    @pl.when(i + 1 < pl.num_programs(0))
    def _():
        _tile_gather_start(r1_next_ref, ys_hbm, buf.at[1 - slot, 0], sem.at[1 - slot, 0])
        _tile_gather_start(r2_next_ref, ys_hbm, buf.at[1 - slot, 1], sem.at[1 - slot, 1])

    _tile_gather_wait(ys_hbm, buf.at[slot, 0], sem.at[slot, 0])
    _tile_gather_wait(ys_hbm, buf.at[slot, 1], sem.at[slot, 1])
    route = route_ref[...]
    f = (route[:, ROUTE_W1:ROUTE_W1 + 1] * _load_token_tiles(buf.at[slot, 0])
         + route[:, ROUTE_W2:ROUTE_W2 + 1] * _load_token_tiles(buf.at[slot, 1]))
    y = x_ref[...] + mod_ref[0, 5:6, :] * f
    if final:
        ms = jnp.mean(y * y, axis=-1, keepdims=True)
        y = y * lax.rsqrt(ms + EPS) * gf_ref[...]
    o_ref[...] = y


def _moe_combine_call(d1, d2, ys, route, x, mod, g_final, seq, final):
    t, d = x.shape
    tm = _row_tile(seq, COMBINE_ROWS)
    per_seq = seq // tm
    n_steps = t // tm
    row = lambda i: (i, 0)
    idx_spec = pl.BlockSpec((tm,), lambda i: (i,), memory_space=pltpu.SMEM)
    next_spec = pl.BlockSpec((tm,), lambda i: (jnp.minimum(i + 1, n_steps - 1),), memory_space=pltpu.SMEM)
    rows1, rows2 = d1 * SUBLANES, d2 * SUBLANES
    return pl.pallas_call(
        functools.partial(_moe_combine_kernel, final),
        grid=(n_steps,),
        in_specs=[
            idx_spec, idx_spec, next_spec, next_spec,
            pl.BlockSpec(memory_space=pl.ANY),
            pl.BlockSpec((tm, LANES), row),
            pl.BlockSpec((tm, d), row),
            pl.BlockSpec((1, 6, d), lambda i: (i // per_seq, 0, 0)),
            pl.BlockSpec((1, d), lambda i: (0, 0)),
        ],
        out_specs=pl.BlockSpec((tm, d), row),
        out_shape=jax.ShapeDtypeStruct((t, d), F32),
        scratch_shapes=[pltpu.VMEM((2, 2, tm * SUBLANES, LANES), F32), pltpu.SemaphoreType.DMA((2, 2))],
        compiler_params=_params(1),
        name="moe_combine",
    )(rows1, rows2, rows1, rows2, ys, route, x, mod, g_final)


def _moe_layout(route, counts, n_experts, t):
    cnt = counts[0, :n_experts].astype(jnp.int32)
    padded = ((cnt + MOE_ROWS - 1) // MOE_ROWS) * MOE_ROWS
    ends = jnp.cumsum(padded)
    offs = ends - padded
    e1 = route[:, ROUTE_E1].astype(jnp.int32)
    e2 = route[:, ROUTE_E2].astype(jnp.int32)
    d1 = offs[e1] + route[:, ROUTE_R1].astype(jnp.int32)
    d2 = offs[e2] + route[:, ROUTE_R2].astype(jnp.int32)
    p = (TOP_K * t // MOE_ROWS + n_experts) * MOE_ROWS
    ids = jnp.arange(t, dtype=jnp.int32)
    tok = jnp.zeros((p,), jnp.int32).at[jnp.concatenate([d1, d2])].set(jnp.concatenate([ids, ids]))
    starts = jnp.arange(p // MOE_ROWS, dtype=jnp.int32) * MOE_ROWS
    tile_expert = jnp.minimum(jnp.searchsorted(ends, starts, side="right"), n_experts - 1).astype(jnp.int32)
    n_used = (ends[-1:] // MOE_ROWS).astype(jnp.int32)
    return d1, d2, tok, tile_expert, n_used


def _ffn_kernel(x_ref, mod_ref, g_ref, w1_ref, w3_ref, w2_ref, o_ref):
    x = x_ref[...]
    h = _norm_mod(x, g_ref[...], mod_ref[0, 3:4, :], mod_ref[0, 4:5, :]).astype(BF16)
    d_ff = w1_ref.shape[1]
    tf = _ffn_tile(d_ff, FFN_DENSE_TILE_CAP)
    acc = None
    for f0 in range(0, d_ff, tf):
        g = _silu(_dot(h, w1_ref[:, f0:f0 + tf])) * _dot(h, w3_ref[:, f0:f0 + tf])
        part = _bdot(g, w2_ref[f0:f0 + tf, :])
        acc = part if acc is None else acc + part
    o_ref[...] = x + mod_ref[0, 5:6, :] * acc


def _ffn_call(x, mod, g, w1, w3, w2, seq):
    t, d = x.shape
    d_ff = w1.shape[1]
    tm = _row_tile(seq, 512)
    per_seq = seq // tm
    row = lambda i: (i, 0)
    const = lambda i: (0, 0)
    return pl.pallas_call(
        _ffn_kernel,
        grid=(t // tm,),
        in_specs=[
            pl.BlockSpec((tm, d), row),
            pl.BlockSpec((1, 6, d), lambda i: (i // per_seq, 0, 0)),
            pl.BlockSpec((1, d), const),
            pl.BlockSpec((d, d_ff), const, **_resident()),
            pl.BlockSpec((d, d_ff), const, **_resident()),
            pl.BlockSpec((d_ff, d), const, **_resident()),
        ],
        out_specs=pl.BlockSpec((tm, d), row),
        out_shape=jax.ShapeDtypeStruct((t, d), F32),
        compiler_params=_params(1),
        name="ffn_dense",
    )(x, mod, g, w1, w3, w2)


def _final_kernel(x_ref, g_ref, o_ref):
    x = x_ref[...]
    ms = jnp.mean(x * x, axis=-1, keepdims=True)
    o_ref[...] = x * lax.rsqrt(ms + EPS) * g_ref[...]


def _final_call(x, g):
    t, d = x.shape
    tm = _row_tile(t, 512)
    return pl.pallas_call(
        _final_kernel,
        grid=(t // tm,),
        in_specs=[pl.BlockSpec((tm, d), lambda i: (i, 0)), pl.BlockSpec((1, d), lambda i: (0, 0))],
        out_specs=pl.BlockSpec((tm, d), lambda i: (i, 0)),
        out_shape=jax.ShapeDtypeStruct((t, d), F32),
        compiler_params=_params(1),
        name="final_norm",
    )(x, g)


def _lane_row(values, lane0, n_rows=1, row=0):
    out = jnp.zeros((n_rows, LANES), F32)
    return out.at[row, lane0:lane0 + values.shape[0]].set(values.astype(F32))


def _split_w_in(w_in):
    qk_a, v_a = N_HEADS * DK_A, N_HEADS * DV
    qk_b, qk_c, vw = N_HEADS * DK_B, N_HEADS * DK_C, N_HEADS * DV
    sizes = (qk_a, qk_a, v_a, v_a, N_HEADS, N_HEADS,
             qk_b, qk_b, vw, vw, N_HEADS, N_HEADS,
             qk_c, qk_c, vw, vw, GLA_RANK)
    names = ("q_a", "k_a", "v_a", "z_a", "a_a", "b_a", "q_b", "k_b", "v_b", "o_b", "i_b", "f_b",
             "q_c", "k_c", "v_c", "z_c", "g_c")
    assert sum(sizes) == w_in.shape[1]
    cols, off = {}, 0
    for name, size in zip(names, sizes):
        cols[name] = w_in[:, off:off + size]
        off += size
    big = jnp.concatenate([cols[n] for n in ("q_a", "k_a", "v_a", "z_a", "q_b", "k_b", "v_b", "o_b",
                                             "q_c", "k_c", "v_c", "z_c")], axis=1).astype(BF16)
    small = jnp.concatenate([cols[n] for n in ("a_a", "b_a", "i_b", "f_b", "g_c")], axis=1)
    small = jnp.pad(small, ((0, 0), (0, LANES - small.shape[1])))
    s_hi = small.astype(BF16)
    s_lo = (small - s_hi.astype(F32)).astype(BF16)
    return big, s_hi, s_lo


def kernel(x, c, w_ada, b_ada, g_mix, g_ffn, g_final, w_in, conv_a, a_log, dt_bias, norm_a, conv_b, b_i, b_f,
           norm_b, w_gla2, b_gla, norm_c, w_br, w_mg, b_mg, w_o, w1_d, w3_d, w2_d, w_router, b_router,
           w1_e, w3_e, w2_e):
    bsz, seq, d = x.shape
    depth = w_ada.shape[0]
    assert seq % MIX_ROWS == 0 and d % LANES == 0
    t = bsz * seq
    xf = x.reshape(t, d)
    mod_all = _ada_call(c, w_ada, b_ada).reshape(depth, bsz, 6, d)

    for layer in range(depth):
        mod = mod_all[layer]
        w_big, ws_hi, ws_lo = _split_w_in(w_in[layer])
        (h, qkv_a, z_a, qk_b, v_b, o_b, qk_c, v_c, z_c, small) = _inproj_call(
            xf, mod, g_mix[layer].reshape(1, d), w_big, ws_hi, ws_lo, seq)

        pa = (_lane_row(-jnp.exp(a_log[layer]), LANE_A_DECAY, SUBLANES, 0)
              + _lane_row(dt_bias[layer], LANE_A_DECAY, SUBLANES, 1))
        y_a = _mixer_a_call(qkv_a, z_a, small, conv_a[layer], pa, norm_a[layer].reshape(1, DV), bsz, seq)

        pb = (_lane_row(b_i[layer], LANE_B_IN, SUBLANES, 0) + _lane_row(b_f[layer], LANE_B_FORGET, SUBLANES, 0))
        y_b = _mixer_b_call(qk_b, v_b, o_b, small, conv_b[layer], pb, norm_b[layer].reshape(1, DV), bsz, seq)

        wg = jnp.zeros((LANES, N_HEADS * DK_C), F32).at[LANE_C_GATE:LANE_C_GATE + GLA_RANK].set(w_gla2[layer])
        y_c = _mixer_c_call(qk_c, v_c, z_c, small, wg, b_gla[layer].reshape(1, -1),
                            norm_c[layer].reshape(1, DV), bsz, seq)

        xf = _merge_call(xf, h, y_a, y_b, y_c, mod, w_mg[layer].astype(BF16), b_mg[layer][:, None, :],
                         w_br[layer].astype(BF16), w_o[layer].astype(BF16), seq)

        j = layer // 2
        last = layer == depth - 1
        if layer % 2 == 0:
            xf = _ffn_call(xf, mod, g_ffn[layer].reshape(1, d), w1_d[j].astype(BF16), w3_d[j].astype(BF16),
                           w2_d[j].astype(BF16), seq)
            if last:
                xf = _final_call(xf, g_final.reshape(1, d))
        else:
            n_experts = w_router.shape[2]
            wr = jnp.pad(w_router[j], ((0, 0), (0, LANES - n_experts)))
            br = jnp.pad(b_router[j], (0, LANES - n_experts)).reshape(1, LANES)
            assert d == TOKEN_TILE, "gathered rows are stored as one (SUBLANES, LANES) tile per token"
            h2, route, counts = _router_call(xf, mod, g_ffn[layer].reshape(1, d), wr, br, n_experts, seq)
            d1, d2, tok, tile_expert, n_used = _moe_layout(route, counts, n_experts, t)
            ys = _moe_ffn_call(tile_expert, n_used, tok, h2, w1_e[j].astype(BF16), w3_e[j].astype(BF16),
                               w2_e[j].astype(BF16))
            xf = _moe_combine_call(d1, d2, ys, route, xf, mod, g_final.reshape(1, d), seq, last)

    return xf.reshape(bsz, seq, d)
```

```python
import functools

import jax
import jax.numpy as jnp
from jax import lax
from jax.experimental import pallas as pl
from jax.experimental.pallas import tpu as pltpu

F32 = jnp.float32
BF16 = jnp.bfloat16

EPS = 1e-6
CHUNK = 64
CONV_K = 4
N_HEADS = 4
DV = 128
DK_A = 128
DK_B = 64
DK_C = 64
GLA_RANK = 16
GLA_TAU = 16.0
TOP_K = 2

V7X_MXU_DIM = 256
MIX_ROWS = V7X_MXU_DIM
LANES = 128
SUBLANES = 8
V7X_VMEM_LIMIT = 56 * 1024 * 1024

LANE_A_DECAY = 0
LANE_A_BETA = 4
LANE_B_IN = 8
LANE_B_FORGET = 12
LANE_C_GATE = 16

NN = (((1,), (0,)), ((), ()))
NT = (((1,), (1,)), ((), ()))
TN = (((0,), (0,)), ((), ()))


def _dot(a, b, dims=NN):
    return lax.dot_general(a, b, dims, preferred_element_type=F32)


def _bdot(a, b, dims=NN):
    return _dot(a.astype(BF16), b.astype(BF16), dims)


def _split3(x):
    hi = x.astype(BF16)
    r1 = x - hi.astype(F32)
    mid = r1.astype(BF16)
    lo = (r1 - mid.astype(F32)).astype(BF16)
    return hi, mid, lo


def _dot_exact_lhs(a, b, dims=NN):
    hi, mid, lo = _split3(b)
    return _dot(a, hi, dims) + _dot(a, mid, dims) + _dot(a, lo, dims)


def _dot_hi(a, b, dims=NN):
    a_hi = a.astype(BF16)
    a_lo = (a - a_hi.astype(F32)).astype(BF16)
    b_hi = b.astype(BF16)
    b_lo = (b - b_hi.astype(F32)).astype(BF16)
    return _dot(a_hi, b_hi, dims) + _dot(a_hi, b_lo, dims) + _dot(a_lo, b_hi, dims)


def _sigmoid(x):
    return 1.0 / (1.0 + jnp.exp(-x))


def _silu(x):
    return x * _sigmoid(x)


def _softplus(x):
    return jnp.maximum(x, 0.0) + jnp.log(1.0 + jnp.exp(-jnp.abs(x)))


def _log_sigmoid(x):
    return jnp.minimum(x, 0.0) - jnp.log(1.0 + jnp.exp(-jnp.abs(x)))


def _params(n_axes):
    return pltpu.CompilerParams(
        dimension_semantics=("arbitrary",) * n_axes,
        vmem_limit_bytes=V7X_VMEM_LIMIT,
    )


def _row_tile(t, want):
    tm = min(want, t)
    assert t % tm == 0, (t, tm)
    return tm


ROW_SPLIT = 2


def _row_halves(tm):
    half = tm // ROW_SPLIT
    return [slice(s * half, (s + 1) * half) for s in range(ROW_SPLIT)]


def _resident():
    return dict(pipeline_mode=pl.Buffered(1))


def _ada_kernel(c_ref, w_ref, b_ref, o_ref):
    c = c_ref[...]
    o_ref[0] = _bdot(_silu(c), w_ref[0]) + b_ref[0]


def _ada_call(c, w_ada, b_ada):
    depth, d, d6 = w_ada.shape
    bsz = c.shape[0]
    n_col = d6 // d
    return pl.pallas_call(
        _ada_kernel,
        grid=(depth, n_col),
        in_specs=[
            pl.BlockSpec((bsz, d), lambda l, n: (0, 0)),
            pl.BlockSpec((1, d, d), lambda l, n: (l, 0, n)),
            pl.BlockSpec((1, 1, d), lambda l, n: (l, 0, n)),
        ],
        out_specs=pl.BlockSpec((1, bsz, d), lambda l, n: (l, 0, n)),
        out_shape=jax.ShapeDtypeStruct((depth, bsz, d6), F32),
        compiler_params=_params(2),
        name="ada_mod",
    )(c, w_ada, b_ada.reshape(depth, 1, d6))


def _norm_mod(x, g, shift, scale):
    ms = jnp.mean(x * x, axis=-1, keepdims=True)
    return (x * lax.rsqrt(ms + EPS) * g) * (1.0 + scale) + shift


IN_WIDTHS = (1536, 512, 512, 512, 512, 512, 512, 512)


def _inproj_kernel(x_ref, mod_ref, g_ref, w_ref, wsh_ref, wsl_ref, h_ref, *out_refs):
    halves = _row_halves(x_ref.shape[0])
    hs = []
    for rows in halves:
        h = _norm_mod(x_ref[rows, :], g_ref[...], mod_ref[0, 0:1, :], mod_ref[0, 1:2, :])
        hb = h.astype(BF16)
        h_ref[rows, :] = hb.astype(h_ref.dtype)
        hs.append((h, hb))
    for rows, (h, hb) in zip(halves, hs):
        off = 0
        for o_ref, width in zip(out_refs[:-1], IN_WIDTHS):
            o_ref[rows, :] = _dot(hb, w_ref[:, off:off + width]).astype(o_ref.dtype)
            off += width
        hl = (h - hb.astype(F32)).astype(BF16)
        wsh = wsh_ref[...]
        out_refs[-1][rows, :] = _dot(hb, wsh) + _dot(hl, wsh) + _dot(hb, wsl_ref[...])


def _inproj_call(x, mod, g, w_big, ws_hi, ws_lo, seq):
    t, d = x.shape
    tm = _row_tile(seq, 1024)
    per_seq = seq // tm
    n_big = w_big.shape[1]
    row = lambda i: (i, 0)
    const = lambda i: (0, 0)
    out_shapes = [jax.ShapeDtypeStruct((t, d), BF16)]
    out_specs = [pl.BlockSpec((tm, d), row)]
    for width in IN_WIDTHS:
        out_shapes.append(jax.ShapeDtypeStruct((t, width), BF16))
        out_specs.append(pl.BlockSpec((tm, width), row))
    out_shapes.append(jax.ShapeDtypeStruct((t, LANES), F32))
    out_specs.append(pl.BlockSpec((tm, LANES), row))
    return pl.pallas_call(
        _inproj_kernel,
        grid=(t // tm,),
        in_specs=[
            pl.BlockSpec((tm, d), row),
            pl.BlockSpec((1, 6, d), lambda i: (i // per_seq, 0, 0)),
            pl.BlockSpec((1, d), const),
            pl.BlockSpec((d, n_big), const, **_resident()),
            pl.BlockSpec((d, LANES), const, **_resident()),
            pl.BlockSpec((d, LANES), const, **_resident()),
        ],
        out_specs=out_specs,
        out_shape=out_shapes,
        compiler_params=_params(1),
        name="in_proj",
    )(x, mod, g, w_big, ws_hi, ws_lo)


def _conv_silu(x_ref, cw_ref, cbuf):
    rows = x_ref.shape[0]
    x = x_ref[...].astype(F32)
    cbuf[SUBLANES:SUBLANES + rows, :] = x
    acc = x * cw_ref[CONV_K - 1:CONV_K, :]
    for s in range(1, CONV_K):
        acc = acc + cbuf[SUBLANES - s:SUBLANES - s + rows, :] * cw_ref[CONV_K - 1 - s:CONV_K - s, :]
    cbuf[0:SUBLANES, :] = cbuf[rows:rows + SUBLANES, :]
    return _silu(acc)


def _chunk_masks(rows):
    r = lax.broadcasted_iota(jnp.int32, (rows, rows), 0)
    c = lax.broadcasted_iota(jnp.int32, (rows, rows), 1)
    same = jnp.right_shift(r, 6) == jnp.right_shift(c, 6)
    causal = jnp.logical_and(same, r >= c)
    strict = jnp.logical_and(same, r > c)
    return causal, strict


def _chunk_end_rows(x, rows):
    parts = []
    for c in range(rows // CHUNK):
        last = x[c * CHUNK + CHUNK - 1:c * CHUNK + CHUNK, :]
        parts.append(jnp.broadcast_to(last, (CHUNK, x.shape[1])))
    return jnp.concatenate(parts, axis=0)


def _head_rms(o, w_row):
    ms = jnp.mean(o * o, axis=-1, keepdims=True)
    return o * lax.rsqrt(ms + EPS) * w_row


def _mixer_a_kernel(qkv_ref, z_ref, small_ref, cw_ref, pa_ref, nw_ref, y_ref, cbuf, s_scr):
    rows = qkv_ref.shape[0]
    n_chunks = rows // CHUNK
    hw = N_HEADS * DK_A

    @pl.when(pl.program_id(1) == 0)
    def _():
        cbuf[0:SUBLANES, :] = jnp.zeros((SUBLANES, cbuf.shape[1]), F32)
        s_scr[...] = jnp.zeros(s_scr.shape, F32)

    qkv = _conv_silu(qkv_ref, cw_ref, cbuf)
    causal, strict = _chunk_masks(rows)

    gates = small_ref[...]
    log_decay = pa_ref[0:1, :] * _softplus(gates + pa_ref[1:2, :])
    beta_all = _sigmoid(gates)
    gc = _dot_exact_lhs(causal.astype(BF16), log_decay)
    gc_t = gc.T
    gc_end = _chunk_end_rows(gc, rows)

    heads = range(N_HEADS)
    a_mats, rhs, attns, qgs, k_ends, s_decays = [], [], [], [], [], []
    for h in heads:
        q = qkv[:, h * DK_A:(h + 1) * DK_A]
        k = qkv[:, hw + h * DK_A:hw + (h + 1) * DK_A]
        v = qkv[:, 2 * hw + h * DV:2 * hw + (h + 1) * DV]
        q = q * (lax.rsqrt(jnp.sum(q * q, axis=-1, keepdims=True) + 1e-6) * (DK_A ** -0.5))
        k = k * lax.rsqrt(jnp.sum(k * k, axis=-1, keepdims=True) + 1e-6)
        beta = beta_all[:, LANE_A_BETA + h:LANE_A_BETA + h + 1]
        g_col = gc[:, LANE_A_DECAY + h:LANE_A_DECAY + h + 1]
        g_row = gc_t[LANE_A_DECAY + h:LANE_A_DECAY + h + 1, :]
        ge_col = gc_end[:, LANE_A_DECAY + h:LANE_A_DECAY + h + 1]
        decay = jnp.exp(jnp.where(causal, g_col - g_row, 0.0))
        kb = k * beta
        k16 = k.astype(BF16)
        a_mats.append(jnp.where(strict, _dot(kb.astype(BF16), k16, NT) * decay, 0.0).astype(BF16))
        eg = jnp.exp(g_col)
        rhs.append(jnp.concatenate([v * beta, kb * eg], axis=1))
        attns.append(jnp.where(causal, _dot(q.astype(BF16), k16, NT) * decay, 0.0).astype(BF16))
        qgs.append((q * eg).astype(BF16))
        k_ends.append((k * jnp.exp(ge_col - g_col)).astype(BF16))
        s_decays.append(jnp.exp(ge_col))

    xs = [rhs[h] - _dot(a_mats[h], rhs[h].astype(BF16)) for h in heads]
    ps = [_dot(a_mats[h], a_mats[h]).astype(BF16) for h in heads]
    for it in range(5):
        xs = [xs[h] + _dot(ps[h], xs[h].astype(BF16)) for h in heads]
        if it < 4:
            ps = [_dot(ps[h], ps[h]).astype(BF16) for h in heads]
    us = [xs[h][:, :DV] for h in heads]
    ws = [xs[h][:, DV:].astype(BF16) for h in heads]

    states = [s_scr[h] for h in heads]
    deltas = [[] for _ in heads]
    inters = [[] for _ in heads]
    for c in range(n_chunks):
        lo, hi = c * CHUNK, (c + 1) * CHUNK
        for h in heads:
            s16 = states[h].astype(BF16)
            delta = us[h][lo:hi] - _dot(ws[h][lo:hi], s16)
            inters[h].append(_dot(qgs[h][lo:hi], s16))
            sd = jnp.concatenate([s_decays[h][lo:hi]] * (DK_A // CHUNK), axis=0)
            states[h] = sd * states[h] + _dot(k_ends[h][lo:hi], delta.astype(BF16), TN)
            deltas[h].append(delta.astype(BF16))
    for h in heads:
        s_scr[h] = states[h]
        o = jnp.concatenate(inters[h], axis=0) + _dot(attns[h], jnp.concatenate(deltas[h], axis=0))
        z = z_ref[:, h * DV:(h + 1) * DV].astype(F32)
        y_ref[:, h * DV:(h + 1) * DV] = (_head_rms(o, nw_ref[...]) * _silu(z)).astype(y_ref.dtype)


def _mixer_a_call(qkv, z, small, conv_w, pa, norm_w, bsz, seq):
    per_seq = seq // MIX_ROWS
    cw = qkv.shape[1]
    row = lambda b, j: (b * per_seq + j, 0)
    const = lambda b, j: (0, 0)
    return pl.pallas_call(
        _mixer_a_kernel,
        grid=(bsz, per_seq),
        in_specs=[
            pl.BlockSpec((MIX_ROWS, cw), row),
            pl.BlockSpec((MIX_ROWS, N_HEADS * DV), row),
            pl.BlockSpec((MIX_ROWS, LANES), row),
            pl.BlockSpec((CONV_K, cw), const),
            pl.BlockSpec((SUBLANES, LANES), const),
            pl.BlockSpec((1, DV), const),
        ],
        out_specs=pl.BlockSpec((MIX_ROWS, N_HEADS * DV), row),
        out_shape=jax.ShapeDtypeStruct((bsz * seq, N_HEADS * DV), BF16),
        scratch_shapes=[
            pltpu.VMEM((MIX_ROWS + 2 * SUBLANES, cw), F32),
            pltpu.VMEM((N_HEADS, DK_A, DV), F32),
        ],
        compiler_params=_params(2),
        name="mixer_deltanet",
    )(qkv, z, small, conv_w, pa, norm_w)


def _mixer_b_kernel(qk_ref, v_ref, og_ref, small_ref, cw_ref, pb_ref, nw_ref, y_ref, cbuf, e_scr, m_scr):
    rows = qk_ref.shape[0]
    n_chunks = rows // CHUNK
    hw = N_HEADS * DK_B

    @pl.when(pl.program_id(1) == 0)
    def _():
        cbuf[0:SUBLANES, :] = jnp.zeros((SUBLANES, cbuf.shape[1]), F32)
        e_scr[...] = jnp.zeros(e_scr.shape, F32)
        m_scr[...] = jnp.zeros(m_scr.shape, F32)

    qk = _conv_silu(qk_ref, cw_ref, cbuf)
    causal, _ = _chunk_masks(rows)

    pre = small_ref[...] + pb_ref[0:1, :]
    log_f = _log_sigmoid(pre)
    bcum = _dot_exact_lhs(causal.astype(BF16), log_f)
    bcum_t = bcum.T
    pre_t = pre.T
    bend = _chunk_end_rows(bcum, rows)
    ones = jnp.ones((rows, DV), BF16)

    heads = range(N_HEADS)
    qs, v_exts, b_cols, be_cols, m_intras, m_chunks, wts, k_ws = [], [], [], [], [], [], [], []
    for h in heads:
        q = qk[:, h * DK_B:(h + 1) * DK_B].astype(BF16)
        k = qk[:, hw + h * DK_B:hw + (h + 1) * DK_B] * (DK_B ** -0.5)
        lf = LANE_B_FORGET + h
        li = LANE_B_IN + h
        b_col = bcum[:, lf:lf + 1]
        b_row = bcum_t[lf:lf + 1, :]
        i_row = pre_t[li:li + 1, :]
        i_col = pre[:, li:li + 1]
        be_col = bend[:, lf:lf + 1]
        dlog = jnp.where(causal, b_col - b_row + i_row, -jnp.inf)
        m_intra = jnp.max(dlog, axis=-1, keepdims=True)
        m_chunk = _chunk_end_rows(m_intra, rows)
        wts.append((jnp.exp(dlog - m_intra) * _dot(q, k.astype(BF16), NT)).astype(BF16))
        k_ws.append((k * jnp.exp(be_col - b_col + i_col - m_chunk)).astype(BF16))
        qs.append(q)
        v_exts.append(jnp.concatenate([v_ref[:, h * DV:(h + 1) * DV], ones], axis=1))
        b_cols.append(b_col)
        be_cols.append(be_col)
        m_intras.append(m_intra)
        m_chunks.append(m_chunk)

    intras = [_dot(wts[h], v_exts[h]) for h in heads]
    kvs = [[_dot(k_ws[h][c * CHUNK:(c + 1) * CHUNK], v_exts[h][c * CHUNK:(c + 1) * CHUNK], TN)
            for c in range(n_chunks)] for h in heads]

    es = [e_scr[h] for h in heads]
    m_ss = [m_scr[:, h:h + 1] for h in heads]
    outs = [[] for _ in heads]
    for c in range(n_chunks):
        lo, hi = c * CHUNK, (c + 1) * CHUNK
        for h in heads:
            m_s = m_ss[h]
            m_tot = jnp.maximum(b_cols[h][lo:hi] + m_s, m_intras[h][lo:hi])
            s_inter = jnp.exp(b_cols[h][lo:hi] + m_s - m_tot)
            s_intra = jnp.exp(m_intras[h][lo:hi] - m_tot)
            both = s_inter * _dot(qs[h][lo:hi], es[h].astype(BF16)) + s_intra * intras[h][lo:hi]
            den = jnp.maximum(jnp.abs(both[:, DV:]), jnp.exp(-m_tot))
            outs[h].append(both[:, :DV] / den)
            m_new = jnp.maximum(be_cols[h][lo:hi] + m_s, m_chunks[h][lo:hi])
            s_prev = jnp.exp(be_cols[h][lo:hi] + m_s - m_new)
            s_cur = jnp.exp(m_chunks[h][lo:hi] - m_new)
            es[h] = s_prev * es[h] + s_cur * kvs[h][c]
            m_ss[h] = m_new
    for h in heads:
        e_scr[h] = es[h]
        m_scr[:, h:h + 1] = m_ss[h]
        o = jnp.concatenate(outs[h], axis=0)
        og = og_ref[:, h * DV:(h + 1) * DV].astype(F32)
        y_ref[:, h * DV:(h + 1) * DV] = (_sigmoid(og) * _head_rms(o, nw_ref[...])).astype(y_ref.dtype)


def _mixer_b_call(qk, v, og, small, conv_w, pb, norm_w, bsz, seq):
    per_seq = seq // MIX_ROWS
    cw = qk.shape[1]
    row = lambda b, j: (b * per_seq + j, 0)
    const = lambda b, j: (0, 0)
    return pl.pallas_call(
        _mixer_b_kernel,
        grid=(bsz, per_seq),
        in_specs=[
            pl.BlockSpec((MIX_ROWS, cw), row),
            pl.BlockSpec((MIX_ROWS, N_HEADS * DV), row),
            pl.BlockSpec((MIX_ROWS, N_HEADS * DV), row),
            pl.BlockSpec((MIX_ROWS, LANES), row),
            pl.BlockSpec((CONV_K, cw), const),
            pl.BlockSpec((SUBLANES, LANES), const),
            pl.BlockSpec((1, DV), const),
        ],
        out_specs=pl.BlockSpec((MIX_ROWS, N_HEADS * DV), row),
        out_shape=jax.ShapeDtypeStruct((bsz * seq, N_HEADS * DV), BF16),
        scratch_shapes=[
            pltpu.VMEM((MIX_ROWS + 2 * SUBLANES, cw), F32),
            pltpu.VMEM((N_HEADS, DK_B, 2 * DV), F32),
            pltpu.VMEM((CHUNK, LANES), F32),
        ],
        compiler_params=_params(2),
        name="mixer_mlstm",
    )(qk, v, og, small, conv_w, pb, norm_w)


def _mixer_c_kernel(qk_ref, v_ref, z_ref, small_ref, wg_ref, bg_ref, nw_ref, y_ref, st_scr):
    rows = qk_ref.shape[0]
    n_chunks = rows // CHUNK
    hw = N_HEADS * DK_C

    @pl.when(pl.program_id(1) == 0)
    def _():
        st_scr[...] = jnp.zeros(st_scr.shape, F32)

    causal, _ = _chunk_masks(rows)
    log_a = _log_sigmoid(_dot_hi(small_ref[...], wg_ref[...]) + bg_ref[...]) * (1.0 / GLA_TAU)
    bc = _dot_exact_lhs(causal.astype(BF16), log_a)
    parts = []
    for c in range(n_chunks):
        mid = bc[c * CHUNK + CHUNK // 2:c * CHUNK + CHUNK // 2 + 1, :]
        parts.append(jnp.broadcast_to(mid, (CHUNK, hw)))
    rel = bc - jnp.concatenate(parts, axis=0)
    bc_end = _chunk_end_rows(bc, rows)
    qk = qk_ref[...].astype(F32)
    q_all = qk[:, :hw] * (DK_C ** -0.5)
    k_all = qk[:, hw:]
    q_rel = (q_all * jnp.exp(rel)).astype(BF16)
    k_rel = (k_all * jnp.exp(-rel)).astype(BF16)
    k_end = (k_all * jnp.exp(bc_end - bc)).astype(BF16)
    q_abs = (q_all * jnp.exp(bc)).astype(BF16)
    a_end = jnp.exp(bc_end)

    heads = range(N_HEADS)
    sls = [slice(h * DK_C, (h + 1) * DK_C) for h in heads]
    vs = [v_ref[:, h * DV:(h + 1) * DV] for h in heads]
    attns = [jnp.where(causal, _dot(q_rel[:, sls[h]], k_rel[:, sls[h]], NT), 0.0).astype(BF16) for h in heads]
    o_intras = [_dot(attns[h], vs[h]) for h in heads]
    kvs = [[_dot(vs[h][c * CHUNK:(c + 1) * CHUNK], k_end[c * CHUNK:(c + 1) * CHUNK, sls[h]], TN)
            for c in range(n_chunks)] for h in heads]
    sts = [st_scr[h] for h in heads]
    inters = [[] for _ in heads]
    for c in range(n_chunks):
        lo, hi = c * CHUNK, (c + 1) * CHUNK
        for h in heads:
            inters[h].append(_dot(q_abs[lo:hi, sls[h]], sts[h].astype(BF16), NT))
            sts[h] = sts[h] * a_end[lo:lo + 1, sls[h]] + kvs[h][c]
    for h in heads:
        st_scr[h] = sts[h]
        o = jnp.concatenate(inters[h], axis=0) + o_intras[h]
        z = z_ref[:, h * DV:(h + 1) * DV].astype(F32)
        y_ref[:, h * DV:(h + 1) * DV] = (_head_rms(o, nw_ref[...]) * _silu(z)).astype(y_ref.dtype)


def _mixer_c_call(qk, v, z, small, wg, bg, norm_w, bsz, seq):
    per_seq = seq // MIX_ROWS
    cw = qk.shape[1]
    row = lambda b, j: (b * per_seq + j, 0)
    const = lambda b, j: (0, 0)
    return pl.pallas_call(
        _mixer_c_kernel,
        grid=(bsz, per_seq),
        in_specs=[
            pl.BlockSpec((MIX_ROWS, cw), row),
            pl.BlockSpec((MIX_ROWS, N_HEADS * DV), row),
            pl.BlockSpec((MIX_ROWS, N_HEADS * DV), row),
            pl.BlockSpec((MIX_ROWS, LANES), row),
            pl.BlockSpec((LANES, N_HEADS * DK_C), const),
            pl.BlockSpec((1, N_HEADS * DK_C), const),
            pl.BlockSpec((1, DV), const),
        ],
        out_specs=pl.BlockSpec((MIX_ROWS, N_HEADS * DV), row),
        out_shape=jax.ShapeDtypeStruct((bsz * seq, N_HEADS * DV), BF16),
        scratch_shapes=[pltpu.VMEM((N_HEADS, DV, DK_C), F32)],
        compiler_params=_params(2),
        name="mixer_gla",
    )(qk, v, z, small, wg, bg, norm_w)


def _merge_kernel(x_ref, h_ref, ya_ref, yb_ref, yc_ref, mod_ref, wmg_ref, bmg_ref, wbr_ref, wo_ref, o_ref):
    for rows in _row_halves(x_ref.shape[0]):
        h = h_ref[rows, :]
        merged = None
        for g, y_ref in enumerate((ya_ref, yb_ref, yc_ref)):
            gate = _sigmoid(_dot(h, wmg_ref[g]) + bmg_ref[g])
            term = gate * _dot(y_ref[rows, :], wbr_ref[g])
            merged = term if merged is None else merged + term
        o_ref[rows, :] = x_ref[rows, :] + mod_ref[0, 2:3, :] * _bdot(merged, wo_ref[...])


def _merge_call(x, h, ya, yb, yc, mod, w_mg, b_mg, w_br, w_o, seq):
    t, d = x.shape
    tm = _row_tile(seq, 1024)
    per_seq = seq // tm
    bw = ya.shape[1]
    row = lambda i: (i, 0)
    return pl.pallas_call(
        _merge_kernel,
        grid=(t // tm,),
        in_specs=[
            pl.BlockSpec((tm, d), row),
            pl.BlockSpec((tm, d), row),
            pl.BlockSpec((tm, bw), row),
            pl.BlockSpec((tm, bw), row),
            pl.BlockSpec((tm, bw), row),
            pl.BlockSpec((1, 6, d), lambda i: (i // per_seq, 0, 0)),
            pl.BlockSpec((3, d, d), lambda i: (0, 0, 0), **_resident()),
            pl.BlockSpec((3, 1, d), lambda i: (0, 0, 0), **_resident()),
            pl.BlockSpec((3, bw, d), lambda i: (0, 0, 0), **_resident()),
            pl.BlockSpec((d, d), lambda i: (0, 0), **_resident()),
        ],
        out_specs=pl.BlockSpec((tm, d), row),
        out_shape=jax.ShapeDtypeStruct((t, d), F32),
        compiler_params=_params(1),
        name="merge",
    )(x, h, ya, yb, yc, mod, w_mg, b_mg, w_br, w_o)


ROUTE_E1, ROUTE_E2, ROUTE_W1, ROUTE_W2, ROUTE_R1, ROUTE_R2 = range(6)

TOKEN_TILE = SUBLANES * LANES


def _store_token_tiles(ref, x):
    n = x.shape[0]
    for j in range(SUBLANES):
        ref[pl.ds(j, n, stride=SUBLANES), :] = x[:, j * LANES:(j + 1) * LANES]


def _load_token_tiles(ref):
    n = ref.shape[0] // SUBLANES
    return jnp.concatenate([ref[pl.ds(j, n, stride=SUBLANES), :] for j in range(SUBLANES)], axis=1)


def _router_kernel(n_experts, x_ref, mod_ref, g_ref, wr_ref, br_ref, h_ref, route_ref, count_ref, carry):
    @pl.when(pl.program_id(0) == 0)
    def _():
        carry[...] = jnp.zeros(carry.shape, F32)

    h = _norm_mod(x_ref[...], g_ref[...], mod_ref[0, 3:4, :], mod_ref[0, 4:5, :])
    _store_token_tiles(h_ref, h)
    logits = _dot_hi(h, wr_ref[...]) + br_ref[...]
    tm = logits.shape[0]
    lane = lax.broadcasted_iota(jnp.int32, logits.shape, 1)
    logits = jnp.where(lane < n_experts, logits, -jnp.inf)
    m1 = jnp.max(logits, axis=-1, keepdims=True)
    i1 = jnp.min(jnp.where(logits == m1, lane, LANES), axis=-1, keepdims=True)
    rest = jnp.where(lane == i1, -jnp.inf, logits)
    m2 = jnp.max(rest, axis=-1, keepdims=True)
    i2 = jnp.min(jnp.where(rest == m2, lane, LANES), axis=-1, keepdims=True)
    e2 = jnp.exp(m2 - m1)
    w1 = 1.0 / (1.0 + e2)
    w2 = e2 * w1

    sel1 = lane == i1
    sel2 = lane == i2
    onehot = jnp.where(jnp.logical_or(sel1, sel2), 1.0, 0.0)
    r = lax.broadcasted_iota(jnp.int32, (tm, tm), 0)
    c = lax.broadcasted_iota(jnp.int32, (tm, tm), 1)
    before = jnp.where(r > c, 1.0, 0.0).astype(BF16)
    rank = _dot(before, onehot.astype(BF16)) + carry[0:1, :]
    carry[...] = carry[...] + jnp.sum(onehot, axis=0, keepdims=True)
    count_ref[...] = carry[...]
    r1 = jnp.sum(jnp.where(sel1, rank, 0.0), axis=-1, keepdims=True)
    r2 = jnp.sum(jnp.where(sel2, rank, 0.0), axis=-1, keepdims=True)
    rec = jnp.zeros(logits.shape, F32)
    for pos, val in ((ROUTE_E1, i1.astype(F32)), (ROUTE_E2, i2.astype(F32)), (ROUTE_W1, w1), (ROUTE_W2, w2),
                     (ROUTE_R1, r1), (ROUTE_R2, r2)):
        rec = jnp.where(lane == pos, val, rec)
    route_ref[...] = rec


def _router_call(x, mod, g, wr, br, n_experts, seq):
    t, d = x.shape
    tm = _row_tile(seq, 512)
    per_seq = seq // tm
    row = lambda i: (i, 0)
    const = lambda i: (0, 0)
    return pl.pallas_call(
        functools.partial(_router_kernel, n_experts),
        grid=(t // tm,),
        in_specs=[
            pl.BlockSpec((tm, d), row),
            pl.BlockSpec((1, 6, d), lambda i: (i // per_seq, 0, 0)),
            pl.BlockSpec((1, d), const),
            pl.BlockSpec((d, LANES), const),
            pl.BlockSpec((1, LANES), const),
        ],
        out_specs=[pl.BlockSpec((tm * SUBLANES, LANES), row), pl.BlockSpec((tm, LANES), row),
                   pl.BlockSpec((SUBLANES, LANES), const)],
        out_shape=[jax.ShapeDtypeStruct((t * SUBLANES, LANES), F32), jax.ShapeDtypeStruct((t, LANES), F32),
                   jax.ShapeDtypeStruct((SUBLANES, LANES), F32)],
        scratch_shapes=[pltpu.VMEM((SUBLANES, LANES), F32)],
        compiler_params=_params(1),
        name="prenorm_router",
    )(x, mod, g, wr, br)


MOE_ROWS = 1024
COMBINE_ROWS = 512
FFN_DENSE_TILE_CAP = 1536
FFN_MOE_TILE_CAP = 512


def _ffn_tile(d_ff, cap):
    best = None
    for tf in range(LANES, min(cap, d_ff) + 1, LANES):
        if d_ff % tf == 0:
            best = tf
    if best is None:
        raise ValueError(d_ff)
    return best


def _tile_gather_start(row_ref, src_hbm, dst_ref, sem, queues=(0, 1)):
    def issue(i, carry):
        for k in range(SUBLANES):
            r = i * SUBLANES + k
            src_row = pl.multiple_of(row_ref[r], SUBLANES)
            dst_row = pl.multiple_of(r * SUBLANES, SUBLANES)
            pltpu.make_async_copy(src_hbm.at[pl.ds(src_row, SUBLANES)], dst_ref.at[pl.ds(dst_row, SUBLANES)],
                                  sem).start(priority=queues[k % len(queues)])
        return carry

    lax.fori_loop(0, dst_ref.shape[0] // (SUBLANES * SUBLANES), issue, 0)


def _tile_gather_wait(src_hbm, dst_ref, sem):
    pltpu.make_async_copy(src_hbm.at[pl.ds(0, dst_ref.shape[0])], dst_ref, sem).wait()


def _moe_ffn_kernel(te_ref, nu_ref, rows_ref, rows_next_ref, h_hbm, w1_ref, w3_ref, w2_ref, ys_ref,
                    xbuf, sem, hb_scr, acc_ref):
    i = pl.program_id(0)
    f = pl.program_id(1)
    last = pl.num_programs(1) - 1
    n_used = nu_ref[0]
    used = i < n_used
    slot = lax.rem(i, 2)

    @pl.when(jnp.logical_and(f == 0, jnp.logical_and(i == 0, used)))
    def _():
        _tile_gather_start(rows_ref, h_hbm, xbuf.at[0], sem.at[0])

    @pl.when(jnp.logical_and(f == 0, i + 1 < n_used))
    def _():
        _tile_gather_start(rows_next_ref, h_hbm, xbuf.at[1 - slot], sem.at[1 - slot], queues=(1,))

    @pl.when(jnp.logical_and(used, f == 0))
    def _():
        _tile_gather_wait(h_hbm, xbuf.at[slot], sem.at[slot])
        hb_scr[...] = _load_token_tiles(xbuf.at[slot]).astype(BF16)
        acc_ref[...] = jnp.zeros(acc_ref.shape, F32)

    @pl.when(used)
    def _():
        h = hb_scr[...]
        g = _silu(_dot(h, w1_ref[0])) * _dot(h, w3_ref[0])
        acc_ref[...] += _bdot(g, w2_ref[0])

    @pl.when(jnp.logical_and(used, f == last))
    def _():
        _store_token_tiles(ys_ref, acc_ref[...])

    @pl.when(jnp.logical_and(jnp.logical_not(used), f == last))
    def _():
        ys_ref[...] = jnp.zeros(ys_ref.shape, F32)


def _moe_ffn_call(tile_expert, n_used, tok, h_tiles, w1, w3, w2):
    p = tok.shape[0]
    n_tiles = p // MOE_ROWS
    d = TOKEN_TILE
    d_ff = w1.shape[2]
    tf = _ffn_tile(d_ff, FFN_MOE_TILE_CAP)
    rows = tok * SUBLANES
    grid_spec = pltpu.PrefetchScalarGridSpec(
        num_scalar_prefetch=2,
        grid=(n_tiles, d_ff // tf),
        in_specs=[
            pl.BlockSpec((MOE_ROWS,), lambda i, f, te, nu: (i,), memory_space=pltpu.SMEM),
            pl.BlockSpec((MOE_ROWS,), lambda i, f, te, nu: (jnp.minimum(i + 1, n_tiles - 1),),
                         memory_space=pltpu.SMEM),
            pl.BlockSpec(memory_space=pl.ANY),
            pl.BlockSpec((1, d, tf), lambda i, f, te, nu: (te[i], 0, f)),
            pl.BlockSpec((1, d, tf), lambda i, f, te, nu: (te[i], 0, f)),
            pl.BlockSpec((1, tf, d), lambda i, f, te, nu: (te[i], f, 0)),
        ],
        out_specs=pl.BlockSpec((MOE_ROWS * SUBLANES, LANES), lambda i, f, te, nu: (i, 0)),
        scratch_shapes=[
            pltpu.VMEM((2, MOE_ROWS * SUBLANES, LANES), F32),
            pltpu.SemaphoreType.DMA((2,)),
            pltpu.VMEM((MOE_ROWS, d), BF16),
            pltpu.VMEM((MOE_ROWS, d), F32),
        ],
    )
    return pl.pallas_call(
        _moe_ffn_kernel,
        grid_spec=grid_spec,
        out_shape=jax.ShapeDtypeStruct((p * SUBLANES, LANES), F32),
        compiler_params=_params(2),
        name="moe_ffn",
    )(tile_expert, n_used, rows, rows, h_tiles, w1, w3, w2)


def _moe_combine_kernel(final, r1_ref, r2_ref, r1_next_ref, r2_next_ref, ys_hbm, route_ref, x_ref, mod_ref,
                        gf_ref, o_ref, buf, sem):
    i = pl.program_id(0)
    slot = lax.rem(i, 2)

    @pl.when(i == 0)
    def _():
        _tile_gather_start(r1_ref, ys_hbm, buf.at[0, 0], sem.at[0, 0])
        _tile_gather_start(r2_ref, ys_hbm, buf.at[0, 1], sem.at[0, 1])

    @pl.when(i + 1 < pl.num_programs(0))
    def _():
        _tile_gather_start(r1_next_ref, ys_hbm, buf.at[1 - slot, 0], sem.at[1 - slot, 0])
        _tile_gather_start(r2_next_ref, ys_hbm, buf.at[1 - slot, 1], sem.at[1 - slot, 1])

    _tile_gather_wait(ys_hbm, buf.at[slot, 0], sem.at[slot, 0])
    _tile_gather_wait(ys_hbm, buf.at[slot, 1], sem.at[slot, 1])
    route = route_ref[...]
    f = (route[:, ROUTE_W1:ROUTE_W1 + 1] * _load_token_tiles(buf.at[slot, 0])
         + route[:, ROUTE_W2:ROUTE_W2 + 1] * _load_token_tiles(buf.at[slot, 1]))
    y = x_ref[...] + mod_ref[0, 5:6, :] * f
    if final:
        ms = jnp.mean(y * y, axis=-1, keepdims=True)
        y = y * lax.rsqrt(ms + EPS) * gf_ref[...]
    o_ref[...] = y


def _moe_combine_call(d1, d2, ys, route, x, mod, g_final, seq, final):
    t, d = x.shape
    tm = _row_tile(seq, COMBINE_ROWS)
    per_seq = seq // tm
    n_steps = t // tm
    row = lambda i: (i, 0)
    idx_spec = pl.BlockSpec((tm,), lambda i: (i,), memory_space=pltpu.SMEM)
    next_spec = pl.BlockSpec((tm,), lambda i: (jnp.minimum(i + 1, n_steps - 1),), memory_space=pltpu.SMEM)
    rows1, rows2 = d1 * SUBLANES, d2 * SUBLANES
    return pl.pallas_call(
        functools.partial(_moe_combine_kernel, final),
        grid=(n_steps,),
        in_specs=[
            idx_spec, idx_spec, next_spec, next_spec,
            pl.BlockSpec(memory_space=pl.ANY),
            pl.BlockSpec((tm, LANES), row),
            pl.BlockSpec((tm, d), row),
            pl.BlockSpec((1, 6, d), lambda i: (i // per_seq, 0, 0)),
            pl.BlockSpec((1, d), lambda i: (0, 0)),
        ],
        out_specs=pl.BlockSpec((tm, d), row),
        out_shape=jax.ShapeDtypeStruct((t, d), F32),
        scratch_shapes=[pltpu.VMEM((2, 2, tm * SUBLANES, LANES), F32), pltpu.SemaphoreType.DMA((2, 2))],
        compiler_params=_params(1),
        name="moe_combine",
    )(rows1, rows2, rows1, rows2, ys, route, x, mod, g_final)


INV_PERM_ROWS = 4096


def _inv_perm_kernel(d1_ref, d2_ref, tok_ref):
    i = pl.program_id(0)
    tm = d1_ref.shape[0]

    @pl.when(i == 0)
    def _():
        def clear(j, carry):
            tok_ref[j] = 0
            return carry

        lax.fori_loop(0, tok_ref.shape[0], clear, 0, unroll=8)

    base = i * tm

    def place(r, carry):
        tok_ref[d1_ref[r]] = base + r
        tok_ref[d2_ref[r]] = base + r
        return carry

    lax.fori_loop(0, tm, place, 0, unroll=8)


def _inv_perm_call(d1, d2, p):
    t = d1.shape[0]
    tm = _row_tile(t, INV_PERM_ROWS)
    idx_spec = pl.BlockSpec((tm,), lambda i: (i,), memory_space=pltpu.SMEM)
    return pl.pallas_call(
        _inv_perm_kernel,
        grid=(t // tm,),
        in_specs=[idx_spec, idx_spec],
        out_specs=pl.BlockSpec(memory_space=pltpu.SMEM),
        out_shape=jax.ShapeDtypeStruct((p,), jnp.int32),
        compiler_params=_params(1),
        name="moe_inverse_permutation",
    )(d1, d2)


def _moe_layout(route, counts, n_experts, t):
    cnt = counts[0, :n_experts].astype(jnp.int32)
    padded = ((cnt + MOE_ROWS - 1) // MOE_ROWS) * MOE_ROWS
    ends = jnp.cumsum(padded)
    offs = ends - padded
    e1 = route[:, ROUTE_E1].astype(jnp.int32)
    e2 = route[:, ROUTE_E2].astype(jnp.int32)
    d1 = offs[e1] + route[:, ROUTE_R1].astype(jnp.int32)
    d2 = offs[e2] + route[:, ROUTE_R2].astype(jnp.int32)
    p = (TOP_K * t // MOE_ROWS + n_experts) * MOE_ROWS
    tok = _inv_perm_call(d1, d2, p)
    starts = jnp.arange(p // MOE_ROWS, dtype=jnp.int32) * MOE_ROWS
    tile_expert = jnp.minimum(jnp.searchsorted(ends, starts, side="right"), n_experts - 1).astype(jnp.int32)
    n_used = (ends[-1:] // MOE_ROWS).astype(jnp.int32)
    return d1, d2, tok, tile_expert, n_used


def _ffn_kernel(x_ref, mod_ref, g_ref, w1_ref, w3_ref, w2_ref, o_ref):
    x = x_ref[...]
    h = _norm_mod(x, g_ref[...], mod_ref[0, 3:4, :], mod_ref[0, 4:5, :]).astype(BF16)
    d_ff = w1_ref.shape[1]
    tf = _ffn_tile(d_ff, FFN_DENSE_TILE_CAP)
    acc = None
    for f0 in range(0, d_ff, tf):
        g = _silu(_dot(h, w1_ref[:, f0:f0 + tf])) * _dot(h, w3_ref[:, f0:f0 + tf])
        part = _bdot(g, w2_ref[f0:f0 + tf, :])
        acc = part if acc is None else acc + part
    o_ref[...] = x + mod_ref[0, 5:6, :] * acc


def _ffn_call(x, mod, g, w1, w3, w2, seq):
    t, d = x.shape
    d_ff = w1.shape[1]
    tm = _row_tile(seq, 512)
    per_seq = seq // tm
    row = lambda i: (i, 0)
    const = lambda i: (0, 0)
    return pl.pallas_call(
        _ffn_kernel,
        grid=(t // tm,),
        in_specs=[
            pl.BlockSpec((tm, d), row),
            pl.BlockSpec((1, 6, d), lambda i: (i // per_seq, 0, 0)),
            pl.BlockSpec((1, d), const),
            pl.BlockSpec((d, d_ff), const, **_resident()),
            pl.BlockSpec((d, d_ff), const, **_resident()),
            pl.BlockSpec((d_ff, d), const, **_resident()),
        ],
        out_specs=pl.BlockSpec((tm, d), row),
        out_shape=jax.ShapeDtypeStruct((t, d), F32),
        compiler_params=_params(1),
        name="ffn_dense",
    )(x, mod, g, w1, w3, w2)


def _final_kernel(x_ref, g_ref, o_ref):
    x = x_ref[...]
    ms = jnp.mean(x * x, axis=-1, keepdims=True)
    o_ref[...] = x * lax.rsqrt(ms + EPS) * g_ref[...]


def _final_call(x, g):
    t, d = x.shape
    tm = _row_tile(t, 512)
    return pl.pallas_call(
        _final_kernel,
        grid=(t // tm,),
        in_specs=[pl.BlockSpec((tm, d), lambda i: (i, 0)), pl.BlockSpec((1, d), lambda i: (0, 0))],
        out_specs=pl.BlockSpec((tm, d), lambda i: (i, 0)),
        out_shape=jax.ShapeDtypeStruct((t, d), F32),
        compiler_params=_params(1),
        name="final_norm",
    )(x, g)


def _lane_row(values, lane0, n_rows=1, row=0):
    out = jnp.zeros((n_rows, LANES), F32)
    return out.at[row, lane0:lane0 + values.shape[0]].set(values.astype(F32))


def _split_w_in(w_in):
    qk_a, v_a = N_HEADS * DK_A, N_HEADS * DV
    qk_b, qk_c, vw = N_HEADS * DK_B, N_HEADS * DK_C, N_HEADS * DV
    sizes = (qk_a, qk_a, v_a, v_a, N_HEADS, N_HEADS,
             qk_b, qk_b, vw, vw, N_HEADS, N_HEADS,
             qk_c, qk_c, vw, vw, GLA_RANK)
    names = ("q_a", "k_a", "v_a", "z_a", "a_a", "b_a", "q_b", "k_b", "v_b", "o_b", "i_b", "f_b",
             "q_c", "k_c", "v_c", "z_c", "g_c")
    assert sum(sizes) == w_in.shape[1]
    cols, off = {}, 0
    for name, size in zip(names, sizes):
        cols[name] = w_in[:, off:off + size]
        off += size
    big = jnp.concatenate([cols[n] for n in ("q_a", "k_a", "v_a", "z_a", "q_b", "k_b", "v_b", "o_b",
                                             "q_c", "k_c", "v_c", "z_c")], axis=1).astype(BF16)
    small = jnp.concatenate([cols[n] for n in ("a_a", "b_a", "i_b", "f_b", "g_c")], axis=1)
    small = jnp.pad(small, ((0, 0), (0, LANES - small.shape[1])))
    s_hi = small.astype(BF16)
    s_lo = (small - s_hi.astype(F32)).astype(BF16)
    return big, s_hi, s_lo


def kernel(x, c, w_ada, b_ada, g_mix, g_ffn, g_final, w_in, conv_a, a_log, dt_bias, norm_a, conv_b, b_i, b_f,
           norm_b, w_gla2, b_gla, norm_c, w_br, w_mg, b_mg, w_o, w1_d, w3_d, w2_d, w_router, b_router,
           w1_e, w3_e, w2_e):
    bsz, seq, d = x.shape
    depth = w_ada.shape[0]
    assert seq % MIX_ROWS == 0 and d % LANES == 0
    t = bsz * seq
    xf = x.reshape(t, d)
    mod_all = _ada_call(c, w_ada, b_ada).reshape(depth, bsz, 6, d)

    for layer in range(depth):
        mod = mod_all[layer]
        w_big, ws_hi, ws_lo = _split_w_in(w_in[layer])
        (h, qkv_a, z_a, qk_b, v_b, o_b, qk_c, v_c, z_c, small) = _inproj_call(
            xf, mod, g_mix[layer].reshape(1, d), w_big, ws_hi, ws_lo, seq)

        pa = (_lane_row(-jnp.exp(a_log[layer]), LANE_A_DECAY, SUBLANES, 0)
              + _lane_row(dt_bias[layer], LANE_A_DECAY, SUBLANES, 1))
        y_a = _mixer_a_call(qkv_a, z_a, small, conv_a[layer], pa, norm_a[layer].reshape(1, DV), bsz, seq)

        pb = (_lane_row(b_i[layer], LANE_B_IN, SUBLANES, 0) + _lane_row(b_f[layer], LANE_B_FORGET, SUBLANES, 0))
        y_b = _mixer_b_call(qk_b, v_b, o_b, small, conv_b[layer], pb, norm_b[layer].reshape(1, DV), bsz, seq)

        wg = jnp.zeros((LANES, N_HEADS * DK_C), F32).at[LANE_C_GATE:LANE_C_GATE + GLA_RANK].set(w_gla2[layer])
        y_c = _mixer_c_call(qk_c, v_c, z_c, small, wg, b_gla[layer].reshape(1, -1),
                            norm_c[layer].reshape(1, DV), bsz, seq)

        xf = _merge_call(xf, h, y_a, y_b, y_c, mod, w_mg[layer].astype(BF16), b_mg[layer][:, None, :],
                         w_br[layer].astype(BF16), w_o[layer].astype(BF16), seq)

        j = layer // 2
        last = layer == depth - 1
        if layer % 2 == 0:
            xf = _ffn_call(xf, mod, g_ffn[layer].reshape(1, d), w1_d[j].astype(BF16), w3_d[j].astype(BF16),
                           w2_d[j].astype(BF16), seq)
            if last:
                xf = _final_call(xf, g_final.reshape(1, d))
        else:
            n_experts = w_router.shape[2]
            wr = jnp.pad(w_router[j], ((0, 0), (0, LANES - n_experts)))
            br = jnp.pad(b_router[j], (0, LANES - n_experts)).reshape(1, LANES)
            assert d == TOKEN_TILE, "gathered rows are stored as one (SUBLANES, LANES) tile per token"
            h2, route, counts = _router_call(xf, mod, g_ffn[layer].reshape(1, d), wr, br, n_experts, seq)
            d1, d2, tok, tile_expert, n_used = _moe_layout(route, counts, n_experts, t)
            ys = _moe_ffn_call(tile_expert, n_used, tok, h2, w1_e[j].astype(BF16), w3_e[j].astype(BF16),
                               w2_e[j].astype(BF16))
            xf = _moe_combine_call(d1, d2, ys, route, xf, mod, g_final.reshape(1, d), seq, last)

    return xf.reshape(bsz, seq, d)
```
